```python
import jax
import jax.numpy as jnp
from jax import lax
import numpy as np

D_MODEL = 1024
BATCH = 8
SEQ = 8192
DEPTH = 2

GRID_W = 64
CTX_LEN = 256
N_MIXERS = 2
N_RWKV = (DEPTH + 1) // 2
N_GLA = DEPTH // 2
NORM_EPS = 1e-6

RWKV_HEAD = 64
RWKV_HEADS = D_MODEL // RWKV_HEAD
DECAY_LORA = 64
ICLR_LORA = 64
GATE_LORA = 128
RWKV_GN_EPS = 64e-5

GLA_HEADS = 4
GLA_QK = D_MODEL // 2
GLA_DK = GLA_QK // GLA_HEADS
GLA_DV = D_MODEL // GLA_HEADS
GLA_GATE_RANK = 16
GLA_GATE_NORM = 16.0
GLA_CHUNK = 64
GLA_CONV_CH = 2 * GLA_QK + D_MODEL
GLA_IN = GLA_CONV_CH + D_MODEL + 2 * GLA_GATE_RANK
CONV_K = 3

PEER_HEADS = 8
N_KEYS = 128
N_EXPERTS = N_KEYS * N_KEYS
PEER_QDIM = 256
PEER_TOPK = 16
PEER_BLOCK = 128

kernel_name = "hybrid_rwkv7_gla_peer_dit"


def rms_norm(x, g, eps=NORM_EPS):
    xf = x.astype(jnp.float32)
    y = xf * lax.rsqrt(jnp.mean(jnp.square(xf), axis=-1, keepdims=True) + eps)
    return (y * g.astype(jnp.float32)).astype(x.dtype)


def ada_modulate(x, g, shift, scale):
    return rms_norm(x, g) * (1 + scale) + shift


def _flip(t):
    return jnp.flip(t, axis=1)


def _keep(t):
    return t


def centred_shift(x):
    z = jnp.zeros_like(x[:, :1])
    prev = jnp.concatenate([z, x[:, :-1]], axis=1)
    nxt = jnp.concatenate([x[:, 1:], z], axis=1)
    return 0.5 * (prev + nxt)


def rwkv7_project(h, p):
    B, T, _ = h.shape
    hd = (B, T, RWKV_HEADS, RWKV_HEAD)
    xx = centred_shift(h) - h
    xr, xw, xk, xv, xa, xg = (h + xx * p["mix"][j] for j in range(6))
    r, k, v = jnp.einsum("nbtd,nde->nbte", jnp.stack([xr, xk, xv]), p["w_rkv"])
    g = jax.nn.sigmoid(xg @ p["g1"]) @ p["g2"]
    kk = (k * p["k_k"]).reshape(hd).astype(jnp.float32)
    kk = kk / jnp.maximum(jnp.sqrt(jnp.sum(jnp.square(kk), axis=-1, keepdims=True)), 1e-12)
    decay, k_dir, b_dir = [], [], []
    for d in range(2):
        pre = (p["w0"][d] + jnp.tanh(xw @ p["w1"][d]) @ p["w2"][d]).astype(jnp.float32)
        w_log = -jax.nn.softplus(-pre) - 0.5
        decay.append(jnp.exp(-jnp.exp(w_log)).reshape(hd))
        a = jax.nn.sigmoid(p["a0"][d] + (xa @ p["a1"][d]) @ p["a2"][d])
        k_dir.append((k * (1 + (a - 1) * p["k_a"])).reshape(hd))
        b_dir.append(kk * a.reshape(hd).astype(jnp.float32))
    return dict(r=r.reshape(hd), v=v.reshape(hd), g=g, kk=kk, decay=decay, k=k_dir, b=b_dir)


def _rwkv7_step_inputs(P, d):
    return (P["r"], P["decay"][d], P["k"][d], P["v"], P["kk"], P["b"][d])


def rwkv7_scan(r, w, k, v, kk, b, s0):
    def step(s, inp):
        r_t, w_t, k_t, v_t, kk_t, b_t = inp
        sa = jnp.einsum("bhvk,bhk->bhv", s, kk_t)
        s = s * w_t[:, :, None, :] - sa[..., None] * b_t[:, :, None, :] + v_t[..., None] * k_t[:, :, None, :]
        y = jnp.einsum("bhvk,bhk->bhv", s, r_t)
        return s, y
    xs = tuple(jnp.moveaxis(t.astype(jnp.float32), 1, 0) for t in (r, w, k, v, kk, b))
    s, y = lax.scan(step, s0, xs)
    return jnp.moveaxis(y, 0, 1), s


def rwkv7_readout(y, P, p):
    B, T = y.shape[:2]
    mu = jnp.mean(y, axis=-1, keepdims=True)
    var = jnp.mean(jnp.square(y - mu), axis=-1, keepdims=True)
    yn = ((y - mu) * lax.rsqrt(var + RWKV_GN_EPS)).reshape(B, T, D_MODEL)
    yn = yn.astype(P["v"].dtype) * p["ln_w"] + p["ln_b"]
    bonus = (jnp.sum(P["r"] * P["k"][0] * p["r_k"], axis=-1, keepdims=True) * P["v"]
             + jnp.sum(P["r"] * P["k"][1] * p["r_k"], axis=-1, keepdims=True) * P["v"])
    return ((yn + bonus.reshape(B, T, D_MODEL)) * P["g"]) @ p["w_o"]


def rwkv7_mixer(h_lat, h_ctx, p, ctx_out):
    lat = rwkv7_project(h_lat, p)
    ctx = rwkv7_project(h_ctx, p)
    B = h_lat.shape[0]
    s0 = jnp.zeros((B, RWKV_HEADS, RWKV_HEAD, RWKV_HEAD), jnp.float32)
    y_lat, y_ctx = [], []
    for d in range(2):
        f = _flip if d else _keep
        yc, sc = rwkv7_scan(*[f(t) for t in _rwkv7_step_inputs(ctx, d)], s0)
        yl, _ = rwkv7_scan(*[f(t) for t in _rwkv7_step_inputs(lat, d)], sc)
        y_lat.append(f(yl))
        y_ctx.append(f(yc))
    out_lat = rwkv7_readout(y_lat[0] + y_lat[1], lat, p)
    out_ctx = rwkv7_readout(y_ctx[0] + y_ctx[1], ctx, p) if ctx_out else None
    return out_lat, out_ctx


def depthwise_conv2d(x, w):
    return lax.conv_general_dilated(x, w, (1, 1), "SAME",
                                    dimension_numbers=("NHWC", "HWIO", "NHWC"),
                                    feature_group_count=x.shape[-1])


def gla_project(h, p, rows):
    B, T, _ = h.shape
    z = h @ p["w_in"]
    qkv = z[..., :GLA_CONV_CH]
    if rows is None:
        qkv = depthwise_conv2d(qkv[:, None], p["conv"][1:2])[:, 0]
    else:
        qkv = depthwise_conv2d(qkv.reshape(B, rows, GRID_W, GLA_CONV_CH), p["conv"]).reshape(B, T, GLA_CONV_CH)
    qkv = jax.nn.silu(qkv)
    hk = (B, T, GLA_HEADS, GLA_DK)
    q = qkv[..., :GLA_QK].reshape(hk) * (GLA_DK ** -0.5)
    k = qkv[..., GLA_QK:2 * GLA_QK].reshape(hk)
    v = qkv[..., 2 * GLA_QK:].reshape(B, T, GLA_HEADS, GLA_DV)
    g = z[..., GLA_CONV_CH:GLA_CONV_CH + D_MODEL]
    lr = z[..., GLA_CONV_CH + D_MODEL:].reshape(B, T, 2, GLA_GATE_RANK)
    la = [(jax.nn.log_sigmoid((lr[:, :, d] @ p["gk_up"][d] + p["gk_b"][d]).astype(jnp.float32))
           / GLA_GATE_NORM).reshape(hk) for d in range(2)]
    return q, k, v, g, la


def gla_chunked(q, k, v, la, s0):
    B, T, H, K = q.shape
    C = GLA_CHUNK
    n = T // C

    def blk(t):
        return t.astype(jnp.float32).reshape(B, n, C, H, t.shape[-1]).transpose(1, 0, 3, 2, 4)

    q, k, v, la = blk(q), blk(k), blk(v), blk(la)
    bcum = jnp.cumsum(la, axis=3)
    ref = bcum[:, :, :, C // 2:C // 2 + 1]
    A = jnp.einsum("nbhik,nbhjk->nbhij", q * jnp.exp(bcum - ref), k * jnp.exp(ref - bcum))
    A = jnp.where(jnp.tril(jnp.ones((C, C), dtype=bool)), A, 0.0)
    o_intra = jnp.einsum("nbhij,nbhjv->nbhiv", A, v)
    b_last = bcum[:, :, :, -1:]
    q_dec = q * jnp.exp(bcum)
    k_dec = k * jnp.exp(b_last - bcum)
    d_last = jnp.exp(b_last[:, :, :, 0])

    def step(s, inp):
        qd, kd, vc, dl = inp
        o = jnp.einsum("bhck,bhkv->bhcv", qd, s)
        s = dl[..., None] * s + jnp.einsum("bhck,bhcv->bhkv", kd, vc)
        return s, o

    s, o_inter = lax.scan(step, s0, (q_dec, k_dec, v, d_last))
    o = o_intra + o_inter
    return o.transpose(1, 0, 3, 2, 4).reshape(B, T, H, v.shape[-1]), s


def gla_readout(o, g, p):
    B, T = o.shape[:2]
    on = o * lax.rsqrt(jnp.mean(jnp.square(o), axis=-1, keepdims=True) + NORM_EPS) * p["head_norm"].astype(jnp.float32)
    gate = jax.nn.silu(g).reshape(B, T, GLA_HEADS, GLA_DV)
    return (on.astype(g.dtype) * gate).reshape(B, T, D_MODEL) @ p["w_o"]


def gla_mixer(h_lat, h_ctx, p, rows, ctx_out):
    lq, lk, lv, lg, lla = gla_project(h_lat, p, rows)
    cq, ck, cv, cg, cla = gla_project(h_ctx, p, None)
    B = h_lat.shape[0]
    s0 = jnp.zeros((B, GLA_HEADS, GLA_DK, GLA_DV), jnp.float32)
    o_lat, o_ctx = [], []
    for d in range(2):
        f = _flip if d else _keep
        oc, sc = gla_chunked(f(cq), f(ck), f(cv), f(cla[d]), s0)
        ol, _ = gla_chunked(f(lq), f(lk), f(lv), f(lla[d]), sc)
        o_lat.append(f(ol))
        o_ctx.append(f(oc))
    out_lat = gla_readout(o_lat[0] + o_lat[1], lg, p)
    out_ctx = gla_readout(o_ctx[0] + o_ctx[1], cg, p) if ctx_out else None
    return out_lat, out_ctx


def peer(h, wq, sub_keys, u, v):
    B, T, D = h.shape
    xs = h.reshape(-1, PEER_BLOCK, D)

    def block(xb):
        q = (xb @ wq).reshape(PEER_BLOCK, PEER_HEADS, 2, PEER_QDIM // 2)
        s = jnp.einsum("thpc,hpnc->thpn", q, sub_keys).astype(jnp.float32)
        sv, si = lax.top_k(s, PEER_TOPK)
        cand = sv[:, :, 0, :, None] + sv[:, :, 1, None, :]
        cand_idx = si[:, :, 0, :, None] * N_KEYS + si[:, :, 1, None, :]
        top_s, top_pos = lax.top_k(cand.reshape(PEER_BLOCK, PEER_HEADS, PEER_TOPK * PEER_TOPK), PEER_TOPK)
        idx = jnp.take_along_axis(cand_idx.reshape(PEER_BLOCK, PEER_HEADS, PEER_TOPK * PEER_TOPK), top_pos, axis=-1)
        gate = jax.nn.softmax(top_s, axis=-1).astype(xb.dtype)
        ue = jnp.take(u, idx, axis=0)
        ve = jnp.take(v, idx, axis=0)
        act = jax.nn.gelu(jnp.einsum("thkd,td->thk", ue, xb), approximate=False)
        return jnp.einsum("thk,thkd->td", gate * act, ve)

    return lax.map(block, xs).reshape(B, T, D)


def setup_inputs(seed: int = 0) -> dict:
    key = jax.random.key(seed)
    ks = iter(jax.random.split(key, 48))
    f32 = jnp.float32

    def nrm(shape, scale):
        return jax.random.normal(next(ks), shape, f32) * scale

    def unif(shape, lo, hi):
        return jax.random.uniform(next(ks), shape, f32, lo, hi)

    D = D_MODEL
    NR, NG = N_RWKV, N_GLA
    inv = D ** -0.5
    return {
        "x": nrm((BATCH, SEQ, D), 1.0),
        "c": nrm((BATCH, D), 1.0),
        "ctx": nrm((BATCH, CTX_LEN, D), 1.0),
        "c_ctx": nrm((D,), 1.0),
        "ada_w": nrm((DEPTH, D, 6 * D), 0.5 * inv),
        "ada_b": nrm((DEPTH, 6 * D), 0.01),
        "norm_mix": 1.0 + nrm((DEPTH, D), 0.01),
        "norm_ffn": 1.0 + nrm((DEPTH, D), 0.01),
        "rw_mix": unif((NR, 6, D), 0.0, 1.0),
        "rw_w_rkv": nrm((NR, 3, D, D), inv),
        "rw_w0": unif((NR, 2, D), -6.0, -1.0),
        "rw_w1": nrm((NR, 2, D, DECAY_LORA), inv),
        "rw_w2": nrm((NR, 2, DECAY_LORA, D), 0.1 * DECAY_LORA ** -0.5),
        "rw_a0": nrm((NR, 2, D), 0.1),
        "rw_a1": nrm((NR, 2, D, ICLR_LORA), inv),
        "rw_a2": nrm((NR, 2, ICLR_LORA, D), 0.1 * ICLR_LORA ** -0.5),
        "rw_g1": nrm((NR, D, GATE_LORA), inv),
        "rw_g2": nrm((NR, GATE_LORA, D), GATE_LORA ** -0.5),
        "rw_k_k": 0.85 + nrm((NR, D), 0.02),
        "rw_k_a": 1.0 + nrm((NR, D), 0.02),
        "rw_r_k": nrm((NR, RWKV_HEADS, RWKV_HEAD), 0.1),
        "rw_ln_w": 1.0 + nrm((NR, D), 0.01),
        "rw_ln_b": nrm((NR, D), 0.01),
        "rw_w_o": nrm((NR, D, D), inv),
        "gla_w_in": nrm((NG, D, GLA_IN), inv),
        "gla_conv": nrm((NG, CONV_K, CONV_K, 1, GLA_CONV_CH), 1.0 / CONV_K),
        "gla_gk_up": nrm((NG, 2, GLA_GATE_RANK, GLA_QK), GLA_GATE_RANK ** -0.5),
        "gla_gk_b": nrm((NG, 2, GLA_QK), 0.1),
        "gla_head_norm": 1.0 + nrm((NG, GLA_DV), 0.01),
        "gla_w_o": nrm((NG, D, D), inv),
        "peer_wq": nrm((DEPTH, D, PEER_HEADS * PEER_QDIM), inv),
        "peer_keys": nrm((DEPTH, PEER_HEADS, 2, N_KEYS, PEER_QDIM // 2), (PEER_QDIM // 2) ** -0.5),
        "peer_u": nrm((DEPTH, N_EXPERTS, D), inv),
        "peer_v": nrm((DEPTH, N_EXPERTS, D), 1.0),
        "final_norm": 1.0 + nrm((D,), 0.01),
    }


def reference(x, c, ctx, c_ctx, ada_w, ada_b, norm_mix, norm_ffn,
              rw_mix, rw_w_rkv, rw_w0, rw_w1, rw_w2, rw_a0, rw_a1, rw_a2, rw_g1, rw_g2,
              rw_k_k, rw_k_a, rw_r_k, rw_ln_w, rw_ln_b, rw_w_o,
              gla_w_in, gla_conv, gla_gk_up, gla_gk_b, gla_head_norm, gla_w_o,
              peer_wq, peer_keys, peer_u, peer_v, final_norm):
    rows = x.shape[1] // GRID_W
    ctx_s = ctx
    silu_c = jax.nn.silu(c)
    silu_cc = jax.nn.silu(c_ctx)
    for i in range(DEPTH):
        last = i == DEPTH - 1
        mod_l = jnp.split((silu_c @ ada_w[i] + ada_b[i])[:, None, :], 6, axis=-1)
        mod_c = jnp.split(silu_cc @ ada_w[i] + ada_b[i], 6, axis=-1)
        h_l = ada_modulate(x, norm_mix[i], mod_l[0], mod_l[1])
        h_c = ada_modulate(ctx_s, norm_mix[i], mod_c[0], mod_c[1])
        j = i // N_MIXERS
        if i % N_MIXERS == 0:
            p = dict(mix=rw_mix[j], w_rkv=rw_w_rkv[j], w0=rw_w0[j], w1=rw_w1[j], w2=rw_w2[j],
                     a0=rw_a0[j], a1=rw_a1[j], a2=rw_a2[j], g1=rw_g1[j], g2=rw_g2[j],
                     k_k=rw_k_k[j], k_a=rw_k_a[j], r_k=rw_r_k[j], ln_w=rw_ln_w[j], ln_b=rw_ln_b[j],
                     w_o=rw_w_o[j])
            y_l, y_c = rwkv7_mixer(h_l, h_c, p, not last)
        else:
            p = dict(w_in=gla_w_in[j], conv=gla_conv[j], gk_up=gla_gk_up[j], gk_b=gla_gk_b[j],
                     head_norm=gla_head_norm[j], w_o=gla_w_o[j])
            y_l, y_c = gla_mixer(h_l, h_c, p, rows, not last)
        x = x + mod_l[2] * y_l
        x = x + mod_l[5] * peer(ada_modulate(x, norm_ffn[i], mod_l[3], mod_l[4]),
                                peer_wq[i], peer_keys[i], peer_u[i], peer_v[i])
        if not last:
            ctx_s = ctx_s + mod_c[2] * y_c
            ctx_s = ctx_s + mod_c[5] * peer(ada_modulate(ctx_s, norm_ffn[i], mod_c[3], mod_c[4]),
                                            peer_wq[i], peer_keys[i], peer_u[i], peer_v[i])
    return rms_norm(x, final_norm)
```

```python
import functools

import numpy as np
import jax
import jax.numpy as jnp
from jax import lax
from jax.experimental import pallas as pl
from jax.experimental.pallas import tpu as pltpu

F32 = jnp.float32
BF16 = jnp.bfloat16

NORM_EPS = 1e-6
GRID_W = 64
RWKV_HEAD = 64
RWKV_GN_EPS = 64e-5
DECAY_LORA = 64
GLA_HEADS = 4
GLA_GATE_RANK = 16
GLA_GATE_NORM = 16.0
GLA_CHUNK = 64
PEER_HEADS = 8
N_KEYS = 128
PEER_TOPK = 16

LANES = 128
VMEM_LIMIT = 56 * 1024 * 1024

TB_PROJ = 128
TB_SCAN = 256
TB_ROUTE = 256
TB_DENSE = 384
EC_DENSE = 1024


def _params(n_axes):
    return pltpu.CompilerParams(
        dimension_semantics=("arbitrary",) * n_axes,
        vmem_limit_bytes=VMEM_LIMIT)


def _full(a):
    nd = a.ndim
    return pl.BlockSpec(a.shape, lambda *_: (0,) * nd)


def _sigmoid(x):
    return 1.0 / (1.0 + jnp.exp(-x))


def _silu(x):
    return x * _sigmoid(x)


def _log_sigmoid(x):
    return jnp.minimum(x, 0.0) - jnp.log(1.0 + jnp.exp(-jnp.abs(x)))


def _modnorm(x, g, shift, scale):
    ms = jnp.mean(x * x, axis=-1, keepdims=True)
    return x * lax.rsqrt(ms + NORM_EPS) * (g * (1.0 + scale)) + shift


def _split2(x):
    hi = x.astype(BF16)
    lo = (x - hi.astype(F32)).astype(BF16)
    return hi, lo


def _split3(x):
    x1 = x.astype(BF16)
    r1 = x - x1.astype(F32)
    x2 = r1.astype(BF16)
    x3 = (r1 - x2.astype(F32)).astype(BF16)
    return x1, x2, x3


def _dot(a, b):
    return jnp.dot(a, b, preferred_element_type=F32)


def _dot_nt(a, b):
    return lax.dot_general(a, b, (((1,), (1,)), ((), ())), preferred_element_type=F32)


def _dot_tn(a, b):
    return lax.dot_general(a, b, (((0,), (0,)), ((), ())), preferred_element_type=F32)


def _dot3(a_hi, a_lo, b_hi, b_lo):
    return _dot(a_hi, b_hi) + _dot(a_hi, b_lo) + _dot(a_lo, b_hi)


def _seg_sum(x, bd):
    outs = []
    for c in range(x.shape[-1] // LANES):
        hi, lo = _split2(x[:, c * LANES:(c + 1) * LANES])
        outs.append(_dot(hi, bd) + _dot(lo, bd))
    return jnp.concatenate(outs, axis=-1)


def _block_diag_ones(group):
    i = np.arange(LANES)
    return jnp.asarray((i[:, None] // group) == (i[None, :] // group), dtype=BF16)


def _mod_kernel(c_ref, w_ref, b_ref, o_ref):
    a = _silu(c_ref[...])
    o_ref[...] = jnp.dot(a, w_ref[...], preferred_element_type=F32,
                         precision=lax.Precision.HIGHEST) + b_ref[...]


def _modulation(cc, ada_w, ada_b):
    depth, d, n = ada_w.shape
    nt = 768
    return pl.pallas_call(
        _mod_kernel,
        grid=(depth, n // nt),
        in_specs=[pl.BlockSpec(cc.shape, lambda i, k: (0, 0)),
                  pl.BlockSpec((None, d, nt), lambda i, k: (i, 0, k)),
                  pl.BlockSpec((None, 1, nt), lambda i, k: (i, 0, k))],
        out_specs=pl.BlockSpec((None, cc.shape[0], nt), lambda i, k: (i, 0, k)),
        out_shape=jax.ShapeDtypeStruct((depth, cc.shape[0], n), F32),
        compiler_params=_params(2),
        name="modulation",
    )(cc, ada_w, ada_b.reshape(depth, 1, n))


def _mod_spec(nb, d, ncb):
    return pl.BlockSpec((None, 6, d), lambda b, j: (jnp.where(j < ncb, nb, b), 0, 0))


def _rw_proj_kernel(x_ref, xp_ref, xn_ref, mod_ref, ng_ref, mix_ref, wrkv_ref,
                    w1_ref, a1_ref, g1_ref, w2_ref, a2_ref, g2_ref, w0_ref, a0_ref,
                    kk_w_ref, ka_ref, rk_ref, bd_ref,
                    r_o, v_o, kk_o, g_o, bon_o, w0_o, k0_o, b0_o, w1_o, k1_o, b1_o,
                    *, ncb, nblk):
    j = pl.program_id(1)
    tb = x_ref.shape[0]
    shift = mod_ref[0:1, :]
    scale = mod_ref[1:2, :]
    g = ng_ref[...]
    h = _modnorm(x_ref[...], g, shift, scale)
    hp = _modnorm(xp_ref[7:8, :], g, shift, scale)
    hn = _modnorm(xn_ref[0:1, :], g, shift, scale)
    hp = jnp.where((j != 0) & (j != ncb), hp, 0.0)
    hn = jnp.where((j != ncb - 1) & (j != nblk - 1), hn, 0.0)
    row = lax.broadcasted_iota(jnp.int32, (tb, 1), 0)
    prev = jnp.where(row == 0, hp, pltpu.roll(h, 1, 0))
    nxt = jnp.where(row == tb - 1, hn, pltpu.roll(h, tb - 1, 0))
    xx = 0.5 * (prev + nxt) - h
    xr, xw, xk, xv, xa, xg = (h + xx * mix_ref[i:i + 1, :] for i in range(6))

    r = _dot(xr.astype(BF16), wrkv_ref[0])
    k = _dot(xk.astype(BF16), wrkv_ref[1])
    v = _dot(xv.astype(BF16), wrkv_ref[2])
    gate = _dot(_sigmoid(_dot(xg.astype(BF16), g1_ref[...])).astype(BF16), g2_ref[...])
    tw = jnp.tanh(_dot(xw.astype(BF16), w1_ref[...])).astype(BF16)
    ta = _dot(xa.astype(BF16), a1_ref[...]).astype(BF16)

    bd = bd_ref[...]
    kk = k * kk_w_ref[...]
    kk = kk / jnp.maximum(jnp.sqrt(_seg_sum(kk * kk, bd)), 1e-12)

    r_o[...] = r
    v_o[...] = v
    kk_o[...] = kk
    g_o[...] = gate
    ksum = None
    for d, (w_o, k_o, b_o) in enumerate(((w0_o, k0_o, b0_o), (w1_o, k1_o, b1_o))):
        pre = w0_ref[d:d + 1, :] + _dot(tw, w2_ref[d])
        w_o[...] = jnp.exp(-_sigmoid(pre) * float(np.exp(-0.5)))
        a = _sigmoid(a0_ref[d:d + 1, :] + _dot(ta, a2_ref[d]))
        kd = k * (1.0 + (a - 1.0) * ka_ref[...])
        k_o[...] = kd
        b_o[...] = kk * a
        ksum = kd if ksum is None else ksum + kd
    bon_o[...] = _seg_sum(r * ksum * rk_ref[...], bd) * v


def _rw_project(xs, mod, ng, w, ncb_tokens):
    nb, t, d = xs.shape
    tb = TB_PROJ
    nblk, ncb = t // tb, ncb_tokens // tb
    sl = tb // 8
    weights = [ng, w["mix"], w["w_rkv"], w["w1"], w["a1"], w["g1"], w["w2"], w["a2"], w["g2"],
               w["w0"], w["a0"], w["k_k"], w["k_a"], w["r_k"], w["bd"]]
    tok = pl.BlockSpec((None, tb, d), lambda b, j: (b, j, 0))
    return pl.pallas_call(
        functools.partial(_rw_proj_kernel, ncb=ncb, nblk=nblk),
        grid=(nb, nblk),
        in_specs=[tok,
                  pl.BlockSpec((None, 8, d), lambda b, j: (b, jnp.maximum(j * sl - 1, 0), 0)),
                  pl.BlockSpec((None, 8, d), lambda b, j: (b, jnp.minimum((j + 1) * sl, t // 8 - 1), 0)),
                  _mod_spec(nb, d, ncb)] + [_full(a) for a in weights],
        out_specs=[tok] * 11,
        out_shape=[jax.ShapeDtypeStruct((nb, t, d), F32)] * 11,
        compiler_params=_params(2),
        name="rwkv_project",
    )(xs, xs, xs, mod, *weights)


def _rw_scan_kernel(rf, wf, kf, vf, kkf, bf, rb, wb, kb, vb, kkb, bb, bd_ref,
                    yf_o, yb_o, s_ref):
    tb = rf.shape[0]
    nch = rf.shape[1] // LANES

    @pl.when(pl.program_id(1) == 0)
    def _():
        s_ref[...] = jnp.zeros_like(s_ref)

    bd = bd_ref[...]
    vi = lax.broadcasted_iota(jnp.int32, (RWKV_HEAD, LANES), 0)
    li = lax.broadcasted_iota(jnp.int32, (RWKV_HEAD, LANES), 1)
    diag = ((li & (RWKV_HEAD - 1)) == vi).astype(F32)

    def step(s, tiles, n):
        r, w, k, v, kk, b = (t[n:n + 1, :] for t in tiles)
        vcol = _dot((v * diag).astype(BF16), bd)
        sa = _dot((s * kk).astype(BF16), bd)
        s = s * w - sa * b + vcol * k
        yb = _dot((s * r).astype(BF16), bd)
        return s, jnp.sum(yb * diag, axis=0, keepdims=True)

    fwd = (rf, wf, kf, vf, kkf, bf)
    bwd = (rb, wb, kb, vb, kkb, bb)
    sub = 8

    def body(g, carry):
        gf = pl.multiple_of(g * sub, sub)
        gb = pl.multiple_of(tb - sub - g * sub, sub)
        for c in range(nch):
            ls = slice(c * LANES, (c + 1) * LANES)
            tf = [ref[pl.ds(gf, sub), ls] for ref in fwd]
            tbk = [ref[pl.ds(gb, sub), ls] for ref in bwd]
            yf, yb = [], []
            sf, sb = s_ref[0, c], s_ref[1, c]
            for n in range(sub):
                sf, y = step(sf, tf, n)
                yf.append(y)
                sb, y = step(sb, tbk, sub - 1 - n)
                yb.append(y)
            s_ref[0, c] = sf
            s_ref[1, c] = sb
            yf_o[pl.ds(gf, sub), ls] = jnp.concatenate(yf, axis=0)
            yb_o[pl.ds(gb, sub), ls] = jnp.concatenate(yb[::-1], axis=0)
        return carry

    lax.fori_loop(0, tb // sub, body, 0)


def _bwd_block(j, ncb, nblk):
    return jnp.where(j < ncb, ncb - 1 - j, nblk - 1 - (j - ncb))


def _rw_scan(P, bd, ncb_tokens):
    r, v, kk, _, _, w0, k0, b0, w1, k1, b1 = P
    nb, t, d = r.shape
    tb = TB_SCAN
    nblk, ncb = t // tb, ncb_tokens // tb
    fs = pl.BlockSpec((None, tb, d), lambda b, j: (b, j, 0))
    bs = pl.BlockSpec((None, tb, d), lambda b, j: (b, _bwd_block(j, ncb, nblk), 0))
    return pl.pallas_call(
        _rw_scan_kernel,
        grid=(nb, nblk),
        in_specs=[fs] * 6 + [bs] * 6 + [_full(bd)],
        out_specs=[fs, bs],
        out_shape=[jax.ShapeDtypeStruct((nb, t, d), F32)] * 2,
        scratch_shapes=[pltpu.VMEM((2, d // LANES, RWKV_HEAD, LANES), F32)],
        compiler_params=_params(2),
        name="rwkv_scan",
    )(r, w0, k0, v, kk, b0, r, w1, k1, v, kk, b1, bd)


def _rw_out_kernel(x_ref, yf_ref, yb_ref, g_ref, bon_ref, mod_ref, lnw_ref, lnb_ref, wo_ref, bd_ref, o_ref):
    bd = bd_ref[...]
    y = yf_ref[...] + yb_ref[...]
    mu = _seg_sum(y, bd) * (1.0 / RWKV_HEAD)
    yc = y - mu
    var = _seg_sum(yc * yc, bd) * (1.0 / RWKV_HEAD)
    yn = yc * lax.rsqrt(var + RWKV_GN_EPS) * lnw_ref[...] + lnb_ref[...]
    z = ((yn + bon_ref[...]) * g_ref[...]).astype(BF16)
    o_ref[...] = x_ref[...] + mod_ref[2:3, :] * _dot(z, wo_ref[...])


def _rw_readout(xs, yf, yb, gate, bon, mod, w, ncb_tokens):
    nb, t, d = xs.shape
    tb = TB_SCAN
    tok = pl.BlockSpec((None, tb, d), lambda b, j: (b, j, 0))
    weights = [w["ln_w"], w["ln_b"], w["w_o"], w["bd"]]
    return pl.pallas_call(
        _rw_out_kernel,
        grid=(nb, t // tb),
        in_specs=[tok] * 5 + [_mod_spec(nb, d, ncb_tokens // tb)] + [_full(a) for a in weights],
        out_specs=tok,
        out_shape=jax.ShapeDtypeStruct((nb, t, d), F32),
        compiler_params=_params(2),
        name="rwkv_readout",
    )(xs, yf, yb, gate, bon, mod, *weights)


def _gla_proj_kernel(x_ref, mod_ref, ng_ref, wqkv_ref, wg_ref, wlr_ref, up_ref, gkb_ref,
                     z_o, g_o, la_o):
    h = _modnorm(x_ref[...], ng_ref[...], mod_ref[0:1, :], mod_ref[1:2, :])
    hb = h.astype(BF16)
    z_o[...] = _dot(hb, wqkv_ref[...])
    g_o[...] = _dot(hb, wg_ref[...])
    lr = _dot(hb, wlr_ref[...]).astype(BF16)
    la_o[...] = _log_sigmoid(_dot(lr, up_ref[...]) + gkb_ref[...]) * (1.0 / GLA_GATE_NORM)


def _gla_project(xs, mod, ng, w, ncb_tokens):
    nb, t, d = xs.shape
    tb = TB_SCAN
    weights = [ng, w["w_qkv"], w["w_g"], w["w_lr"], w["up"], w["gk_b"]]
    tok = pl.BlockSpec((None, tb, d), lambda b, j: (b, j, 0))
    tok2 = pl.BlockSpec((None, tb, 2 * d), lambda b, j: (b, j, 0))
    return pl.pallas_call(
        _gla_proj_kernel,
        grid=(nb, t // tb),
        in_specs=[tok, _mod_spec(nb, d, ncb_tokens // tb)] + [_full(a) for a in weights],
        out_specs=[tok2, tok, tok],
        out_shape=[jax.ShapeDtypeStruct((nb, t, 2 * d), F32),
                   jax.ShapeDtypeStruct((nb, t, d), F32),
                   jax.ShapeDtypeStruct((nb, t, d), F32)],
        compiler_params=_params(2),
        name="gla_project",
    )(xs, mod, *weights)


def _gla_conv_kernel(zc_ref, zp_ref, zn_ref, cw_ref, o_ref, *, nblk, qk_width, q_scale):
    j = pl.program_id(1)
    tb, ch = zc_ref.shape
    is_ctx = j == 0
    up_ok = jnp.where(j > 1, 1.0, 0.0)
    dn_ok = jnp.where((j > 0) & (j < nblk - 1), 1.0, 0.0)
    vert = jnp.where(is_ctx, 0.0, 1.0)
    ext = tb + 2 * GRID_W
    pos = lax.broadcasted_iota(jnp.int32, (ext, 1), 0) - GRID_W
    col = jnp.where(is_ctx, pos, pos & (GRID_W - 1))
    no_left = col == 0
    no_right = col == jnp.where(is_ctx, tb - 1, GRID_W - 1)
    cc = 256
    for c in range(ch // cc):
        cs = slice(c * cc, (c + 1) * cc)
        e = jnp.concatenate([zp_ref[:, cs] * up_ok, zc_ref[:, cs], zn_ref[:, cs] * dn_ok], axis=0)
        em = jnp.where(no_left, 0.0, pltpu.roll(e, 1, 0))
        ep = jnp.where(no_right, 0.0, pltpu.roll(e, ext - 1, 0))
        acc = None
        for dy in range(3):
            lo = dy * GRID_W
            for dx, src in enumerate((em, e, ep)):
                wt = cw_ref[dy * 3 + dx:dy * 3 + dx + 1, cs]
                if dy != 1:
                    wt = wt * vert
                term = src[lo:lo + tb] * wt
                acc = term if acc is None else acc + term
        y = _silu(acc)
        if (c + 1) * cc <= qk_width // 2:
            y = y * q_scale
        o_ref[:, cs] = y


def _gla_conv(z, cw, ncb_tokens, qk_width, q_scale):
    nb, t, ch = z.shape
    tb = TB_SCAN
    assert ncb_tokens == tb and tb % GRID_W == 0
    sl = tb // GRID_W
    nblk = t // tb
    return pl.pallas_call(
        functools.partial(_gla_conv_kernel, nblk=nblk, qk_width=qk_width, q_scale=q_scale),
        grid=(nb, nblk),
        in_specs=[pl.BlockSpec((None, tb, ch), lambda b, j: (b, j, 0)),
                  pl.BlockSpec((None, GRID_W, ch), lambda b, j: (b, jnp.maximum(j * sl - 1, 0), 0)),
                  pl.BlockSpec((None, GRID_W, ch),
                               lambda b, j: (b, jnp.minimum((j + 1) * sl, t // GRID_W - 1), 0)),
                  _full(cw)],
        out_specs=pl.BlockSpec((None, tb, ch), lambda b, j: (b, j, 0)),
        out_shape=jax.ShapeDtypeStruct((nb, t, ch), F32),
        compiler_params=_params(2),
        name="gla_conv",
    )(z, z, z, cw)


def _gla_scan_kernel(qf_ref, laf_ref, qb_ref, lab_ref, tri_ref, of_o, ob_o, st_ref, *, dk, dv):
    tb = qf_ref.shape[0]
    c = GLA_CHUNK
    nh = GLA_HEADS
    qkw = nh * dk

    @pl.when(pl.program_id(1) == 0)
    def _():
        st_ref[...] = jnp.zeros_like(st_ref)

    for d, (q_ref, la_ref, o_o) in enumerate(((qf_ref, laf_ref, of_o), (qb_ref, lab_ref, ob_o))):
        tri = tri_ref[d]
        keep = tri > 0
        ref_row = c // 2 if d == 0 else c - 1 - c // 2
        last_row = c - 1 if d == 0 else 0
        chunks = range(tb // c) if d == 0 else range(tb // c - 1, -1, -1)
        for n in chunks:
            rows = slice(n * c, (n + 1) * c)
            for hh in range(nh):
                q = q_ref[rows, hh * dk:(hh + 1) * dk]
                k = q_ref[rows, qkw + hh * dk:qkw + (hh + 1) * dk]
                v = q_ref[rows, 2 * qkw + hh * dv:2 * qkw + (hh + 1) * dv]
                la = la_ref[rows, d * qkw + hh * dk:d * qkw + (hh + 1) * dk]
                l1, l2, l3 = _split3(la)
                bcum = _dot(tri, l1) + _dot(tri, l2) + _dot(tri, l3)
                ref = bcum[ref_row:ref_row + 1]
                last = bcum[last_row:last_row + 1]
                a = _dot_nt((q * jnp.exp(bcum - ref)).astype(BF16), (k * jnp.exp(ref - bcum)).astype(BF16))
                a = jnp.where(keep, a, 0.0)
                vb = v.astype(BF16)
                st = st_ref[d, hh]
                o = _dot(a.astype(BF16), vb) + _dot_nt((q * jnp.exp(bcum)).astype(BF16), st.astype(BF16))
                o_o[rows, hh * dv:(hh + 1) * dv] = o
                kd = (k * jnp.exp(last - bcum)).astype(BF16)
                st_ref[d, hh] = st * jnp.exp(last) + _dot_tn(vb, kd)


def _gla_scan(qkv, la, tri, ncb_tokens, dk, dv):
    nb, t, ch = qkv.shape
    d = la.shape[-1]
    tb = TB_SCAN
    nblk, ncb = t // tb, ncb_tokens // tb
    f2 = pl.BlockSpec((None, tb, ch), lambda b, j: (b, j, 0))
    b2 = pl.BlockSpec((None, tb, ch), lambda b, j: (b, _bwd_block(j, ncb, nblk), 0))
    f1 = pl.BlockSpec((None, tb, d), lambda b, j: (b, j, 0))
    b1 = pl.BlockSpec((None, tb, d), lambda b, j: (b, _bwd_block(j, ncb, nblk), 0))
    return pl.pallas_call(
        functools.partial(_gla_scan_kernel, dk=dk, dv=dv),
        grid=(nb, nblk),
        in_specs=[f2, f1, b2, b1, _full(tri)],
        out_specs=[f1, b1],
        out_shape=[jax.ShapeDtypeStruct((nb, t, d), F32)] * 2,
        scratch_shapes=[pltpu.VMEM((2, GLA_HEADS, dv, dk), F32)],
        compiler_params=_params(2),
        name="gla_scan",
    )(qkv, la, qkv, la, tri)


def _gla_out_kernel(x_ref, of_ref, ob_ref, g_ref, mod_ref, hn_ref, wo_ref, o_ref, *, dv):
    o = of_ref[...] + ob_ref[...]
    parts = []
    for hh in range(GLA_HEADS):
        oh = o[:, hh * dv:(hh + 1) * dv]
        ms = jnp.mean(oh * oh, axis=-1, keepdims=True)
        parts.append(oh * lax.rsqrt(ms + NORM_EPS) * hn_ref[...])
    on = jnp.concatenate(parts, axis=-1)
    z = (on * _silu(g_ref[...])).astype(BF16)
    o_ref[...] = x_ref[...] + mod_ref[2:3, :] * _dot(z, wo_ref[...])


def _gla_readout(xs, of, ob, g, mod, w, ncb_tokens, dv):
    nb, t, d = xs.shape
    tb = TB_SCAN
    tok = pl.BlockSpec((None, tb, d), lambda b, j: (b, j, 0))
    weights = [w["head_norm"], w["w_o"]]
    return pl.pallas_call(
        functools.partial(_gla_out_kernel, dv=dv),
        grid=(nb, t // tb),
        in_specs=[tok] * 4 + [_mod_spec(nb, d, ncb_tokens // tb)] + [_full(a) for a in weights],
        out_specs=tok,
        out_shape=jax.ShapeDtypeStruct((nb, t, d), F32),
        compiler_params=_params(2),
        name="gla_readout",
    )(xs, of, ob, g, mod, *weights)


def _top_values(s, k):
    vals = []
    cur = s
    for _ in range(k):
        m = jnp.max(cur, axis=0, keepdims=True)
        vals.append(m)
        cur = jnp.where(cur == m, -jnp.inf, cur)
    return vals


def _peer_route_kernel(x_ref, mod_ref, ng_ref, wq_hi_ref, wq_lo_ref, key_hi_ref, key_lo_ref,
                       ht_o, s0_o, s1_o, meta_o, q_hi_ref, q_lo_ref):
    h = _modnorm(x_ref[...], ng_ref[...], mod_ref[3:4, :], mod_ref[4:5, :])
    ht = h.T
    ht_o[...] = ht.astype(BF16)
    h_hi, h_lo = _split2(ht)
    qt = _dot3(wq_hi_ref[...], wq_lo_ref[...], h_hi, h_lo)
    q_hi, q_lo = _split2(qt)
    q_hi_ref[...] = q_hi
    q_lo_ref[...] = q_lo
    nk = N_KEYS
    tb = x_ref.shape[0]
    zeros = jnp.zeros((4, tb), F32)

    def head(hd, carry):
        sv = []
        for p in range(2):
            hp = hd * 2 + p
            rows = pl.ds(pl.multiple_of(hp * nk, nk), nk)
            s = _dot3(key_hi_ref[hp], key_lo_ref[hp], q_hi_ref[rows, :], q_lo_ref[rows, :])
            (s0_o, s1_o)[p][hd] = s
            sv.append(_top_values(s, PEER_TOPK))
        sv1 = jnp.concatenate(sv[1], axis=0)
        cand = jnp.concatenate([sv[0][a] + sv1 for a in range(PEER_TOPK)], axis=0)
        top = _top_values(cand, PEER_TOPK)
        z = top[0] * 0.0
        for cval in top:
            z = z + jnp.exp(cval - top[0])
        meta_o[hd] = jnp.concatenate([top[-1], sv[0][0], sv[1][0], 1.0 / z, zeros], axis=0)
        return carry

    lax.fori_loop(0, PEER_HEADS, head, 0)


def _peer_route(xs, mod, ng, w, ncb_tokens):
    nb, t, d = xs.shape
    tb = TB_ROUTE
    nq = w["wq_hi"].shape[0]
    weights = [ng, w["wq_hi"], w["wq_lo"], w["key_hi"], w["key_lo"]]
    sspec = pl.BlockSpec((None, PEER_HEADS, N_KEYS, tb), lambda b, j: (b, 0, 0, j))
    return pl.pallas_call(
        _peer_route_kernel,
        grid=(nb, t // tb),
        in_specs=[pl.BlockSpec((None, tb, d), lambda b, j: (b, j, 0)),
                  _mod_spec(nb, d, ncb_tokens // tb)] + [_full(a) for a in weights],
        out_specs=[pl.BlockSpec((None, d, tb), lambda b, j: (b, 0, j)), sspec, sspec,
                   pl.BlockSpec((None, PEER_HEADS, 8, tb), lambda b, j: (b, 0, 0, j))],
        out_shape=[jax.ShapeDtypeStruct((nb, d, t), BF16),
                   jax.ShapeDtypeStruct((nb, PEER_HEADS, N_KEYS, t), F32),
                   jax.ShapeDtypeStruct((nb, PEER_HEADS, N_KEYS, t), F32),
                   jax.ShapeDtypeStruct((nb, PEER_HEADS, 8, t), F32)],
        scratch_shapes=[pltpu.VMEM((nq, tb), BF16), pltpu.VMEM((nq, tb), BF16)],
        compiler_params=_params(2),
        name="peer_route",
    )(xs, mod, *weights)


def _peer_dense_kernel(x_ref, ht_ref, s0_ref, s1_ref, meta_ref, modl_ref, modc_ref, u_ref, vt_ref, o_ref,
                       acc_ref, e0_ref, e1_ref, *, nctx):
    ec = pl.program_id(2)
    nec = pl.num_programs(2)
    ne = u_ref.shape[0]
    nk = N_KEYS
    tb = x_ref.shape[0]
    tok0 = pl.program_id(1) * tb

    @pl.when(ec == 0)
    def _():
        acc_ref[...] = jnp.zeros_like(acc_ref)
        for hd in range(PEER_HEADS):
            e0_ref[hd] = jnp.exp(s0_ref[hd] - meta_ref[hd, 1:2, :]) * meta_ref[hd, 3:4, :]
            e1_ref[hd] = jnp.exp(s1_ref[hd] - meta_ref[hd, 2:3, :])

    a = _dot(u_ref[...], ht_ref[...])
    act = 0.5 * a * (1.0 + lax.erf(a * float(1.0 / np.sqrt(2.0))))
    parts = []
    ni = ne // nk
    i0 = pl.multiple_of(ec * ni, ni)
    for ii in range(ni):
        wacc = None
        for hd in range(PEER_HEADS):
            s0row = s0_ref[hd, pl.ds(i0, ni), :][ii:ii + 1]
            e0row = e0_ref[hd, pl.ds(i0, ni), :][ii:ii + 1]
            sel = (s0row + s1_ref[hd]) >= meta_ref[hd, 0:1, :]
            term = jnp.where(sel, e1_ref[hd], 0.0) * e0row
            wacc = term if wacc is None else wacc + term
        parts.append((wacc * act[ii * nk:(ii + 1) * nk]).astype(BF16))
    wa = jnp.concatenate(parts, axis=0)
    acc_ref[...] += _dot(vt_ref[...], wa)

    @pl.when(ec == nec - 1)
    def _():
        tok = tok0 + lax.broadcasted_iota(jnp.int32, (tb, 1), 0)
        gate = jnp.where(tok < nctx, modc_ref[5:6, :], modl_ref[5:6, :])
        o_ref[...] = x_ref[...] + gate * acc_ref[...].T


def _peer_dense(xs, ht, s0, s1, meta, mod, w, ncb_tokens):
    nb, t, d = xs.shape
    tb = TB_DENSE
    ne = EC_DENSE
    n_exp = w["u"].shape[0]
    assert t % tb == 0
    sspec = pl.BlockSpec((None, PEER_HEADS, N_KEYS, tb), lambda b, j, e: (b, 0, 0, j))
    return pl.pallas_call(
        functools.partial(_peer_dense_kernel, nctx=ncb_tokens),
        grid=(nb, t // tb, n_exp // ne),
        in_specs=[pl.BlockSpec((None, tb, d), lambda b, j, e: (b, j, 0)),
                  pl.BlockSpec((None, d, tb), lambda b, j, e: (b, 0, j)),
                  sspec, sspec,
                  pl.BlockSpec((None, PEER_HEADS, 8, tb), lambda b, j, e: (b, 0, 0, j)),
                  pl.BlockSpec((None, 6, d), lambda b, j, e: (b, 0, 0)),
                  pl.BlockSpec((None, 6, d), lambda b, j, e: (nb, 0, 0)),
                  pl.BlockSpec((ne, d), lambda b, j, e: (e, 0)),
                  pl.BlockSpec((d, ne), lambda b, j, e: (0, e))],
        out_specs=pl.BlockSpec((None, tb, d), lambda b, j, e: (b, j, 0)),
        out_shape=jax.ShapeDtypeStruct((nb, t, d), F32),
        scratch_shapes=[pltpu.VMEM((d, tb), F32),
                        pltpu.VMEM((PEER_HEADS, N_KEYS, tb), F32),
                        pltpu.VMEM((PEER_HEADS, N_KEYS, tb), F32)],
        compiler_params=_params(3),
        name="peer_dense",
    )(xs, ht, s0, s1, meta, mod, mod, w["u"], w["vt"])


def _final_norm_kernel(x_ref, g_ref, o_ref):
    x = x_ref[...]
    ms = jnp.mean(x * x, axis=-1, keepdims=True)
    o_ref[...] = x * lax.rsqrt(ms + NORM_EPS) * g_ref[...]


def _final_norm(xs, g, ncb_tokens):
    nb, t, d = xs.shape
    tb = TB_SCAN
    ncb = ncb_tokens // tb
    return pl.pallas_call(
        _final_norm_kernel,
        grid=(nb, t // tb - ncb),
        in_specs=[pl.BlockSpec((None, tb, d), lambda b, j: (b, j + ncb, 0)), _full(g)],
        out_specs=pl.BlockSpec((None, tb, d), lambda b, j: (b, j, 0)),
        out_shape=jax.ShapeDtypeStruct((nb, t - ncb_tokens, d), F32),
        compiler_params=_params(2),
        name="final_norm",
    )(xs, g)


def _row(a):
    return a.reshape(1, -1)


def _pad_dir(w2):
    z = jnp.zeros_like(w2[0])
    return jnp.stack([jnp.concatenate([w2[0], z], axis=0), jnp.concatenate([z, w2[1]], axis=0)]).astype(BF16)


def kernel(x, c, ctx, c_ctx, ada_w, ada_b, norm_mix, norm_ffn, rw_mix, rw_w_rkv, rw_w0, rw_w1, rw_w2, rw_a0, rw_a1, rw_a2, rw_g1, rw_g2, rw_k_k, rw_k_a, rw_r_k, rw_ln_w, rw_ln_b, rw_w_o, gla_w_in, gla_conv, gla_gk_up, gla_gk_b, gla_head_norm, gla_w_o, peer_wq, peer_keys, peer_u, peer_v, final_norm):
    nb, seq, d = x.shape
    nctx = ctx.shape[1]
    depth = ada_w.shape[0]
    assert nctx == TB_SCAN and seq % TB_SCAN == 0 and d % LANES == 0

    xs = jnp.concatenate([ctx, x], axis=1)
    cc = jnp.zeros((16, d), F32).at[:nb].set(c).at[nb].set(c_ctx)
    mods = _modulation(cc, ada_w, ada_b)[:, :nb + 1].reshape(depth, nb + 1, 6, d)

    bd_head = _block_diag_ones(RWKV_HEAD)
    ci = np.arange(GLA_CHUNK)
    tri = jnp.asarray(np.stack([ci[None, :] <= ci[:, None], ci[None, :] >= ci[:, None]]), dtype=BF16)

    for i in range(depth):
        mod = mods[i]
        j = i // 2
        if i % 2 == 0:
            w = dict(
                mix=rw_mix[j], w_rkv=rw_w_rkv[j].astype(BF16),
                w1=jnp.concatenate([rw_w1[j, 0], rw_w1[j, 1]], axis=1).astype(BF16),
                a1=jnp.concatenate([rw_a1[j, 0], rw_a1[j, 1]], axis=1).astype(BF16),
                g1=rw_g1[j].astype(BF16), w2=_pad_dir(rw_w2[j]), a2=_pad_dir(rw_a2[j]),
                g2=rw_g2[j].astype(BF16), w0=rw_w0[j], a0=rw_a0[j],
                k_k=_row(rw_k_k[j]), k_a=_row(rw_k_a[j]), r_k=_row(rw_r_k[j]),
                ln_w=_row(rw_ln_w[j]), ln_b=_row(rw_ln_b[j]), w_o=rw_w_o[j].astype(BF16), bd=bd_head)
            P = _rw_project(xs, mod, _row(norm_mix[i]), w, nctx)
            yf, yb = _rw_scan(P, bd_head, nctx)
            xs = _rw_readout(xs, yf, yb, P[3], P[4], mod, w, nctx)
        else:
            w_in = gla_w_in[j]
            qk_width = gla_gk_up.shape[-1] * 2
            conv_ch = qk_width + d
            dk = gla_gk_up.shape[-1] // GLA_HEADS
            dv = d // GLA_HEADS
            rank = gla_gk_up.shape[2]
            lr0 = conv_ch + d
            w_lr = jnp.zeros((d, LANES), F32).at[:, :2 * rank].set(w_in[:, lr0:lr0 + 2 * rank])
            up = jnp.zeros((LANES, d), F32)
            up = up.at[:rank, :qk_width // 2].set(gla_gk_up[j, 0]).at[rank:2 * rank, qk_width // 2:].set(gla_gk_up[j, 1])
            w = dict(w_qkv=w_in[:, :conv_ch].astype(BF16), w_g=w_in[:, conv_ch:lr0].astype(BF16),
                     w_lr=w_lr.astype(BF16), up=up.astype(BF16),
                     gk_b=jnp.concatenate([gla_gk_b[j, 0], gla_gk_b[j, 1]]).reshape(1, d),
                     head_norm=_row(gla_head_norm[j]), w_o=gla_w_o[j].astype(BF16))
            z, g, la = _gla_project(xs, mod, _row(norm_mix[i]), w, nctx)
            qkv = _gla_conv(z, gla_conv[j].reshape(9, conv_ch), nctx, qk_width, float(dk) ** -0.5)
            of, ob = _gla_scan(qkv, la, tri, nctx, dk, dv)
            xs = _gla_readout(xs, of, ob, g, mod, w, nctx, dv)

        wq_t = peer_wq[i].T
        wq_hi = wq_t.astype(BF16)
        keys = peer_keys[i].reshape(PEER_HEADS * 2, N_KEYS, -1)
        key_hi = keys.astype(BF16)
        pw = dict(wq_hi=wq_hi, wq_lo=(wq_t - wq_hi.astype(F32)).astype(BF16),
                  key_hi=key_hi, key_lo=(keys - key_hi.astype(F32)).astype(BF16),
                  u=peer_u[i].astype(BF16), vt=peer_v[i].T.astype(BF16))
        ht, s0, s1, meta = _peer_route(xs, mod, _row(norm_ffn[i]), pw, nctx)
        xs = _peer_dense(xs, ht, s0, s1, meta, mod, pw, nctx)

    return _final_norm(xs, _row(final_norm), nctx)
```

```python
import functools

import numpy as np
import jax
import jax.numpy as jnp
from jax import lax
from jax.experimental import pallas as pl
from jax.experimental.pallas import tpu as pltpu

F32 = jnp.float32
BF16 = jnp.bfloat16

NORM_EPS = 1e-6
GRID_W = 64
RWKV_HEAD = 64
RWKV_GN_EPS = 64e-5
DECAY_LORA = 64
GLA_HEADS = 4
GLA_GATE_RANK = 16
GLA_GATE_NORM = 16.0
GLA_CHUNK = 64
PEER_HEADS = 8
N_KEYS = 128
PEER_TOPK = 16

LANES = 128
VMEM_LIMIT = 56 * 1024 * 1024

TB_PROJ = 128
TB_SCAN = 256
TB_ROUTE = 256
TB_DENSE = 384
EC_DENSE = 1024


def _params(n_axes):
    return pltpu.CompilerParams(
        dimension_semantics=("arbitrary",) * n_axes,
        vmem_limit_bytes=VMEM_LIMIT)


def _full(a):
    nd = a.ndim
    return pl.BlockSpec(a.shape, lambda *_: (0,) * nd)


def _sigmoid(x):
    return 1.0 / (1.0 + jnp.exp(-x))


def _silu(x):
    return x * _sigmoid(x)


def _log_sigmoid(x):
    return jnp.minimum(x, 0.0) - jnp.log(1.0 + jnp.exp(-jnp.abs(x)))


def _modnorm(x, g, shift, scale):
    ms = jnp.mean(x * x, axis=-1, keepdims=True)
    return x * lax.rsqrt(ms + NORM_EPS) * (g * (1.0 + scale)) + shift


def _split2(x):
    hi = x.astype(BF16)
    lo = (x - hi.astype(F32)).astype(BF16)
    return hi, lo


def _split3(x):
    x1 = x.astype(BF16)
    r1 = x - x1.astype(F32)
    x2 = r1.astype(BF16)
    x3 = (r1 - x2.astype(F32)).astype(BF16)
    return x1, x2, x3


def _dot(a, b):
    return jnp.dot(a, b, preferred_element_type=F32)


def _dot_nt(a, b):
    return lax.dot_general(a, b, (((1,), (1,)), ((), ())), preferred_element_type=F32)


def _dot_tn(a, b):
    return lax.dot_general(a, b, (((0,), (0,)), ((), ())), preferred_element_type=F32)


def _dot3(a_hi, a_lo, b_hi, b_lo):
    return _dot(a_hi, b_hi) + _dot(a_hi, b_lo) + _dot(a_lo, b_hi)


def _seg_sum(x, bd):
    outs = []
    for c in range(x.shape[-1] // LANES):
        hi, lo = _split2(x[:, c * LANES:(c + 1) * LANES])
        outs.append(_dot(hi, bd) + _dot(lo, bd))
    return jnp.concatenate(outs, axis=-1)


def _block_diag_ones(group):
    i = np.arange(LANES)
    return jnp.asarray((i[:, None] // group) == (i[None, :] // group), dtype=BF16)


def _mod_kernel(c_ref, w_ref, b_ref, o_ref):
    a = _silu(c_ref[...])
    o_ref[...] = jnp.dot(a, w_ref[...], preferred_element_type=F32,
                         precision=lax.Precision.HIGHEST) + b_ref[...]


def _modulation(cc, ada_w, ada_b):
    depth, d, n = ada_w.shape
    nt = 768
    return pl.pallas_call(
        _mod_kernel,
        grid=(depth, n // nt),
        in_specs=[pl.BlockSpec(cc.shape, lambda i, k: (0, 0)),
                  pl.BlockSpec((None, d, nt), lambda i, k: (i, 0, k)),
                  pl.BlockSpec((None, 1, nt), lambda i, k: (i, 0, k))],
        out_specs=pl.BlockSpec((None, cc.shape[0], nt), lambda i, k: (i, 0, k)),
        out_shape=jax.ShapeDtypeStruct((depth, cc.shape[0], n), F32),
        compiler_params=_params(2),
        name="modulation",
    )(cc, ada_w, ada_b.reshape(depth, 1, n))


def _mod_spec(nb, d, ncb):
    return pl.BlockSpec((None, 6, d), lambda b, j: (jnp.where(j < ncb, nb, b), 0, 0))


def _rw_proj_kernel(x_ref, xp_ref, xn_ref, mod_ref, ng_ref, mix_ref, wrkv_ref,
                    w1_ref, a1_ref, g1_ref, w2_ref, a2_ref, g2_ref, w0_ref, a0_ref,
                    kk_w_ref, ka_ref, rk_ref, bd_ref,
                    r_o, v_o, kk_o, g_o, bon_o, w0_o, k0_o, b0_o, w1_o, k1_o, b1_o,
                    *, ncb, nblk):
    j = pl.program_id(1)
    tb = x_ref.shape[0]
    shift = mod_ref[0:1, :]
    scale = mod_ref[1:2, :]
    g = ng_ref[...]
    h = _modnorm(x_ref[...], g, shift, scale)
    hp = _modnorm(xp_ref[7:8, :], g, shift, scale)
    hn = _modnorm(xn_ref[0:1, :], g, shift, scale)
    hp = jnp.where((j != 0) & (j != ncb), hp, 0.0)
    hn = jnp.where((j != ncb - 1) & (j != nblk - 1), hn, 0.0)
    row = lax.broadcasted_iota(jnp.int32, (tb, 1), 0)
    prev = jnp.where(row == 0, hp, pltpu.roll(h, 1, 0))
    nxt = jnp.where(row == tb - 1, hn, pltpu.roll(h, tb - 1, 0))
    xx = 0.5 * (prev + nxt) - h
    xr, xw, xk, xv, xa, xg = (h + xx * mix_ref[i:i + 1, :] for i in range(6))

    r = _dot(xr.astype(BF16), wrkv_ref[0])
    k = _dot(xk.astype(BF16), wrkv_ref[1])
    v = _dot(xv.astype(BF16), wrkv_ref[2])
    gate = _dot(_sigmoid(_dot(xg.astype(BF16), g1_ref[...])).astype(BF16), g2_ref[...])
    tw = jnp.tanh(_dot(xw.astype(BF16), w1_ref[...])).astype(BF16)
    ta = _dot(xa.astype(BF16), a1_ref[...]).astype(BF16)

    bd = bd_ref[...]
    kk = k * kk_w_ref[...]
    kk = kk / jnp.maximum(jnp.sqrt(_seg_sum(kk * kk, bd)), 1e-12)

    r_o[...] = r
    v_o[...] = v
    kk_o[...] = kk
    g_o[...] = gate
    ksum = None
    for d, (w_o, k_o, b_o) in enumerate(((w0_o, k0_o, b0_o), (w1_o, k1_o, b1_o))):
        pre = w0_ref[d:d + 1, :] + _dot(tw, w2_ref[d])
        w_o[...] = jnp.exp(-_sigmoid(pre) * float(np.exp(-0.5)))
        a = _sigmoid(a0_ref[d:d + 1, :] + _dot(ta, a2_ref[d]))
        kd = k * (1.0 + (a - 1.0) * ka_ref[...])
        k_o[...] = kd
        b_o[...] = kk * a
        ksum = kd if ksum is None else ksum + kd
    bon_o[...] = _seg_sum(r * ksum * rk_ref[...], bd) * v


def _rw_project(xs, mod, ng, w, ncb_tokens):
    nb, t, d = xs.shape
    tb = TB_PROJ
    nblk, ncb = t // tb, ncb_tokens // tb
    sl = tb // 8
    weights = [ng, w["mix"], w["w_rkv"], w["w1"], w["a1"], w["g1"], w["w2"], w["a2"], w["g2"],
               w["w0"], w["a0"], w["k_k"], w["k_a"], w["r_k"], w["bd"]]
    tok = pl.BlockSpec((None, tb, d), lambda b, j: (b, j, 0))
    return pl.pallas_call(
        functools.partial(_rw_proj_kernel, ncb=ncb, nblk=nblk),
        grid=(nb, nblk),
        in_specs=[tok,
                  pl.BlockSpec((None, 8, d), lambda b, j: (b, jnp.maximum(j * sl - 1, 0), 0)),
                  pl.BlockSpec((None, 8, d), lambda b, j: (b, jnp.minimum((j + 1) * sl, t // 8 - 1), 0)),
                  _mod_spec(nb, d, ncb)] + [_full(a) for a in weights],
        out_specs=[tok] * 11,
        out_shape=[jax.ShapeDtypeStruct((nb, t, d), F32)] * 11,
        compiler_params=_params(2),
        name="rwkv_project",
    )(xs, xs, xs, mod, *weights)


def _rw_scan_kernel(rf, wf, kf, vf, kkf, bf, rb, wb, kb, vb, kkb, bb, bd_ref,
                    yf_o, yb_o, s_ref):
    tb = rf.shape[0]
    nch = rf.shape[1] // LANES

    @pl.when(pl.program_id(1) == 0)
    def _():
        s_ref[...] = jnp.zeros_like(s_ref)

    bd = bd_ref[...]
    hw = RWKV_HEAD
    vi = lax.broadcasted_iota(jnp.int32, (hw, LANES), 0)
    li = lax.broadcasted_iota(jnp.int32, (hw, LANES), 1)
    diag = ((li & (hw - 1)) == vi).astype(F32)
    lane = lax.broadcasted_iota(jnp.int32, (1, LANES), 1)
    first_head = lane < hw
    head_mask = (first_head.astype(F32), 1.0 - first_head.astype(F32))
    sub = 8

    gch = 4
    groups = [(d, list(range(c0, c0 + gch))) for c0 in range(0, nch, gch) for d in (0, 1)]
    fwd = (rf, wf, kf, vf, kkf, bf)
    bwd = (rb, wb, kb, vb, kkb, bb)

    def rows_of(s, i):
        return s[i * hw:(i + 1) * hw]

    def body(g, carry):
        base = (pl.multiple_of(g * sub, sub), pl.multiple_of(tb - sub - g * sub, sub))
        lanes = [slice(c * LANES, (c + 1) * LANES) for c in range(nch)]
        tiles = [[[ref[pl.ds(base[d], sub), ls] for ref in refs] for ls in lanes]
                 for d, refs in enumerate((fwd, bwd))]

        def rows(d, c, n):
            m = n if d == 0 else sub - 1 - n
            return [t[m:m + 1] for t in tiles[d][c]]

        def start(d, cs, s, n):
            vcol = _dot(jnp.concatenate([rows(d, c, n)[3] * diag for c in cs], axis=0).astype(BF16), bd)
            sa = _dot(jnp.concatenate([s[i] * rows(d, c, n)[4] for i, c in enumerate(cs)], axis=0).astype(BF16), bd)
            return vcol, sa

        state = [[s_ref[d, c] for c in cs] for d, cs in groups]
        pend = [start(d, cs, state[gi], 0) for gi, (d, cs) in enumerate(groups)]
        ys = [[[] for _ in range(nch)] for _ in range(2)]
        for n in range(sub):
            for gi, (d, cs) in enumerate(groups):
                vcol, sa = pend[gi]
                s = []
                for i, c in enumerate(cs):
                    r, w, k, v, kk, b = rows(d, c, n)
                    s.append(state[gi][i] * w - rows_of(sa, i) * b + rows_of(vcol, i) * k)
                state[gi] = s
                if n + 1 < sub:
                    pend[gi] = start(d, cs, s, n + 1)
                lhs = jnp.concatenate([rows(d, c, n)[0] * head_mask[h] for c in cs for h in range(2)], axis=0)
                out = _dot_nt(lhs.astype(BF16), jnp.concatenate(s, axis=0).astype(BF16))
                for m in range(gch // 2):
                    x = out[:, m * LANES:(m + 1) * LANES]
                    xr = pltpu.roll(x, hw, 1)
                    r0 = 4 * m
                    ys[d][cs[2 * m]].append(jnp.where(first_head, x[r0:r0 + 1], xr[r0 + 1:r0 + 2]))
                    ys[d][cs[2 * m + 1]].append(jnp.where(first_head, xr[r0 + 2:r0 + 3], x[r0 + 3:r0 + 4]))
        for gi, (d, cs) in enumerate(groups):
            for i, c in enumerate(cs):
                s_ref[d, c] = state[gi][i]
        for c, ls in enumerate(lanes):
            yf_o[pl.ds(base[0], sub), ls] = jnp.concatenate(ys[0][c], axis=0)
            yb_o[pl.ds(base[1], sub), ls] = jnp.concatenate(ys[1][c][::-1], axis=0)
        return carry

    lax.fori_loop(0, tb // sub, body, 0)


def _bwd_block(j, ncb, nblk):
    return jnp.where(j < ncb, ncb - 1 - j, nblk - 1 - (j - ncb))


def _rw_scan(P, bd, ncb_tokens):
    r, v, kk, _, _, w0, k0, b0, w1, k1, b1 = P
    nb, t, d = r.shape
    tb = TB_SCAN
    nblk, ncb = t // tb, ncb_tokens // tb
    fs = pl.BlockSpec((None, tb, d), lambda b, j: (b, j, 0))
    bs = pl.BlockSpec((None, tb, d), lambda b, j: (b, _bwd_block(j, ncb, nblk), 0))
    return pl.pallas_call(
        _rw_scan_kernel,
        grid=(nb, nblk),
        in_specs=[fs] * 6 + [bs] * 6 + [_full(bd)],
        out_specs=[fs, bs],
        out_shape=[jax.ShapeDtypeStruct((nb, t, d), F32)] * 2,
        scratch_shapes=[pltpu.VMEM((2, d // LANES, RWKV_HEAD, LANES), F32)],
        compiler_params=_params(2),
        name="rwkv_scan",
    )(r, w0, k0, v, kk, b0, r, w1, k1, v, kk, b1, bd)


def _rw_out_kernel(x_ref, yf_ref, yb_ref, g_ref, bon_ref, mod_ref, lnw_ref, lnb_ref, wo_ref, bd_ref, o_ref):
    bd = bd_ref[...]
    y = yf_ref[...] + yb_ref[...]
    mu = _seg_sum(y, bd) * (1.0 / RWKV_HEAD)
    yc = y - mu
    var = _seg_sum(yc * yc, bd) * (1.0 / RWKV_HEAD)
    yn = yc * lax.rsqrt(var + RWKV_GN_EPS) * lnw_ref[...] + lnb_ref[...]
    z = ((yn + bon_ref[...]) * g_ref[...]).astype(BF16)
    o_ref[...] = x_ref[...] + mod_ref[2:3, :] * _dot(z, wo_ref[...])


def _rw_readout(xs, yf, yb, gate, bon, mod, w, ncb_tokens):
    nb, t, d = xs.shape
    tb = TB_SCAN
    tok = pl.BlockSpec((None, tb, d), lambda b, j: (b, j, 0))
    weights = [w["ln_w"], w["ln_b"], w["w_o"], w["bd"]]
    return pl.pallas_call(
        _rw_out_kernel,
        grid=(nb, t // tb),
        in_specs=[tok] * 5 + [_mod_spec(nb, d, ncb_tokens // tb)] + [_full(a) for a in weights],
        out_specs=tok,
        out_shape=jax.ShapeDtypeStruct((nb, t, d), F32),
        compiler_params=_params(2),
        name="rwkv_readout",
    )(xs, yf, yb, gate, bon, mod, *weights)


def _gla_proj_kernel(x_ref, mod_ref, ng_ref, wqkv_ref, wg_ref, wlr_ref, up_ref, gkb_ref,
                     z_o, g_o, la_o):
    h = _modnorm(x_ref[...], ng_ref[...], mod_ref[0:1, :], mod_ref[1:2, :])
    hb = h.astype(BF16)
    z_o[...] = _dot(hb, wqkv_ref[...])
    g_o[...] = _dot(hb, wg_ref[...])
    lr = _dot(hb, wlr_ref[...]).astype(BF16)
    la_o[...] = _log_sigmoid(_dot(lr, up_ref[...]) + gkb_ref[...]) * (1.0 / GLA_GATE_NORM)


def _gla_project(xs, mod, ng, w, ncb_tokens):
    nb, t, d = xs.shape
    tb = TB_SCAN
    weights = [ng, w["w_qkv"], w["w_g"], w["w_lr"], w["up"], w["gk_b"]]
    tok = pl.BlockSpec((None, tb, d), lambda b, j: (b, j, 0))
    tok2 = pl.BlockSpec((None, tb, 2 * d), lambda b, j: (b, j, 0))
    return pl.pallas_call(
        _gla_proj_kernel,
        grid=(nb, t // tb),
        in_specs=[tok, _mod_spec(nb, d, ncb_tokens // tb)] + [_full(a) for a in weights],
        out_specs=[tok2, tok, tok],
        out_shape=[jax.ShapeDtypeStruct((nb, t, 2 * d), F32),
                   jax.ShapeDtypeStruct((nb, t, d), F32),
                   jax.ShapeDtypeStruct((nb, t, d), F32)],
        compiler_params=_params(2),
        name="gla_project",
    )(xs, mod, *weights)


def _gla_conv_kernel(zc_ref, zp_ref, zn_ref, cw_ref, o_ref, *, nblk, qk_width, q_scale):
    j = pl.program_id(1)
    tb, ch = zc_ref.shape
    is_ctx = j == 0
    up_ok = jnp.where(j > 1, 1.0, 0.0)
    dn_ok = jnp.where((j > 0) & (j < nblk - 1), 1.0, 0.0)
    vert = jnp.where(is_ctx, 0.0, 1.0)
    ext = tb + 2 * GRID_W
    pos = lax.broadcasted_iota(jnp.int32, (ext, 1), 0) - GRID_W
    col = jnp.where(is_ctx, pos, pos & (GRID_W - 1))
    no_left = col == 0
    no_right = col == jnp.where(is_ctx, tb - 1, GRID_W - 1)
    cc = 256
    for c in range(ch // cc):
        cs = slice(c * cc, (c + 1) * cc)
        e = jnp.concatenate([zp_ref[:, cs] * up_ok, zc_ref[:, cs], zn_ref[:, cs] * dn_ok], axis=0)
        em = jnp.where(no_left, 0.0, pltpu.roll(e, 1, 0))
        ep = jnp.where(no_right, 0.0, pltpu.roll(e, ext - 1, 0))
        acc = None
        for dy in range(3):
            lo = dy * GRID_W
            for dx, src in enumerate((em, e, ep)):
                wt = cw_ref[dy * 3 + dx:dy * 3 + dx + 1, cs]
                if dy != 1:
                    wt = wt * vert
                term = src[lo:lo + tb] * wt
                acc = term if acc is None else acc + term
        y = _silu(acc)
        if (c + 1) * cc <= qk_width // 2:
            y = y * q_scale
        o_ref[:, cs] = y


def _gla_conv(z, cw, ncb_tokens, qk_width, q_scale):
    nb, t, ch = z.shape
    tb = TB_SCAN
    assert ncb_tokens == tb and tb % GRID_W == 0
    sl = tb // GRID_W
    nblk = t // tb
    return pl.pallas_call(
        functools.partial(_gla_conv_kernel, nblk=nblk, qk_width=qk_width, q_scale=q_scale),
        grid=(nb, nblk),
        in_specs=[pl.BlockSpec((None, tb, ch), lambda b, j: (b, j, 0)),
                  pl.BlockSpec((None, GRID_W, ch), lambda b, j: (b, jnp.maximum(j * sl - 1, 0), 0)),
                  pl.BlockSpec((None, GRID_W, ch),
                               lambda b, j: (b, jnp.minimum((j + 1) * sl, t // GRID_W - 1), 0)),
                  _full(cw)],
        out_specs=pl.BlockSpec((None, tb, ch), lambda b, j: (b, j, 0)),
        out_shape=jax.ShapeDtypeStruct((nb, t, ch), F32),
        compiler_params=_params(2),
        name="gla_conv",
    )(z, z, z, cw)


def _gla_scan_kernel(qf_ref, laf_ref, qb_ref, lab_ref, tri_ref, of_o, ob_o, st_ref, *, dk, dv):
    tb = qf_ref.shape[0]
    c = GLA_CHUNK
    nh = GLA_HEADS
    qkw = nh * dk

    @pl.when(pl.program_id(1) == 0)
    def _():
        st_ref[...] = jnp.zeros_like(st_ref)

    for d, (q_ref, la_ref, o_o) in enumerate(((qf_ref, laf_ref, of_o), (qb_ref, lab_ref, ob_o))):
        tri = tri_ref[d]
        keep = tri > 0
        ref_row = c // 2 if d == 0 else c - 1 - c // 2
        last_row = c - 1 if d == 0 else 0
        chunks = range(tb // c) if d == 0 else range(tb // c - 1, -1, -1)
        for n in chunks:
            rows = slice(n * c, (n + 1) * c)
            for hh in range(nh):
                q = q_ref[rows, hh * dk:(hh + 1) * dk]
                k = q_ref[rows, qkw + hh * dk:qkw + (hh + 1) * dk]
                v = q_ref[rows, 2 * qkw + hh * dv:2 * qkw + (hh + 1) * dv]
                la = la_ref[rows, d * qkw + hh * dk:d * qkw + (hh + 1) * dk]
                l1, l2, l3 = _split3(la)
                bcum = _dot(tri, l1) + _dot(tri, l2) + _dot(tri, l3)
                ref = bcum[ref_row:ref_row + 1]
                last = bcum[last_row:last_row + 1]
                a = _dot_nt((q * jnp.exp(bcum - ref)).astype(BF16), (k * jnp.exp(ref - bcum)).astype(BF16))
                a = jnp.where(keep, a, 0.0)
                vb = v.astype(BF16)
                st = st_ref[d, hh]
                o = _dot(a.astype(BF16), vb) + _dot_nt((q * jnp.exp(bcum)).astype(BF16), st.astype(BF16))
                o_o[rows, hh * dv:(hh + 1) * dv] = o
                kd = (k * jnp.exp(last - bcum)).astype(BF16)
                st_ref[d, hh] = st * jnp.exp(last) + _dot_tn(vb, kd)


def _gla_scan(qkv, la, tri, ncb_tokens, dk, dv):
    nb, t, ch = qkv.shape
    d = la.shape[-1]
    tb = TB_SCAN
    nblk, ncb = t // tb, ncb_tokens // tb
    f2 = pl.BlockSpec((None, tb, ch), lambda b, j: (b, j, 0))
    b2 = pl.BlockSpec((None, tb, ch), lambda b, j: (b, _bwd_block(j, ncb, nblk), 0))
    f1 = pl.BlockSpec((None, tb, d), lambda b, j: (b, j, 0))
    b1 = pl.BlockSpec((None, tb, d), lambda b, j: (b, _bwd_block(j, ncb, nblk), 0))
    return pl.pallas_call(
        functools.partial(_gla_scan_kernel, dk=dk, dv=dv),
        grid=(nb, nblk),
        in_specs=[f2, f1, b2, b1, _full(tri)],
        out_specs=[f1, b1],
        out_shape=[jax.ShapeDtypeStruct((nb, t, d), F32)] * 2,
        scratch_shapes=[pltpu.VMEM((2, GLA_HEADS, dv, dk), F32)],
        compiler_params=_params(2),
        name="gla_scan",
    )(qkv, la, qkv, la, tri)


def _gla_out_kernel(x_ref, of_ref, ob_ref, g_ref, mod_ref, hn_ref, wo_ref, o_ref, *, dv):
    o = of_ref[...] + ob_ref[...]
    parts = []
    for hh in range(GLA_HEADS):
        oh = o[:, hh * dv:(hh + 1) * dv]
        ms = jnp.mean(oh * oh, axis=-1, keepdims=True)
        parts.append(oh * lax.rsqrt(ms + NORM_EPS) * hn_ref[...])
    on = jnp.concatenate(parts, axis=-1)
    z = (on * _silu(g_ref[...])).astype(BF16)
    o_ref[...] = x_ref[...] + mod_ref[2:3, :] * _dot(z, wo_ref[...])


def _gla_readout(xs, of, ob, g, mod, w, ncb_tokens, dv):
    nb, t, d = xs.shape
    tb = TB_SCAN
    tok = pl.BlockSpec((None, tb, d), lambda b, j: (b, j, 0))
    weights = [w["head_norm"], w["w_o"]]
    return pl.pallas_call(
        functools.partial(_gla_out_kernel, dv=dv),
        grid=(nb, t // tb),
        in_specs=[tok] * 4 + [_mod_spec(nb, d, ncb_tokens // tb)] + [_full(a) for a in weights],
        out_specs=tok,
        out_shape=jax.ShapeDtypeStruct((nb, t, d), F32),
        compiler_params=_params(2),
        name="gla_readout",
    )(xs, of, ob, g, mod, *weights)


def _top_values(s, k):
    vals = []
    cur = s
    for _ in range(k):
        m = jnp.max(cur, axis=0, keepdims=True)
        vals.append(m)
        cur = jnp.where(cur == m, -jnp.inf, cur)
    return vals


def _peer_route_kernel(x_ref, mod_ref, ng_ref, wq_hi_ref, wq_lo_ref, key_hi_ref, key_lo_ref,
                       ht_o, s0_o, s1_o, meta_o, q_hi_ref, q_lo_ref):
    h = _modnorm(x_ref[...], ng_ref[...], mod_ref[3:4, :], mod_ref[4:5, :])
    ht = h.T
    ht_o[...] = ht.astype(BF16)
    h_hi, h_lo = _split2(ht)
    qt = _dot3(wq_hi_ref[...], wq_lo_ref[...], h_hi, h_lo)
    q_hi, q_lo = _split2(qt)
    q_hi_ref[...] = q_hi
    q_lo_ref[...] = q_lo
    nk = N_KEYS
    tb = x_ref.shape[0]
    zeros = jnp.zeros((4, tb), F32)

    def head(hd, carry):
        sv = []
        for p in range(2):
            hp = hd * 2 + p
            rows = pl.ds(pl.multiple_of(hp * nk, nk), nk)
            s = _dot3(key_hi_ref[hp], key_lo_ref[hp], q_hi_ref[rows, :], q_lo_ref[rows, :])
            (s0_o, s1_o)[p][hd] = s
            sv.append(_top_values(s, PEER_TOPK + 1))
        k, half = PEER_TOPK, PEER_TOPK // 2
        sv0 = jnp.concatenate(sv[0][:k], axis=0)
        sv1 = jnp.concatenate(sv[1][:k], axis=0)
        edge = jnp.concatenate([sv[0][k] + sv[1][0], sv[0][0] + sv[1][k],
                                jnp.full((6, tb), -jnp.inf, F32)], axis=0)
        cand = jnp.concatenate([sv[0][0] + sv1]
                               + [sv[0][a] + sv1[:half] for a in range(1, half)]
                               + [sv0[half:] + sv[1][0], edge], axis=0)
        top = _top_values(cand, k + 1)
        z = top[0] * 0.0
        for cval in top[:k]:
            z = z + jnp.exp(cval - top[0])
        tau = 0.5 * (top[k - 1] + top[k])
        meta_o[hd] = jnp.concatenate([tau, sv[0][0], sv[1][0], 1.0 / z, zeros], axis=0)
        return carry

    lax.fori_loop(0, PEER_HEADS, head, 0)


def _peer_route(xs, mod, ng, w, ncb_tokens):
    nb, t, d = xs.shape
    tb = TB_ROUTE
    nq = w["wq_hi"].shape[0]
    weights = [ng, w["wq_hi"], w["wq_lo"], w["key_hi"], w["key_lo"]]
    sspec = pl.BlockSpec((None, PEER_HEADS, N_KEYS, tb), lambda b, j: (b, 0, 0, j))
    return pl.pallas_call(
        _peer_route_kernel,
        grid=(nb, t // tb),
        in_specs=[pl.BlockSpec((None, tb, d), lambda b, j: (b, j, 0)),
                  _mod_spec(nb, d, ncb_tokens // tb)] + [_full(a) for a in weights],
        out_specs=[pl.BlockSpec((None, d, tb), lambda b, j: (b, 0, j)), sspec, sspec,
                   pl.BlockSpec((None, PEER_HEADS, 8, tb), lambda b, j: (b, 0, 0, j))],
        out_shape=[jax.ShapeDtypeStruct((nb, d, t), BF16),
                   jax.ShapeDtypeStruct((nb, PEER_HEADS, N_KEYS, t), F32),
                   jax.ShapeDtypeStruct((nb, PEER_HEADS, N_KEYS, t), F32),
                   jax.ShapeDtypeStruct((nb, PEER_HEADS, 8, t), F32)],
        scratch_shapes=[pltpu.VMEM((nq, tb), BF16), pltpu.VMEM((nq, tb), BF16)],
        compiler_params=_params(2),
        name="peer_route",
    )(xs, mod, *weights)


def _peer_dense_kernel(x_ref, ht_ref, s0_ref, s1_ref, meta_ref, modl_ref, modc_ref, u_ref, vt_ref, o_ref,
                       acc_ref, th_ref, e0_ref, e1_ref, act_ref, wa_ref, *, nctx):
    ec = pl.program_id(2)
    nec = pl.num_programs(2)
    ne = u_ref.shape[0]
    nk = N_KEYS
    tb = x_ref.shape[0]
    nlt = tb // LANES
    tok0 = pl.program_id(1) * tb

    @pl.when(ec == 0)
    def _():
        acc_ref[...] = jnp.zeros_like(acc_ref)
        for hd in range(PEER_HEADS):
            e1_ref[hd] = jnp.exp(s1_ref[hd] - meta_ref[hd, 2:3, :])
            for lt in range(nlt):
                ls = slice(lt * LANES, (lt + 1) * LANES)
                s0 = s0_ref[hd, :, ls]
                th_ref[hd, lt] = meta_ref[hd, 0:1, ls] - s0
                e0_ref[hd, lt] = jnp.exp(s0 - meta_ref[hd, 1:2, ls]) * meta_ref[hd, 3:4, ls]

    a = _dot(u_ref[...], ht_ref[...])
    act_ref[...] = 0.5 * a * (1.0 + lax.erf(a * float(1.0 / np.sqrt(2.0))))
    ni = ne // nk
    i0 = pl.multiple_of(ec * ni, ni)
    jr = 32
    for lt in range(nlt):
        ls = slice(lt * LANES, (lt + 1) * LANES)
        for js in range(nk // jr):
            rows = slice(js * jr, (js + 1) * jr)
            accs = [None] * ni
            for hd in range(PEER_HEADS):
                s1 = s1_ref[hd, rows, ls]
                e1 = e1_ref[hd, rows, ls]
                th8 = th_ref[hd, lt, pl.ds(i0, ni), :]
                e08 = e0_ref[hd, lt, pl.ds(i0, ni), :]
                for ii in range(ni):
                    term = jnp.where(s1 >= th8[ii:ii + 1], e1, 0.0) * e08[ii:ii + 1]
                    accs[ii] = term if accs[ii] is None else accs[ii] + term
            for ii in range(ni):
                er = slice(ii * nk + js * jr, ii * nk + (js + 1) * jr)
                wa_ref[er, ls] = (accs[ii] * act_ref[er, ls]).astype(BF16)
    acc_ref[...] += _dot(vt_ref[...], wa_ref[...])

    @pl.when(ec == nec - 1)
    def _():
        tok = tok0 + lax.broadcasted_iota(jnp.int32, (tb, 1), 0)
        gate = jnp.where(tok < nctx, modc_ref[5:6, :], modl_ref[5:6, :])
        o_ref[...] = x_ref[...] + gate * acc_ref[...].T


def _peer_dense(xs, ht, s0, s1, meta, mod, w, ncb_tokens):
    nb, t, d = xs.shape
    tb = TB_DENSE
    ne = EC_DENSE
    n_exp = w["u"].shape[0]
    assert t % tb == 0
    sspec = pl.BlockSpec((None, PEER_HEADS, N_KEYS, tb), lambda b, j, e: (b, 0, 0, j))
    return pl.pallas_call(
        functools.partial(_peer_dense_kernel, nctx=ncb_tokens),
        grid=(nb, t // tb, n_exp // ne),
        in_specs=[pl.BlockSpec((None, tb, d), lambda b, j, e: (b, j, 0)),
                  pl.BlockSpec((None, d, tb), lambda b, j, e: (b, 0, j)),
                  sspec, sspec,
                  pl.BlockSpec((None, PEER_HEADS, 8, tb), lambda b, j, e: (b, 0, 0, j)),
                  pl.BlockSpec((None, 6, d), lambda b, j, e: (b, 0, 0)),
                  pl.BlockSpec((None, 6, d), lambda b, j, e: (nb, 0, 0)),
                  pl.BlockSpec((ne, d), lambda b, j, e: (e, 0)),
                  pl.BlockSpec((d, ne), lambda b, j, e: (0, e))],
        out_specs=pl.BlockSpec((None, tb, d), lambda b, j, e: (b, j, 0)),
        out_shape=jax.ShapeDtypeStruct((nb, t, d), F32),
        scratch_shapes=[pltpu.VMEM((d, tb), F32),
                        pltpu.VMEM((PEER_HEADS, tb // LANES, N_KEYS, LANES), F32),
                        pltpu.VMEM((PEER_HEADS, tb // LANES, N_KEYS, LANES), F32),
                        pltpu.VMEM((PEER_HEADS, N_KEYS, tb), F32),
                        pltpu.VMEM((ne, tb), F32),
                        pltpu.VMEM((ne, tb), BF16)],
        compiler_params=_params(3),
        name="peer_dense",
    )(xs, ht, s0, s1, meta, mod, mod, w["u"], w["vt"])


def _final_norm_kernel(x_ref, g_ref, o_ref):
    x = x_ref[...]
    ms = jnp.mean(x * x, axis=-1, keepdims=True)
    o_ref[...] = x * lax.rsqrt(ms + NORM_EPS) * g_ref[...]


def _final_norm(xs, g, ncb_tokens):
    nb, t, d = xs.shape
    tb = TB_SCAN
    ncb = ncb_tokens // tb
    return pl.pallas_call(
        _final_norm_kernel,
        grid=(nb, t // tb - ncb),
        in_specs=[pl.BlockSpec((None, tb, d), lambda b, j: (b, j + ncb, 0)), _full(g)],
        out_specs=pl.BlockSpec((None, tb, d), lambda b, j: (b, j, 0)),
        out_shape=jax.ShapeDtypeStruct((nb, t - ncb_tokens, d), F32),
        compiler_params=_params(2),
        name="final_norm",
    )(xs, g)


def _row(a):
    return a.reshape(1, -1)


def _pad_dir(w2):
    z = jnp.zeros_like(w2[0])
    return jnp.stack([jnp.concatenate([w2[0], z], axis=0), jnp.concatenate([z, w2[1]], axis=0)]).astype(BF16)


def kernel(x, c, ctx, c_ctx, ada_w, ada_b, norm_mix, norm_ffn, rw_mix, rw_w_rkv, rw_w0, rw_w1, rw_w2, rw_a0, rw_a1, rw_a2, rw_g1, rw_g2, rw_k_k, rw_k_a, rw_r_k, rw_ln_w, rw_ln_b, rw_w_o, gla_w_in, gla_conv, gla_gk_up, gla_gk_b, gla_head_norm, gla_w_o, peer_wq, peer_keys, peer_u, peer_v, final_norm):
    nb, seq, d = x.shape
    nctx = ctx.shape[1]
    depth = ada_w.shape[0]
    assert nctx == TB_SCAN and seq % TB_SCAN == 0 and d % LANES == 0

    xs = jnp.concatenate([ctx, x], axis=1)
    cc = jnp.zeros((16, d), F32).at[:nb].set(c).at[nb].set(c_ctx)
    mods = _modulation(cc, ada_w, ada_b)[:, :nb + 1].reshape(depth, nb + 1, 6, d)

    bd_head = _block_diag_ones(RWKV_HEAD)
    ci = np.arange(GLA_CHUNK)
    tri = jnp.asarray(np.stack([ci[None, :] <= ci[:, None], ci[None, :] >= ci[:, None]]), dtype=BF16)

    for i in range(depth):
        mod = mods[i]
        j = i // 2
        if i % 2 == 0:
            w = dict(
                mix=rw_mix[j], w_rkv=rw_w_rkv[j].astype(BF16),
                w1=jnp.concatenate([rw_w1[j, 0], rw_w1[j, 1]], axis=1).astype(BF16),
                a1=jnp.concatenate([rw_a1[j, 0], rw_a1[j, 1]], axis=1).astype(BF16),
                g1=rw_g1[j].astype(BF16), w2=_pad_dir(rw_w2[j]), a2=_pad_dir(rw_a2[j]),
                g2=rw_g2[j].astype(BF16), w0=rw_w0[j], a0=rw_a0[j],
                k_k=_row(rw_k_k[j]), k_a=_row(rw_k_a[j]), r_k=_row(rw_r_k[j]),
                ln_w=_row(rw_ln_w[j]), ln_b=_row(rw_ln_b[j]), w_o=rw_w_o[j].astype(BF16), bd=bd_head)
            P = _rw_project(xs, mod, _row(norm_mix[i]), w, nctx)
            yf, yb = _rw_scan(P, bd_head, nctx)
            xs = _rw_readout(xs, yf, yb, P[3], P[4], mod, w, nctx)
        else:
            w_in = gla_w_in[j]
            qk_width = gla_gk_up.shape[-1] * 2
            conv_ch = qk_width + d
            dk = gla_gk_up.shape[-1] // GLA_HEADS
            dv = d // GLA_HEADS
            rank = gla_gk_up.shape[2]
            lr0 = conv_ch + d
            w_lr = jnp.zeros((d, LANES), F32).at[:, :2 * rank].set(w_in[:, lr0:lr0 + 2 * rank])
            up = jnp.zeros((LANES, d), F32)
            up = up.at[:rank, :qk_width // 2].set(gla_gk_up[j, 0]).at[rank:2 * rank, qk_width // 2:].set(gla_gk_up[j, 1])
            w = dict(w_qkv=w_in[:, :conv_ch].astype(BF16), w_g=w_in[:, conv_ch:lr0].astype(BF16),
                     w_lr=w_lr.astype(BF16), up=up.astype(BF16),
                     gk_b=jnp.concatenate([gla_gk_b[j, 0], gla_gk_b[j, 1]]).reshape(1, d),
                     head_norm=_row(gla_head_norm[j]), w_o=gla_w_o[j].astype(BF16))
            z, g, la = _gla_project(xs, mod, _row(norm_mix[i]), w, nctx)
            qkv = _gla_conv(z, gla_conv[j].reshape(9, conv_ch), nctx, qk_width, float(dk) ** -0.5)
            of, ob = _gla_scan(qkv, la, tri, nctx, dk, dv)
            xs = _gla_readout(xs, of, ob, g, mod, w, nctx, dv)

        wq_t = peer_wq[i].T
        wq_hi = wq_t.astype(BF16)
        keys = peer_keys[i].reshape(PEER_HEADS * 2, N_KEYS, -1)
        key_hi = keys.astype(BF16)
        pw = dict(wq_hi=wq_hi, wq_lo=(wq_t - wq_hi.astype(F32)).astype(BF16),
                  key_hi=key_hi, key_lo=(keys - key_hi.astype(F32)).astype(BF16),
                  u=peer_u[i].astype(BF16), vt=peer_v[i].T.astype(BF16))
        ht, s0, s1, meta = _peer_route(xs, mod, _row(norm_ffn[i]), pw, nctx)
        xs = _peer_dense(xs, ht, s0, s1, meta, mod, pw, nctx)

    return _final_norm(xs, _row(final_norm), nctx)
```

```python
import functools

import numpy as np
import jax
import jax.numpy as jnp
from jax import lax
from jax.experimental import pallas as pl
from jax.experimental.pallas import tpu as pltpu

F32 = jnp.float32
BF16 = jnp.bfloat16

NORM_EPS = 1e-6
GRID_W = 64
RWKV_HEAD = 64
RWKV_GN_EPS = 64e-5
DECAY_LORA = 64
GLA_HEADS = 4
GLA_GATE_RANK = 16
GLA_GATE_NORM = 16.0
GLA_CHUNK = 64
PEER_HEADS = 8
N_KEYS = 128
PEER_TOPK = 16

LANES = 128
VMEM_LIMIT = 56 * 1024 * 1024

TB_PROJ = 128
TB_SCAN = 256
TB_ROUTE = 256
TB_DENSE = 256
EC_DENSE = 1024
DENSE_QUARTERS = 2


def _params(n_axes):
    return pltpu.CompilerParams(
        dimension_semantics=("arbitrary",) * n_axes,
        vmem_limit_bytes=VMEM_LIMIT)


def _full(a):
    nd = a.ndim
    return pl.BlockSpec(a.shape, lambda *_: (0,) * nd)


def _sigmoid(x):
    return 1.0 / (1.0 + jnp.exp(-x))


def _silu(x):
    return x * _sigmoid(x)


def _log_sigmoid(x):
    return jnp.minimum(x, 0.0) - jnp.log(1.0 + jnp.exp(-jnp.abs(x)))


def _modnorm(x, g, shift, scale):
    ms = jnp.mean(x * x, axis=-1, keepdims=True)
    return x * lax.rsqrt(ms + NORM_EPS) * (g * (1.0 + scale)) + shift


def _split2(x):
    hi = x.astype(BF16)
    lo = (x - hi.astype(F32)).astype(BF16)
    return hi, lo


def _split3(x):
    x1 = x.astype(BF16)
    r1 = x - x1.astype(F32)
    x2 = r1.astype(BF16)
    x3 = (r1 - x2.astype(F32)).astype(BF16)
    return x1, x2, x3


def _dot(a, b):
    return jnp.dot(a, b, preferred_element_type=F32)


def _dot_nt(a, b):
    return lax.dot_general(a, b, (((1,), (1,)), ((), ())), preferred_element_type=F32)


def _dot_tn(a, b):
    return lax.dot_general(a, b, (((0,), (0,)), ((), ())), preferred_element_type=F32)


def _dot3(a_hi, a_lo, b_hi, b_lo):
    return _dot(a_hi, b_hi) + _dot(a_hi, b_lo) + _dot(a_lo, b_hi)


def _seg_sum(x, bd):
    outs = []
    for c in range(x.shape[-1] // LANES):
        hi, lo = _split2(x[:, c * LANES:(c + 1) * LANES])
        outs.append(_dot(hi, bd) + _dot(lo, bd))
    return jnp.concatenate(outs, axis=-1)


def _block_diag_ones(group):
    i = np.arange(LANES)
    return jnp.asarray((i[:, None] // group) == (i[None, :] // group), dtype=BF16)


def _mod_kernel(c_ref, w_ref, b_ref, o_ref):
    a = _silu(c_ref[...])
    o_ref[...] = jnp.dot(a, w_ref[...], preferred_element_type=F32,
                         precision=lax.Precision.HIGHEST) + b_ref[...]


def _modulation(cc, ada_w, ada_b):
    depth, d, n = ada_w.shape
    nt = 768
    return pl.pallas_call(
        _mod_kernel,
        grid=(depth, n // nt),
        in_specs=[pl.BlockSpec(cc.shape, lambda i, k: (0, 0)),
                  pl.BlockSpec((None, d, nt), lambda i, k: (i, 0, k)),
                  pl.BlockSpec((None, 1, nt), lambda i, k: (i, 0, k))],
        out_specs=pl.BlockSpec((None, cc.shape[0], nt), lambda i, k: (i, 0, k)),
        out_shape=jax.ShapeDtypeStruct((depth, cc.shape[0], n), F32),
        compiler_params=_params(2),
        name="modulation",
    )(cc, ada_w, ada_b.reshape(depth, 1, n))


def _mod_spec(nb, d, ncb):
    return pl.BlockSpec((None, 6, d), lambda b, j: (jnp.where(j < ncb, nb, b), 0, 0))


def _rw_proj_kernel(x_ref, xp_ref, xn_ref, mod_ref, ng_ref, mix_ref, wrkv_ref,
                    w1_ref, a1_ref, g1_ref, w2_ref, a2_ref, g2_ref, w0_ref, a0_ref,
                    kk_w_ref, ka_ref, rk_ref, bd_ref,
                    r_o, v_o, kk_o, g_o, bon_o, w0_o, k0_o, b0_o, w1_o, k1_o, b1_o,
                    *, ncb, nblk):
    j = pl.program_id(1)
    tb = x_ref.shape[0]
    shift = mod_ref[0:1, :]
    scale = mod_ref[1:2, :]
    g = ng_ref[...]
    h = _modnorm(x_ref[...], g, shift, scale)
    hp = _modnorm(xp_ref[7:8, :], g, shift, scale)
    hn = _modnorm(xn_ref[0:1, :], g, shift, scale)
    hp = jnp.where((j != 0) & (j != ncb), hp, 0.0)
    hn = jnp.where((j != ncb - 1) & (j != nblk - 1), hn, 0.0)
    row = lax.broadcasted_iota(jnp.int32, (tb, 1), 0)
    prev = jnp.where(row == 0, hp, pltpu.roll(h, 1, 0))
    nxt = jnp.where(row == tb - 1, hn, pltpu.roll(h, tb - 1, 0))
    xx = 0.5 * (prev + nxt) - h
    xr, xw, xk, xv, xa, xg = (h + xx * mix_ref[i:i + 1, :] for i in range(6))

    r = _dot(xr.astype(BF16), wrkv_ref[0])
    k = _dot(xk.astype(BF16), wrkv_ref[1])
    v = _dot(xv.astype(BF16), wrkv_ref[2])
    gate = _dot(_sigmoid(_dot(xg.astype(BF16), g1_ref[...])).astype(BF16), g2_ref[...])
    tw = jnp.tanh(_dot(xw.astype(BF16), w1_ref[...])).astype(BF16)
    ta = _dot(xa.astype(BF16), a1_ref[...]).astype(BF16)

    bd = bd_ref[...]
    kk = k * kk_w_ref[...]
    kk = kk / jnp.maximum(jnp.sqrt(_seg_sum(kk * kk, bd)), 1e-12)

    r_o[...] = r
    v_o[...] = v
    kk_o[...] = kk
    g_o[...] = gate
    ksum = None
    for d, (w_o, k_o, b_o) in enumerate(((w0_o, k0_o, b0_o), (w1_o, k1_o, b1_o))):
        pre = w0_ref[d:d + 1, :] + _dot(tw, w2_ref[d])
        w_o[...] = jnp.exp(-_sigmoid(pre) * float(np.exp(-0.5)))
        a = _sigmoid(a0_ref[d:d + 1, :] + _dot(ta, a2_ref[d]))
        kd = k * (1.0 + (a - 1.0) * ka_ref[...])
        k_o[...] = kd
        b_o[...] = kk * a
        ksum = kd if ksum is None else ksum + kd
    bon_o[...] = _seg_sum(r * ksum * rk_ref[...], bd) * v


def _rw_project(xs, mod, ng, w, ncb_tokens):
    nb, t, d = xs.shape
    tb = TB_PROJ
    nblk, ncb = t // tb, ncb_tokens // tb
    sl = tb // 8
    weights = [ng, w["mix"], w["w_rkv"], w["w1"], w["a1"], w["g1"], w["w2"], w["a2"], w["g2"],
               w["w0"], w["a0"], w["k_k"], w["k_a"], w["r_k"], w["bd"]]
    tok = pl.BlockSpec((None, tb, d), lambda b, j: (b, j, 0))
    return pl.pallas_call(
        functools.partial(_rw_proj_kernel, ncb=ncb, nblk=nblk),
        grid=(nb, nblk),
        in_specs=[tok,
                  pl.BlockSpec((None, 8, d), lambda b, j: (b, jnp.maximum(j * sl - 1, 0), 0)),
                  pl.BlockSpec((None, 8, d), lambda b, j: (b, jnp.minimum((j + 1) * sl, t // 8 - 1), 0)),
                  _mod_spec(nb, d, ncb)] + [_full(a) for a in weights],
        out_specs=[tok] * 11,
        out_shape=[jax.ShapeDtypeStruct((nb, t, d), F32)] * 11,
        compiler_params=_params(2),
        name="rwkv_project",
    )(xs, xs, xs, mod, *weights)


def _rw_scan_kernel(rf, wf, kf, vf, kkf, bf, rb, wb, kb, vb, kkb, bb, bd_ref,
                    yf_o, yb_o, s_ref):
    tb = rf.shape[0]
    nch = rf.shape[1] // LANES

    @pl.when(pl.program_id(1) == 0)
    def _():
        s_ref[...] = jnp.zeros_like(s_ref)

    bd = bd_ref[...]
    hw = RWKV_HEAD
    vi = lax.broadcasted_iota(jnp.int32, (hw, LANES), 0)
    li = lax.broadcasted_iota(jnp.int32, (hw, LANES), 1)
    diag = ((li & (hw - 1)) == vi).astype(F32)
    lane = lax.broadcasted_iota(jnp.int32, (1, LANES), 1)
    first_head = lane < hw
    head_mask = (first_head.astype(F32), 1.0 - first_head.astype(F32))
    sub = 16

    gch = 4
    groups = [(d, list(range(c0, c0 + gch))) for c0 in range(0, nch, gch) for d in (0, 1)]
    fwd = (rf, wf, kf, vf, kkf, bf)
    bwd = (rb, wb, kb, vb, kkb, bb)

    def rows_of(s, i):
        return s[i * hw:(i + 1) * hw]

    def body(g, carry):
        base = (pl.multiple_of(g * sub, sub), pl.multiple_of(tb - sub - g * sub, sub))
        lanes = [slice(c * LANES, (c + 1) * LANES) for c in range(nch)]
        tiles = [[[ref[pl.ds(base[d], sub), ls] for ref in refs] for ls in lanes]
                 for d, refs in enumerate((fwd, bwd))]

        def rows(d, c, n):
            m = n if d == 0 else sub - 1 - n
            return [t[m:m + 1] for t in tiles[d][c]]

        def start(d, cs, s, n):
            vcol = _dot(jnp.concatenate([rows(d, c, n)[3] * diag for c in cs], axis=0).astype(BF16), bd)
            sa = _dot(jnp.concatenate([s[i] * rows(d, c, n)[4] for i, c in enumerate(cs)], axis=0).astype(BF16), bd)
            return vcol, sa

        state = [[s_ref[d, c] for c in cs] for d, cs in groups]
        pend = [start(d, cs, state[gi], 0) for gi, (d, cs) in enumerate(groups)]
        ys = [[[] for _ in range(nch)] for _ in range(2)]
        for n in range(sub):
            for gi, (d, cs) in enumerate(groups):
                vcol, sa = pend[gi]
                s = []
                for i, c in enumerate(cs):
                    r, w, k, v, kk, b = rows(d, c, n)
                    s.append(state[gi][i] * w - rows_of(sa, i) * b + rows_of(vcol, i) * k)
                state[gi] = s
                if n + 1 < sub:
                    pend[gi] = start(d, cs, s, n + 1)
                lhs = jnp.concatenate([rows(d, c, n)[0] * head_mask[h] for c in cs for h in range(2)], axis=0)
                out = _dot_nt(lhs.astype(BF16), jnp.concatenate(s, axis=0).astype(BF16))
                for m in range(gch // 2):
                    x = out[:, m * LANES:(m + 1) * LANES]
                    xr = pltpu.roll(x, hw, 1)
                    r0 = 4 * m
                    ys[d][cs[2 * m]].append(jnp.where(first_head, x[r0:r0 + 1], xr[r0 + 1:r0 + 2]))
                    ys[d][cs[2 * m + 1]].append(jnp.where(first_head, xr[r0 + 2:r0 + 3], x[r0 + 3:r0 + 4]))
        for gi, (d, cs) in enumerate(groups):
            for i, c in enumerate(cs):
                s_ref[d, c] = state[gi][i]
        for c, ls in enumerate(lanes):
            yf_o[pl.ds(base[0], sub), ls] = jnp.concatenate(ys[0][c], axis=0)
            yb_o[pl.ds(base[1], sub), ls] = jnp.concatenate(ys[1][c][::-1], axis=0)
        return carry

    lax.fori_loop(0, tb // sub, body, 0)


def _bwd_block(j, ncb, nblk):
    return jnp.where(j < ncb, ncb - 1 - j, nblk - 1 - (j - ncb))


def _rw_scan(P, bd, ncb_tokens):
    r, v, kk, _, _, w0, k0, b0, w1, k1, b1 = P
    nb, t, d = r.shape
    tb = TB_SCAN
    nblk, ncb = t // tb, ncb_tokens // tb
    fs = pl.BlockSpec((None, tb, d), lambda b, j: (b, j, 0))
    bs = pl.BlockSpec((None, tb, d), lambda b, j: (b, _bwd_block(j, ncb, nblk), 0))
    return pl.pallas_call(
        _rw_scan_kernel,
        grid=(nb, nblk),
        in_specs=[fs] * 6 + [bs] * 6 + [_full(bd)],
        out_specs=[fs, bs],
        out_shape=[jax.ShapeDtypeStruct((nb, t, d), F32)] * 2,
        scratch_shapes=[pltpu.VMEM((2, d // LANES, RWKV_HEAD, LANES), F32)],
        compiler_params=_params(2),
        name="rwkv_scan",
    )(r, w0, k0, v, kk, b0, r, w1, k1, v, kk, b1, bd)


def _rw_out_kernel(x_ref, yf_ref, yb_ref, g_ref, bon_ref, mod_ref, lnw_ref, lnb_ref, wo_ref, bd_ref, o_ref):
    bd = bd_ref[...]
    y = yf_ref[...] + yb_ref[...]
    mu = _seg_sum(y, bd) * (1.0 / RWKV_HEAD)
    yc = y - mu
    var = _seg_sum(yc * yc, bd) * (1.0 / RWKV_HEAD)
    yn = yc * lax.rsqrt(var + RWKV_GN_EPS) * lnw_ref[...] + lnb_ref[...]
    z = ((yn + bon_ref[...]) * g_ref[...]).astype(BF16)
    o_ref[...] = x_ref[...] + mod_ref[2:3, :] * _dot(z, wo_ref[...])


def _rw_readout(xs, yf, yb, gate, bon, mod, w, ncb_tokens):
    nb, t, d = xs.shape
    tb = TB_SCAN
    tok = pl.BlockSpec((None, tb, d), lambda b, j: (b, j, 0))
    weights = [w["ln_w"], w["ln_b"], w["w_o"], w["bd"]]
    return pl.pallas_call(
        _rw_out_kernel,
        grid=(nb, t // tb),
        in_specs=[tok] * 5 + [_mod_spec(nb, d, ncb_tokens // tb)] + [_full(a) for a in weights],
        out_specs=tok,
        out_shape=jax.ShapeDtypeStruct((nb, t, d), F32),
        compiler_params=_params(2),
        name="rwkv_readout",
    )(xs, yf, yb, gate, bon, mod, *weights)


def _gla_proj_kernel(x_ref, mod_ref, ng_ref, wqkv_ref, wg_ref, wlr_ref, up_ref, gkb_ref,
                     z_o, g_o, la_o):
    h = _modnorm(x_ref[...], ng_ref[...], mod_ref[0:1, :], mod_ref[1:2, :])
    hb = h.astype(BF16)
    z_o[...] = _dot(hb, wqkv_ref[...])
    g_o[...] = _dot(hb, wg_ref[...])
    lr = _dot(hb, wlr_ref[...]).astype(BF16)
    la_o[...] = _log_sigmoid(_dot(lr, up_ref[...]) + gkb_ref[...]) * (1.0 / GLA_GATE_NORM)


def _gla_project(xs, mod, ng, w, ncb_tokens):
    nb, t, d = xs.shape
    tb = TB_SCAN
    weights = [ng, w["w_qkv"], w["w_g"], w["w_lr"], w["up"], w["gk_b"]]
    tok = pl.BlockSpec((None, tb, d), lambda b, j: (b, j, 0))
    tok2 = pl.BlockSpec((None, tb, 2 * d), lambda b, j: (b, j, 0))
    return pl.pallas_call(
        _gla_proj_kernel,
        grid=(nb, t // tb),
        in_specs=[tok, _mod_spec(nb, d, ncb_tokens // tb)] + [_full(a) for a in weights],
        out_specs=[tok2, tok, tok],
        out_shape=[jax.ShapeDtypeStruct((nb, t, 2 * d), F32),
                   jax.ShapeDtypeStruct((nb, t, d), F32),
                   jax.ShapeDtypeStruct((nb, t, d), F32)],
        compiler_params=_params(2),
        name="gla_project",
    )(xs, mod, *weights)


def _gla_conv_kernel(zc_ref, zp_ref, zn_ref, cw_ref, o_ref, *, nblk, qk_width, q_scale):
    j = pl.program_id(1)
    tb, ch = zc_ref.shape
    is_ctx = j == 0
    up_ok = jnp.where(j > 1, 1.0, 0.0)
    dn_ok = jnp.where((j > 0) & (j < nblk - 1), 1.0, 0.0)
    vert = jnp.where(is_ctx, 0.0, 1.0)
    ext = tb + 2 * GRID_W
    pos = lax.broadcasted_iota(jnp.int32, (ext, 1), 0) - GRID_W
    col = jnp.where(is_ctx, pos, pos & (GRID_W - 1))
    no_left = col == 0
    no_right = col == jnp.where(is_ctx, tb - 1, GRID_W - 1)
    cc = 256
    for c in range(ch // cc):
        cs = slice(c * cc, (c + 1) * cc)
        e = jnp.concatenate([zp_ref[:, cs] * up_ok, zc_ref[:, cs], zn_ref[:, cs] * dn_ok], axis=0)
        em = jnp.where(no_left, 0.0, pltpu.roll(e, 1, 0))
        ep = jnp.where(no_right, 0.0, pltpu.roll(e, ext - 1, 0))
        acc = None
        for dy in range(3):
            lo = dy * GRID_W
            for dx, src in enumerate((em, e, ep)):
                wt = cw_ref[dy * 3 + dx:dy * 3 + dx + 1, cs]
                if dy != 1:
                    wt = wt * vert
                term = src[lo:lo + tb] * wt
                acc = term if acc is None else acc + term
        y = _silu(acc)
        if (c + 1) * cc <= qk_width // 2:
            y = y * q_scale
        o_ref[:, cs] = y


def _gla_conv(z, cw, ncb_tokens, qk_width, q_scale):
    nb, t, ch = z.shape
    tb = TB_SCAN
    assert ncb_tokens == tb and tb % GRID_W == 0
    sl = tb // GRID_W
    nblk = t // tb
    return pl.pallas_call(
        functools.partial(_gla_conv_kernel, nblk=nblk, qk_width=qk_width, q_scale=q_scale),
        grid=(nb, nblk),
        in_specs=[pl.BlockSpec((None, tb, ch), lambda b, j: (b, j, 0)),
                  pl.BlockSpec((None, GRID_W, ch), lambda b, j: (b, jnp.maximum(j * sl - 1, 0), 0)),
                  pl.BlockSpec((None, GRID_W, ch),
                               lambda b, j: (b, jnp.minimum((j + 1) * sl, t // GRID_W - 1), 0)),
                  _full(cw)],
        out_specs=pl.BlockSpec((None, tb, ch), lambda b, j: (b, j, 0)),
        out_shape=jax.ShapeDtypeStruct((nb, t, ch), F32),
        compiler_params=_params(2),
        name="gla_conv",
    )(z, z, z, cw)


def _gla_scan_kernel(qf_ref, laf_ref, qb_ref, lab_ref, tri_ref, of_o, ob_o, st_ref, *, dk, dv):
    tb = qf_ref.shape[0]
    c = GLA_CHUNK
    nh = GLA_HEADS
    qkw = nh * dk

    @pl.when(pl.program_id(1) == 0)
    def _():
        st_ref[...] = jnp.zeros_like(st_ref)

    for d, (q_ref, la_ref, o_o) in enumerate(((qf_ref, laf_ref, of_o), (qb_ref, lab_ref, ob_o))):
        tri = tri_ref[d]
        keep = tri > 0
        ref_row = c // 2 if d == 0 else c - 1 - c // 2
        last_row = c - 1 if d == 0 else 0
        chunks = range(tb // c) if d == 0 else range(tb // c - 1, -1, -1)
        for n in chunks:
            rows = slice(n * c, (n + 1) * c)
            for hh in range(nh):
                q = q_ref[rows, hh * dk:(hh + 1) * dk]
                k = q_ref[rows, qkw + hh * dk:qkw + (hh + 1) * dk]
                v = q_ref[rows, 2 * qkw + hh * dv:2 * qkw + (hh + 1) * dv]
                la = la_ref[rows, d * qkw + hh * dk:d * qkw + (hh + 1) * dk]
                l1, l2, l3 = _split3(la)
                bcum = _dot(tri, l1) + _dot(tri, l2) + _dot(tri, l3)
                ref = bcum[ref_row:ref_row + 1]
                last = bcum[last_row:last_row + 1]
                a = _dot_nt((q * jnp.exp(bcum - ref)).astype(BF16), (k * jnp.exp(ref - bcum)).astype(BF16))
                a = jnp.where(keep, a, 0.0)
                vb = v.astype(BF16)
                st = st_ref[d, hh]
                o = _dot(a.astype(BF16), vb) + _dot_nt((q * jnp.exp(bcum)).astype(BF16), st.astype(BF16))
                o_o[rows, hh * dv:(hh + 1) * dv] = o
                kd = (k * jnp.exp(last - bcum)).astype(BF16)
                st_ref[d, hh] = st * jnp.exp(last) + _dot_tn(vb, kd)


def _gla_scan(qkv, la, tri, ncb_tokens, dk, dv):
    nb, t, ch = qkv.shape
    d = la.shape[-1]
    tb = TB_SCAN
    nblk, ncb = t // tb, ncb_tokens // tb
    f2 = pl.BlockSpec((None, tb, ch), lambda b, j: (b, j, 0))
    b2 = pl.BlockSpec((None, tb, ch), lambda b, j: (b, _bwd_block(j, ncb, nblk), 0))
    f1 = pl.BlockSpec((None, tb, d), lambda b, j: (b, j, 0))
    b1 = pl.BlockSpec((None, tb, d), lambda b, j: (b, _bwd_block(j, ncb, nblk), 0))
    return pl.pallas_call(
        functools.partial(_gla_scan_kernel, dk=dk, dv=dv),
        grid=(nb, nblk),
        in_specs=[f2, f1, b2, b1, _full(tri)],
        out_specs=[f1, b1],
        out_shape=[jax.ShapeDtypeStruct((nb, t, d), F32)] * 2,
        scratch_shapes=[pltpu.VMEM((2, GLA_HEADS, dv, dk), F32)],
        compiler_params=_params(2),
        name="gla_scan",
    )(qkv, la, qkv, la, tri)


def _gla_out_kernel(x_ref, of_ref, ob_ref, g_ref, mod_ref, hn_ref, wo_ref, o_ref, *, dv):
    o = of_ref[...] + ob_ref[...]
    parts = []
    for hh in range(GLA_HEADS):
        oh = o[:, hh * dv:(hh + 1) * dv]
        ms = jnp.mean(oh * oh, axis=-1, keepdims=True)
        parts.append(oh * lax.rsqrt(ms + NORM_EPS) * hn_ref[...])
    on = jnp.concatenate(parts, axis=-1)
    z = (on * _silu(g_ref[...])).astype(BF16)
    o_ref[...] = x_ref[...] + mod_ref[2:3, :] * _dot(z, wo_ref[...])


def _gla_readout(xs, of, ob, g, mod, w, ncb_tokens, dv):
    nb, t, d = xs.shape
    tb = TB_SCAN
    tok = pl.BlockSpec((None, tb, d), lambda b, j: (b, j, 0))
    weights = [w["head_norm"], w["w_o"]]
    return pl.pallas_call(
        functools.partial(_gla_out_kernel, dv=dv),
        grid=(nb, t // tb),
        in_specs=[tok] * 4 + [_mod_spec(nb, d, ncb_tokens // tb)] + [_full(a) for a in weights],
        out_specs=tok,
        out_shape=jax.ShapeDtypeStruct((nb, t, d), F32),
        compiler_params=_params(2),
        name="gla_readout",
    )(xs, of, ob, g, mod, *weights)


def _top_values(s, k):
    vals = []
    cur = s
    for _ in range(k):
        m = jnp.max(cur, axis=0, keepdims=True)
        vals.append(m)
        cur = jnp.where(cur == m, -jnp.inf, cur)
    return vals


def _peer_route_kernel(x_ref, mod_ref, ng_ref, wq_ref, key_hi_ref, key_lo_ref,
                       ht_o, s0_o, s1_o, meta_o, q_hi_ref, q_lo_ref):
    h = _modnorm(x_ref[...], ng_ref[...], mod_ref[3:4, :], mod_ref[4:5, :])
    hb = h.T.astype(BF16)
    ht_o[...] = hb
    qt = _dot(wq_ref[...], hb)
    q_hi, q_lo = _split2(qt)
    q_hi_ref[...] = q_hi
    q_lo_ref[...] = q_lo
    nk = N_KEYS
    tb = x_ref.shape[0]
    zeros = jnp.zeros((4, tb), F32)

    def head(hd, carry):
        sv = []
        for p in range(2):
            hp = hd * 2 + p
            rows = pl.ds(pl.multiple_of(hp * nk, nk), nk)
            s = _dot3(key_hi_ref[hp], key_lo_ref[hp], q_hi_ref[rows, :], q_lo_ref[rows, :])
            (s0_o, s1_o)[p][hd] = s
            sv.append(_top_values(s, PEER_TOPK + 1))
        k, half = PEER_TOPK, PEER_TOPK // 2
        sv0 = jnp.concatenate(sv[0][:k], axis=0)
        sv1 = jnp.concatenate(sv[1][:k], axis=0)
        edge = jnp.concatenate([sv[0][k] + sv[1][0], sv[0][0] + sv[1][k],
                                jnp.full((6, tb), -jnp.inf, F32)], axis=0)
        cand = jnp.concatenate([sv[0][0] + sv1]
                               + [sv[0][a] + sv1[:half] for a in range(1, half)]
                               + [sv0[half:] + sv[1][0], edge], axis=0)
        top = _top_values(cand, k + 1)
        z = top[0] * 0.0
        for cval in top[:k]:
            z = z + jnp.exp(cval - top[0])
        tau = 0.5 * (top[k - 1] + top[k])
        meta_o[hd] = jnp.concatenate([tau, sv[0][0], sv[1][0], 1.0 / z, zeros], axis=0)
        return carry

    lax.fori_loop(0, PEER_HEADS, head, 0)


def _peer_route(xs, mod, ng, w, ncb_tokens):
    nb, t, d = xs.shape
    tb = TB_ROUTE
    nq = w["wq"].shape[0]
    weights = [ng, w["wq"], w["key_hi"], w["key_lo"]]
    sspec = pl.BlockSpec((None, PEER_HEADS, N_KEYS, tb), lambda b, j: (b, 0, 0, j))
    return pl.pallas_call(
        _peer_route_kernel,
        grid=(nb, t // tb),
        in_specs=[pl.BlockSpec((None, tb, d), lambda b, j: (b, j, 0)),
                  _mod_spec(nb, d, ncb_tokens // tb)] + [_full(a) for a in weights],
        out_specs=[pl.BlockSpec((None, d, tb), lambda b, j: (b, 0, j)), sspec, sspec,
                   pl.BlockSpec((None, PEER_HEADS, 8, tb), lambda b, j: (b, 0, 0, j))],
        out_shape=[jax.ShapeDtypeStruct((nb, d, t), BF16),
                   jax.ShapeDtypeStruct((nb, PEER_HEADS, N_KEYS, t), F32),
                   jax.ShapeDtypeStruct((nb, PEER_HEADS, N_KEYS, t), F32),
                   jax.ShapeDtypeStruct((nb, PEER_HEADS, 8, t), F32)],
        scratch_shapes=[pltpu.VMEM((nq, tb), BF16), pltpu.VMEM((nq, tb), BF16)],
        compiler_params=_params(2),
        name="peer_route",
    )(xs, mod, *weights)


def _peer_dense_kernel(x_ref, ht_ref, s0_ref, s1_ref, meta_ref, modl_ref, modc_ref, u_ref, vt_ref, o_ref,
                       acc_ref, th_ref, e0_ref, e1_ref, act0_ref, act1_ref, wa0_ref, wa1_ref,
                       *, nctx, nec, ntile, nsteps):
    g = pl.program_id(0)
    ne = u_ref.shape[0]
    d, tb = acc_ref.shape
    nk = N_KEYS
    nq = DENSE_QUARTERS
    qe, qd = ne // nq, d // nq
    qi = qe // nk
    g_b = jnp.clip(g - 1, 0, nsteps - 1)
    g_c = jnp.clip(g - 2, 0, nsteps - 1)
    e_b = g_b % nec
    e_c = g_c % nec
    tok0 = ((g_c // nec) % ntile) * tb

    @pl.when(g == 0)
    def _():
        for ref in (act0_ref, act1_ref, wa0_ref, wa1_ref, th_ref, e0_ref):
            ref[...] = jnp.zeros_like(ref)

    @pl.when(e_b == 0)
    def _():
        for hd in range(PEER_HEADS):
            e1_ref[hd] = jnp.exp(s1_ref[hd] - meta_ref[hd, 2:3, :])
            s0 = s0_ref[hd]
            th = meta_ref[hd, 0:1, :] - s0
            e0 = jnp.exp(s0 - meta_ref[hd, 1:2, :]) * meta_ref[hd, 3:4, :]
            for p in range(nk // qi):
                th_ref[hd, p, 0:qi, :] = th[p * qi:(p + 1) * qi]
                e0_ref[hd, p, 0:qi, :] = e0[p * qi:(p + 1) * qi]

    @pl.when(e_c == 0)
    def _():
        acc_ref[...] = jnp.zeros_like(acc_ref)

    jr = 32

    def quarter(act_new, act_old, wa_new, wa_old, q, carry):
        r0 = pl.multiple_of(q * qe, qe)
        d0 = pl.multiple_of(q * qd, qd)
        pair = e_b * nq + q
        a = _dot(u_ref[pl.ds(r0, qe), :], ht_ref[...])
        c = _dot(vt_ref[pl.ds(d0, qd), :], wa_old[...])
        for jq in range(nk // jr):
            rows = slice(jq * jr, (jq + 1) * jr)
            accs = [None] * qi
            for hd in range(PEER_HEADS):
                s1 = s1_ref[hd, rows, :]
                e1 = e1_ref[hd, rows, :]
                th = th_ref[hd, pair]
                e0 = e0_ref[hd, pair]
                for ii in range(qi):
                    term = jnp.where(s1 >= th[ii:ii + 1], e1, 0.0) * e0[ii:ii + 1]
                    accs[ii] = term if accs[ii] is None else accs[ii] + term
            for ii in range(qi):
                er = pl.ds(pl.multiple_of(r0 + ii * nk + jq * jr, jr), jr)
                wa_new[er, :] = (accs[ii] * act_old[er, :]).astype(BF16)

        act_new[pl.ds(r0, qe), :] = 0.5 * a * (1.0 + lax.erf(a * float(1.0 / np.sqrt(2.0))))
        acc_ref[pl.ds(d0, qd), :] += c
        return carry

    @pl.when(g % 2 == 0)
    def _():
        lax.fori_loop(0, nq, functools.partial(quarter, act0_ref, act1_ref, wa0_ref, wa1_ref), 0)

    @pl.when(g % 2 == 1)
    def _():
        lax.fori_loop(0, nq, functools.partial(quarter, act1_ref, act0_ref, wa1_ref, wa0_ref), 0)

    @pl.when((e_c == nec - 1) & (g >= 2))
    def _():
        tok = tok0 + lax.broadcasted_iota(jnp.int32, (tb, 1), 0)
        gate = jnp.where(tok < nctx, modc_ref[5:6, :], modl_ref[5:6, :])
        o_ref[...] = x_ref[...] + gate * acc_ref[...].T


def _peer_dense(xs, ht, s0, s1, meta, mod, w, ncb_tokens):
    nb, t, d = xs.shape
    tb = TB_DENSE
    ne = EC_DENSE
    n_exp = w["u"].shape[0]
    assert t % tb == 0 and n_exp % ne == 0 and ne % (DENSE_QUARTERS * N_KEYS) == 0
    ntile, nec = t // tb, n_exp // ne
    nsteps = nb * ntile * nec
    pairs = N_KEYS // (ne // DENSE_QUARTERS // N_KEYS)

    def at(lag):
        def f(g):
            s = jnp.clip(g - lag, 0, nsteps - 1)
            return s // (ntile * nec), (s // nec) % ntile, s % nec
        return f

    sa, sb, sc = at(0), at(1), at(2)
    sspec = pl.BlockSpec((None, PEER_HEADS, N_KEYS, tb), lambda g: (sb(g)[0], 0, 0, sb(g)[1]))
    return pl.pallas_call(
        functools.partial(_peer_dense_kernel, nctx=ncb_tokens, nec=nec, ntile=ntile, nsteps=nsteps),
        grid=(nsteps + 2,),
        in_specs=[pl.BlockSpec((None, tb, d), lambda g: (sc(g)[0], sc(g)[1], 0)),
                  pl.BlockSpec((None, d, tb), lambda g: (sa(g)[0], 0, sa(g)[1])),
                  sspec, sspec,
                  pl.BlockSpec((None, PEER_HEADS, 8, tb), lambda g: (sb(g)[0], 0, 0, sb(g)[1])),
                  pl.BlockSpec((None, 6, d), lambda g: (sc(g)[0], 0, 0)),
                  pl.BlockSpec((None, 6, d), lambda g: (nb, 0, 0)),
                  pl.BlockSpec((ne, d), lambda g: (sa(g)[2], 0)),
                  pl.BlockSpec((d, ne), lambda g: (0, sc(g)[2]))],
        out_specs=pl.BlockSpec((None, tb, d), lambda g: (sc(g)[0], sc(g)[1], 0)),
        out_shape=jax.ShapeDtypeStruct((nb, t, d), F32),
        scratch_shapes=[pltpu.VMEM((d, tb), F32),
                        pltpu.VMEM((PEER_HEADS, pairs, 8, tb), F32),
                        pltpu.VMEM((PEER_HEADS, pairs, 8, tb), F32),
                        pltpu.VMEM((PEER_HEADS, N_KEYS, tb), F32),
                        pltpu.VMEM((ne, tb), F32), pltpu.VMEM((ne, tb), F32),
                        pltpu.VMEM((ne, tb), BF16), pltpu.VMEM((ne, tb), BF16)],
        compiler_params=_params(1),
        name="peer_dense",
    )(xs, ht, s0, s1, meta, mod, mod, w["u"], w["vt"])


def _final_norm_kernel(x_ref, g_ref, o_ref):
    x = x_ref[...]
    ms = jnp.mean(x * x, axis=-1, keepdims=True)
    o_ref[...] = x * lax.rsqrt(ms + NORM_EPS) * g_ref[...]


def _final_norm(xs, g, ncb_tokens):
    nb, t, d = xs.shape
    tb = TB_SCAN
    ncb = ncb_tokens // tb
    return pl.pallas_call(
        _final_norm_kernel,
        grid=(nb, t // tb - ncb),
        in_specs=[pl.BlockSpec((None, tb, d), lambda b, j: (b, j + ncb, 0)), _full(g)],
        out_specs=pl.BlockSpec((None, tb, d), lambda b, j: (b, j, 0)),
        out_shape=jax.ShapeDtypeStruct((nb, t - ncb_tokens, d), F32),
        compiler_params=_params(2),
        name="final_norm",
    )(xs, g)


def _row(a):
    return a.reshape(1, -1)


def _pad_dir(w2):
    z = jnp.zeros_like(w2[0])
    return jnp.stack([jnp.concatenate([w2[0], z], axis=0), jnp.concatenate([z, w2[1]], axis=0)]).astype(BF16)


def kernel(x, c, ctx, c_ctx, ada_w, ada_b, norm_mix, norm_ffn, rw_mix, rw_w_rkv, rw_w0, rw_w1, rw_w2, rw_a0, rw_a1, rw_a2, rw_g1, rw_g2, rw_k_k, rw_k_a, rw_r_k, rw_ln_w, rw_ln_b, rw_w_o, gla_w_in, gla_conv, gla_gk_up, gla_gk_b, gla_head_norm, gla_w_o, peer_wq, peer_keys, peer_u, peer_v, final_norm):
    nb, seq, d = x.shape
    nctx = ctx.shape[1]
    depth = ada_w.shape[0]
    assert nctx == TB_SCAN and seq % TB_SCAN == 0 and d % LANES == 0

    xs = jnp.concatenate([ctx, x], axis=1)
    cc = jnp.zeros((16, d), F32).at[:nb].set(c).at[nb].set(c_ctx)
    mods = _modulation(cc, ada_w, ada_b)[:, :nb + 1].reshape(depth, nb + 1, 6, d)

    bd_head = _block_diag_ones(RWKV_HEAD)
    ci = np.arange(GLA_CHUNK)
    tri = jnp.asarray(np.stack([ci[None, :] <= ci[:, None], ci[None, :] >= ci[:, None]]), dtype=BF16)

    for i in range(depth):
        mod = mods[i]
        j = i // 2
        if i % 2 == 0:
            w = dict(
                mix=rw_mix[j], w_rkv=rw_w_rkv[j].astype(BF16),
                w1=jnp.concatenate([rw_w1[j, 0], rw_w1[j, 1]], axis=1).astype(BF16),
                a1=jnp.concatenate([rw_a1[j, 0], rw_a1[j, 1]], axis=1).astype(BF16),
                g1=rw_g1[j].astype(BF16), w2=_pad_dir(rw_w2[j]), a2=_pad_dir(rw_a2[j]),
                g2=rw_g2[j].astype(BF16), w0=rw_w0[j], a0=rw_a0[j],
                k_k=_row(rw_k_k[j]), k_a=_row(rw_k_a[j]), r_k=_row(rw_r_k[j]),
                ln_w=_row(rw_ln_w[j]), ln_b=_row(rw_ln_b[j]), w_o=rw_w_o[j].astype(BF16), bd=bd_head)
            P = _rw_project(xs, mod, _row(norm_mix[i]), w, nctx)
            yf, yb = _rw_scan(P, bd_head, nctx)
            xs = _rw_readout(xs, yf, yb, P[3], P[4], mod, w, nctx)
        else:
            w_in = gla_w_in[j]
            qk_width = gla_gk_up.shape[-1] * 2
            conv_ch = qk_width + d
            dk = gla_gk_up.shape[-1] // GLA_HEADS
            dv = d // GLA_HEADS
            rank = gla_gk_up.shape[2]
            lr0 = conv_ch + d
            w_lr = jnp.zeros((d, LANES), F32).at[:, :2 * rank].set(w_in[:, lr0:lr0 + 2 * rank])
            up = jnp.zeros((LANES, d), F32)
            up = up.at[:rank, :qk_width // 2].set(gla_gk_up[j, 0]).at[rank:2 * rank, qk_width // 2:].set(gla_gk_up[j, 1])
            w = dict(w_qkv=w_in[:, :conv_ch].astype(BF16), w_g=w_in[:, conv_ch:lr0].astype(BF16),
                     w_lr=w_lr.astype(BF16), up=up.astype(BF16),
                     gk_b=jnp.concatenate([gla_gk_b[j, 0], gla_gk_b[j, 1]]).reshape(1, d),
                     head_norm=_row(gla_head_norm[j]), w_o=gla_w_o[j].astype(BF16))
            z, g, la = _gla_project(xs, mod, _row(norm_mix[i]), w, nctx)
            qkv = _gla_conv(z, gla_conv[j].reshape(9, conv_ch), nctx, qk_width, float(dk) ** -0.5)
            of, ob = _gla_scan(qkv, la, tri, nctx, dk, dv)
            xs = _gla_readout(xs, of, ob, g, mod, w, nctx, dv)

        keys = peer_keys[i].reshape(PEER_HEADS * 2, N_KEYS, -1)
        key_hi = keys.astype(BF16)
        pw = dict(wq=peer_wq[i].T.astype(BF16),
                  key_hi=key_hi, key_lo=(keys - key_hi.astype(F32)).astype(BF16),
                  u=peer_u[i].astype(BF16), vt=peer_v[i].T.astype(BF16))
        ht, s0, s1, meta = _peer_route(xs, mod, _row(norm_ffn[i]), pw, nctx)
        xs = _peer_dense(xs, ht, s0, s1, meta, mod, pw, nctx)

    return _final_norm(xs, _row(final_norm), nctx)
```

```python
import functools

import numpy as np
import jax
import jax.numpy as jnp
from jax import lax
from jax.experimental import pallas as pl
from jax.experimental.pallas import tpu as pltpu

F32 = jnp.float32
BF16 = jnp.bfloat16

NORM_EPS = 1e-6
GRID_W = 64
RWKV_HEAD = 64
RWKV_GN_EPS = 64e-5
DECAY_LORA = 64
GLA_HEADS = 4
GLA_GATE_RANK = 16
GLA_GATE_NORM = 16.0
GLA_CHUNK = 64
PEER_HEADS = 8
N_KEYS = 128
PEER_TOPK = 16

LANES = 128
VMEM_LIMIT = 56 * 1024 * 1024

TB_PROJ = 128
TB_SCAN = 256
TB_ROUTE = 256
TB_DENSE = 256
EC_DENSE = 1024
DENSE_QUARTERS = 4


def _params(n_axes):
    return pltpu.CompilerParams(
        dimension_semantics=("arbitrary",) * n_axes,
        vmem_limit_bytes=VMEM_LIMIT)


def _full(a):
    nd = a.ndim
    return pl.BlockSpec(a.shape, lambda *_: (0,) * nd)


def _sigmoid(x):
    return 1.0 / (1.0 + jnp.exp(-x))


def _silu(x):
    return x * _sigmoid(x)


def _log_sigmoid(x):
    return jnp.minimum(x, 0.0) - jnp.log(1.0 + jnp.exp(-jnp.abs(x)))


def _modnorm(x, g, shift, scale):
    ms = jnp.mean(x * x, axis=-1, keepdims=True)
    return x * lax.rsqrt(ms + NORM_EPS) * (g * (1.0 + scale)) + shift


def _split2(x):
    hi = x.astype(BF16)
    lo = (x - hi.astype(F32)).astype(BF16)
    return hi, lo


def _split3(x):
    x1 = x.astype(BF16)
    r1 = x - x1.astype(F32)
    x2 = r1.astype(BF16)
    x3 = (r1 - x2.astype(F32)).astype(BF16)
    return x1, x2, x3


def _dot(a, b):
    return jnp.dot(a, b, preferred_element_type=F32)


def _dot_nt(a, b):
    return lax.dot_general(a, b, (((1,), (1,)), ((), ())), preferred_element_type=F32)


def _dot_tn(a, b):
    return lax.dot_general(a, b, (((0,), (0,)), ((), ())), preferred_element_type=F32)


def _dot3(a_hi, a_lo, b_hi, b_lo):
    return _dot(a_hi, b_hi) + _dot(a_hi, b_lo) + _dot(a_lo, b_hi)


def _seg_sum(x, bd):
    outs = []
    for c in range(x.shape[-1] // LANES):
        hi, lo = _split2(x[:, c * LANES:(c + 1) * LANES])
        outs.append(_dot(hi, bd) + _dot(lo, bd))
    return jnp.concatenate(outs, axis=-1)


def _block_diag_ones(group):
    i = np.arange(LANES)
    return jnp.asarray((i[:, None] // group) == (i[None, :] // group), dtype=BF16)


def _mod_kernel(c_ref, w_ref, b_ref, o_ref):
    a = _silu(c_ref[...])
    o_ref[...] = jnp.dot(a, w_ref[...], preferred_element_type=F32,
                         precision=lax.Precision.HIGHEST) + b_ref[...]


def _modulation(cc, ada_w, ada_b):
    depth, d, n = ada_w.shape
    nt = 768
    return pl.pallas_call(
        _mod_kernel,
        grid=(depth, n // nt),
        in_specs=[pl.BlockSpec(cc.shape, lambda i, k: (0, 0)),
                  pl.BlockSpec((None, d, nt), lambda i, k: (i, 0, k)),
                  pl.BlockSpec((None, 1, nt), lambda i, k: (i, 0, k))],
        out_specs=pl.BlockSpec((None, cc.shape[0], nt), lambda i, k: (i, 0, k)),
        out_shape=jax.ShapeDtypeStruct((depth, cc.shape[0], n), F32),
        compiler_params=_params(2),
        name="modulation",
    )(cc, ada_w, ada_b.reshape(depth, 1, n))


def _mod_spec(nb, d, ncb):
    return pl.BlockSpec((None, 6, d), lambda b, j: (jnp.where(j < ncb, nb, b), 0, 0))


def _rw_proj_kernel(x_ref, xp_ref, xn_ref, mod_ref, ng_ref, mix_ref, wrkv_ref,
                    w1_ref, a1_ref, g1_ref, w2_ref, a2_ref, g2_ref, w0_ref, a0_ref,
                    kk_w_ref, ka_ref, rk_ref, bd_ref,
                    r_o, v_o, kk_o, g_o, bon_o, w0_o, k0_o, b0_o, w1_o, k1_o, b1_o,
                    *, ncb, nblk):
    j = pl.program_id(1)
    tb = x_ref.shape[0]
    shift = mod_ref[0:1, :]
    scale = mod_ref[1:2, :]
    g = ng_ref[...]
    h = _modnorm(x_ref[...], g, shift, scale)
    hp = _modnorm(xp_ref[7:8, :], g, shift, scale)
    hn = _modnorm(xn_ref[0:1, :], g, shift, scale)
    hp = jnp.where((j != 0) & (j != ncb), hp, 0.0)
    hn = jnp.where((j != ncb - 1) & (j != nblk - 1), hn, 0.0)
    row = lax.broadcasted_iota(jnp.int32, (tb, 1), 0)
    prev = jnp.where(row == 0, hp, pltpu.roll(h, 1, 0))
    nxt = jnp.where(row == tb - 1, hn, pltpu.roll(h, tb - 1, 0))
    xx = 0.5 * (prev + nxt) - h
    xr, xw, xk, xv, xa, xg = (h + xx * mix_ref[i:i + 1, :] for i in range(6))

    r = _dot(xr.astype(BF16), wrkv_ref[0])
    k = _dot(xk.astype(BF16), wrkv_ref[1])
    v = _dot(xv.astype(BF16), wrkv_ref[2])
    gate = _dot(_sigmoid(_dot(xg.astype(BF16), g1_ref[...])).astype(BF16), g2_ref[...])
    tw = jnp.tanh(_dot(xw.astype(BF16), w1_ref[...])).astype(BF16)
    ta = _dot(xa.astype(BF16), a1_ref[...]).astype(BF16)

    bd = bd_ref[...]
    kk = k * kk_w_ref[...]
    kk = kk / jnp.maximum(jnp.sqrt(_seg_sum(kk * kk, bd)), 1e-12)

    r_o[...] = r
    v_o[...] = v
    kk_o[...] = kk
    g_o[...] = gate
    ksum = None
    for d, (w_o, k_o, b_o) in enumerate(((w0_o, k0_o, b0_o), (w1_o, k1_o, b1_o))):
        pre = w0_ref[d:d + 1, :] + _dot(tw, w2_ref[d])
        w_o[...] = jnp.exp(-_sigmoid(pre) * float(np.exp(-0.5)))
        a = _sigmoid(a0_ref[d:d + 1, :] + _dot(ta, a2_ref[d]))
        kd = k * (1.0 + (a - 1.0) * ka_ref[...])
        k_o[...] = kd
        b_o[...] = kk * a
        ksum = kd if ksum is None else ksum + kd
    bon_o[...] = _seg_sum(r * ksum * rk_ref[...], bd) * v


def _rw_project(xs, mod, ng, w, ncb_tokens):
    nb, t, d = xs.shape
    tb = TB_PROJ
    nblk, ncb = t // tb, ncb_tokens // tb
    sl = tb // 8
    weights = [ng, w["mix"], w["w_rkv"], w["w1"], w["a1"], w["g1"], w["w2"], w["a2"], w["g2"],
               w["w0"], w["a0"], w["k_k"], w["k_a"], w["r_k"], w["bd"]]
    tok = pl.BlockSpec((None, tb, d), lambda b, j: (b, j, 0))
    return pl.pallas_call(
        functools.partial(_rw_proj_kernel, ncb=ncb, nblk=nblk),
        grid=(nb, nblk),
        in_specs=[tok,
                  pl.BlockSpec((None, 8, d), lambda b, j: (b, jnp.maximum(j * sl - 1, 0), 0)),
                  pl.BlockSpec((None, 8, d), lambda b, j: (b, jnp.minimum((j + 1) * sl, t // 8 - 1), 0)),
                  _mod_spec(nb, d, ncb)] + [_full(a) for a in weights],
        out_specs=[tok] * 11,
        out_shape=[jax.ShapeDtypeStruct((nb, t, d), F32)] * 11,
        compiler_params=_params(2),
        name="rwkv_project",
    )(xs, xs, xs, mod, *weights)


def _rw_scan_kernel(rf, wf, kf, vf, kkf, bf, rb, wb, kb, vb, kkb, bb, bd_ref,
                    yf_o, yb_o, s_ref):
    tb = rf.shape[0]
    nch = rf.shape[1] // LANES

    @pl.when(pl.program_id(1) == 0)
    def _():
        s_ref[...] = jnp.zeros_like(s_ref)

    bd = bd_ref[...]
    hw = RWKV_HEAD
    vi = lax.broadcasted_iota(jnp.int32, (hw, LANES), 0)
    li = lax.broadcasted_iota(jnp.int32, (hw, LANES), 1)
    diag = ((li & (hw - 1)) == vi).astype(F32)
    lane = lax.broadcasted_iota(jnp.int32, (1, LANES), 1)
    first_head = lane < hw
    head_mask = (first_head.astype(F32), 1.0 - first_head.astype(F32))
    sub = 16

    gch = 4
    groups = [(d, list(range(c0, c0 + gch))) for c0 in range(0, nch, gch) for d in (0, 1)]
    fwd = (rf, wf, kf, vf, kkf, bf)
    bwd = (rb, wb, kb, vb, kkb, bb)

    def rows_of(s, i):
        return s[i * hw:(i + 1) * hw]

    def body(g, carry):
        base = (pl.multiple_of(g * sub, sub), pl.multiple_of(tb - sub - g * sub, sub))
        lanes = [slice(c * LANES, (c + 1) * LANES) for c in range(nch)]
        tiles = [[[ref[pl.ds(base[d], sub), ls] for ref in refs] for ls in lanes]
                 for d, refs in enumerate((fwd, bwd))]

        def rows(d, c, n):
            m = n if d == 0 else sub - 1 - n
            return [t[m:m + 1] for t in tiles[d][c]]

        def start(d, cs, s, n):
            vcol = _dot(jnp.concatenate([rows(d, c, n)[3] * diag for c in cs], axis=0).astype(BF16), bd)
            sa = _dot(jnp.concatenate([s[i] * rows(d, c, n)[4] for i, c in enumerate(cs)], axis=0).astype(BF16), bd)
            return vcol, sa

        state = [[s_ref[d, c] for c in cs] for d, cs in groups]
        pend = [start(d, cs, state[gi], 0) for gi, (d, cs) in enumerate(groups)]
        ys = [[[] for _ in range(nch)] for _ in range(2)]
        for n in range(sub):
            for gi, (d, cs) in enumerate(groups):
                vcol, sa = pend[gi]
                s = []
                for i, c in enumerate(cs):
                    r, w, k, v, kk, b = rows(d, c, n)
                    s.append(state[gi][i] * w - rows_of(sa, i) * b + rows_of(vcol, i) * k)
                state[gi] = s
                if n + 1 < sub:
                    pend[gi] = start(d, cs, s, n + 1)
                lhs = jnp.concatenate([rows(d, c, n)[0] * head_mask[h] for c in cs for h in range(2)], axis=0)
                out = _dot_nt(lhs.astype(BF16), jnp.concatenate(s, axis=0).astype(BF16))
                for m in range(gch // 2):
                    x = out[:, m * LANES:(m + 1) * LANES]
                    xr = pltpu.roll(x, hw, 1)
                    r0 = 4 * m
                    ys[d][cs[2 * m]].append(jnp.where(first_head, x[r0:r0 + 1], xr[r0 + 1:r0 + 2]))
                    ys[d][cs[2 * m + 1]].append(jnp.where(first_head, xr[r0 + 2:r0 + 3], x[r0 + 3:r0 + 4]))
        for gi, (d, cs) in enumerate(groups):
            for i, c in enumerate(cs):
                s_ref[d, c] = state[gi][i]
        for c, ls in enumerate(lanes):
            yf_o[pl.ds(base[0], sub), ls] = jnp.concatenate(ys[0][c], axis=0)
            yb_o[pl.ds(base[1], sub), ls] = jnp.concatenate(ys[1][c][::-1], axis=0)
        return carry

    lax.fori_loop(0, tb // sub, body, 0)


def _bwd_block(j, ncb, nblk):
    return jnp.where(j < ncb, ncb - 1 - j, nblk - 1 - (j - ncb))


def _rw_scan(P, bd, ncb_tokens):
    r, v, kk, _, _, w0, k0, b0, w1, k1, b1 = P
    nb, t, d = r.shape
    tb = TB_SCAN
    nblk, ncb = t // tb, ncb_tokens // tb
    fs = pl.BlockSpec((None, tb, d), lambda b, j: (b, j, 0))
    bs = pl.BlockSpec((None, tb, d), lambda b, j: (b, _bwd_block(j, ncb, nblk), 0))
    return pl.pallas_call(
        _rw_scan_kernel,
        grid=(nb, nblk),
        in_specs=[fs] * 6 + [bs] * 6 + [_full(bd)],
        out_specs=[fs, bs],
        out_shape=[jax.ShapeDtypeStruct((nb, t, d), F32)] * 2,
        scratch_shapes=[pltpu.VMEM((2, d // LANES, RWKV_HEAD, LANES), F32)],
        compiler_params=_params(2),
        name="rwkv_scan",
    )(r, w0, k0, v, kk, b0, r, w1, k1, v, kk, b1, bd)


def _rw_out_kernel(x_ref, yf_ref, yb_ref, g_ref, bon_ref, mod_ref, lnw_ref, lnb_ref, wo_ref, bd_ref, o_ref):
    bd = bd_ref[...]
    y = yf_ref[...] + yb_ref[...]
    mu = _seg_sum(y, bd) * (1.0 / RWKV_HEAD)
    yc = y - mu
    var = _seg_sum(yc * yc, bd) * (1.0 / RWKV_HEAD)
    yn = yc * lax.rsqrt(var + RWKV_GN_EPS) * lnw_ref[...] + lnb_ref[...]
    z = ((yn + bon_ref[...]) * g_ref[...]).astype(BF16)
    o_ref[...] = x_ref[...] + mod_ref[2:3, :] * _dot(z, wo_ref[...])


def _rw_readout(xs, yf, yb, gate, bon, mod, w, ncb_tokens):
    nb, t, d = xs.shape
    tb = TB_SCAN
    tok = pl.BlockSpec((None, tb, d), lambda b, j: (b, j, 0))
    weights = [w["ln_w"], w["ln_b"], w["w_o"], w["bd"]]
    return pl.pallas_call(
        _rw_out_kernel,
        grid=(nb, t // tb),
        in_specs=[tok] * 5 + [_mod_spec(nb, d, ncb_tokens // tb)] + [_full(a) for a in weights],
        out_specs=tok,
        out_shape=jax.ShapeDtypeStruct((nb, t, d), F32),
        compiler_params=_params(2),
        name="rwkv_readout",
    )(xs, yf, yb, gate, bon, mod, *weights)


def _gla_proj_kernel(x_ref, mod_ref, ng_ref, wqkv_ref, wg_ref, wlr_ref, up_ref, gkb_ref,
                     z_o, g_o, la_o):
    h = _modnorm(x_ref[...], ng_ref[...], mod_ref[0:1, :], mod_ref[1:2, :])
    hb = h.astype(BF16)
    z_o[...] = _dot(hb, wqkv_ref[...])
    g_o[...] = _dot(hb, wg_ref[...])
    lr = _dot(hb, wlr_ref[...]).astype(BF16)
    la_o[...] = _log_sigmoid(_dot(lr, up_ref[...]) + gkb_ref[...]) * (1.0 / GLA_GATE_NORM)


def _gla_project(xs, mod, ng, w, ncb_tokens):
    nb, t, d = xs.shape
    tb = TB_SCAN
    weights = [ng, w["w_qkv"], w["w_g"], w["w_lr"], w["up"], w["gk_b"]]
    tok = pl.BlockSpec((None, tb, d), lambda b, j: (b, j, 0))
    tok2 = pl.BlockSpec((None, tb, 2 * d), lambda b, j: (b, j, 0))
    return pl.pallas_call(
        _gla_proj_kernel,
        grid=(nb, t // tb),
        in_specs=[tok, _mod_spec(nb, d, ncb_tokens // tb)] + [_full(a) for a in weights],
        out_specs=[tok2, tok, tok],
        out_shape=[jax.ShapeDtypeStruct((nb, t, 2 * d), F32),
                   jax.ShapeDtypeStruct((nb, t, d), F32),
                   jax.ShapeDtypeStruct((nb, t, d), F32)],
        compiler_params=_params(2),
        name="gla_project",
    )(xs, mod, *weights)


def _gla_conv_kernel(zc_ref, zp_ref, zn_ref, cw_ref, o_ref, *, nblk, qk_width, q_scale):
    j = pl.program_id(1)
    tb, ch = zc_ref.shape
    is_ctx = j == 0
    up_ok = jnp.where(j > 1, 1.0, 0.0)
    dn_ok = jnp.where((j > 0) & (j < nblk - 1), 1.0, 0.0)
    vert = jnp.where(is_ctx, 0.0, 1.0)
    ext = tb + 2 * GRID_W
    pos = lax.broadcasted_iota(jnp.int32, (ext, 1), 0) - GRID_W
    col = jnp.where(is_ctx, pos, pos & (GRID_W - 1))
    no_left = col == 0
    no_right = col == jnp.where(is_ctx, tb - 1, GRID_W - 1)
    cc = 256
    for c in range(ch // cc):
        cs = slice(c * cc, (c + 1) * cc)
        e = jnp.concatenate([zp_ref[:, cs] * up_ok, zc_ref[:, cs], zn_ref[:, cs] * dn_ok], axis=0)
        em = jnp.where(no_left, 0.0, pltpu.roll(e, 1, 0))
        ep = jnp.where(no_right, 0.0, pltpu.roll(e, ext - 1, 0))
        acc = None
        for dy in range(3):
            lo = dy * GRID_W
            for dx, src in enumerate((em, e, ep)):
                wt = cw_ref[dy * 3 + dx:dy * 3 + dx + 1, cs]
                if dy != 1:
                    wt = wt * vert
                term = src[lo:lo + tb] * wt
                acc = term if acc is None else acc + term
        y = _silu(acc)
        if (c + 1) * cc <= qk_width // 2:
            y = y * q_scale
        o_ref[:, cs] = y


def _gla_conv(z, cw, ncb_tokens, qk_width, q_scale):
    nb, t, ch = z.shape
    tb = TB_SCAN
    assert ncb_tokens == tb and tb % GRID_W == 0
    sl = tb // GRID_W
    nblk = t // tb
    return pl.pallas_call(
        functools.partial(_gla_conv_kernel, nblk=nblk, qk_width=qk_width, q_scale=q_scale),
        grid=(nb, nblk),
        in_specs=[pl.BlockSpec((None, tb, ch), lambda b, j: (b, j, 0)),
                  pl.BlockSpec((None, GRID_W, ch), lambda b, j: (b, jnp.maximum(j * sl - 1, 0), 0)),
                  pl.BlockSpec((None, GRID_W, ch),
                               lambda b, j: (b, jnp.minimum((j + 1) * sl, t // GRID_W - 1), 0)),
                  _full(cw)],
        out_specs=pl.BlockSpec((None, tb, ch), lambda b, j: (b, j, 0)),
        out_shape=jax.ShapeDtypeStruct((nb, t, ch), F32),
        compiler_params=_params(2),
        name="gla_conv",
    )(z, z, z, cw)


def _gla_scan_kernel(qf_ref, laf_ref, qb_ref, lab_ref, tri_ref, of_o, ob_o, st_ref, *, dk, dv):
    tb = qf_ref.shape[0]
    c = GLA_CHUNK
    nh = GLA_HEADS
    qkw = nh * dk

    @pl.when(pl.program_id(1) == 0)
    def _():
        st_ref[...] = jnp.zeros_like(st_ref)

    for d, (q_ref, la_ref, o_o) in enumerate(((qf_ref, laf_ref, of_o), (qb_ref, lab_ref, ob_o))):
        tri = tri_ref[d]
        keep = tri > 0
        ref_row = c // 2 if d == 0 else c - 1 - c // 2
        last_row = c - 1 if d == 0 else 0
        chunks = range(tb // c) if d == 0 else range(tb // c - 1, -1, -1)
        for n in chunks:
            rows = slice(n * c, (n + 1) * c)
            for hh in range(nh):
                q = q_ref[rows, hh * dk:(hh + 1) * dk]
                k = q_ref[rows, qkw + hh * dk:qkw + (hh + 1) * dk]
                v = q_ref[rows, 2 * qkw + hh * dv:2 * qkw + (hh + 1) * dv]
                la = la_ref[rows, d * qkw + hh * dk:d * qkw + (hh + 1) * dk]
                l1, l2, l3 = _split3(la)
                bcum = _dot(tri, l1) + _dot(tri, l2) + _dot(tri, l3)
                ref = bcum[ref_row:ref_row + 1]
                last = bcum[last_row:last_row + 1]
                a = _dot_nt((q * jnp.exp(bcum - ref)).astype(BF16), (k * jnp.exp(ref - bcum)).astype(BF16))
                a = jnp.where(keep, a, 0.0)
                vb = v.astype(BF16)
                st = st_ref[d, hh]
                o = _dot(a.astype(BF16), vb) + _dot_nt((q * jnp.exp(bcum)).astype(BF16), st.astype(BF16))
                o_o[rows, hh * dv:(hh + 1) * dv] = o
                kd = (k * jnp.exp(last - bcum)).astype(BF16)
                st_ref[d, hh] = st * jnp.exp(last) + _dot_tn(vb, kd)


def _gla_scan(qkv, la, tri, ncb_tokens, dk, dv):
    nb, t, ch = qkv.shape
    d = la.shape[-1]
    tb = TB_SCAN
    nblk, ncb = t // tb, ncb_tokens // tb
    f2 = pl.BlockSpec((None, tb, ch), lambda b, j: (b, j, 0))
    b2 = pl.BlockSpec((None, tb, ch), lambda b, j: (b, _bwd_block(j, ncb, nblk), 0))
    f1 = pl.BlockSpec((None, tb, d), lambda b, j: (b, j, 0))
    b1 = pl.BlockSpec((None, tb, d), lambda b, j: (b, _bwd_block(j, ncb, nblk), 0))
    return pl.pallas_call(
        functools.partial(_gla_scan_kernel, dk=dk, dv=dv),
        grid=(nb, nblk),
        in_specs=[f2, f1, b2, b1, _full(tri)],
        out_specs=[f1, b1],
        out_shape=[jax.ShapeDtypeStruct((nb, t, d), F32)] * 2,
        scratch_shapes=[pltpu.VMEM((2, GLA_HEADS, dv, dk), F32)],
        compiler_params=_params(2),
        name="gla_scan",
    )(qkv, la, qkv, la, tri)


def _gla_out_kernel(x_ref, of_ref, ob_ref, g_ref, mod_ref, hn_ref, wo_ref, o_ref, *, dv):
    o = of_ref[...] + ob_ref[...]
    parts = []
    for hh in range(GLA_HEADS):
        oh = o[:, hh * dv:(hh + 1) * dv]
        ms = jnp.mean(oh * oh, axis=-1, keepdims=True)
        parts.append(oh * lax.rsqrt(ms + NORM_EPS) * hn_ref[...])
    on = jnp.concatenate(parts, axis=-1)
    z = (on * _silu(g_ref[...])).astype(BF16)
    o_ref[...] = x_ref[...] + mod_ref[2:3, :] * _dot(z, wo_ref[...])


def _gla_readout(xs, of, ob, g, mod, w, ncb_tokens, dv):
    nb, t, d = xs.shape
    tb = TB_SCAN
    tok = pl.BlockSpec((None, tb, d), lambda b, j: (b, j, 0))
    weights = [w["head_norm"], w["w_o"]]
    return pl.pallas_call(
        functools.partial(_gla_out_kernel, dv=dv),
        grid=(nb, t // tb),
        in_specs=[tok] * 4 + [_mod_spec(nb, d, ncb_tokens // tb)] + [_full(a) for a in weights],
        out_specs=tok,
        out_shape=jax.ShapeDtypeStruct((nb, t, d), F32),
        compiler_params=_params(2),
        name="gla_readout",
    )(xs, of, ob, g, mod, *weights)


def _top_values(s, k):
    vals = []
    cur = s
    for _ in range(k):
        m = jnp.max(cur, axis=0, keepdims=True)
        vals.append(m)
        cur = jnp.where(cur == m, -jnp.inf, cur)
    return vals


def _sorting_network(n):
    pairs = []
    p = 1
    while p < n:
        k = p
        while k >= 1:
            for j in range(k % p, n - k, 2 * k):
                for i in range(min(k, n - j - k)):
                    if (i + j) // (2 * p) == (i + j + k) // (2 * p):
                        pairs.append((i + j, i + j + k))
            k //= 2
        p *= 2
    return pairs


def _top_values_tiled(s, k):
    sub = 8
    n = s.shape[0] // sub
    lists = [s[g * sub:(g + 1) * sub] for g in range(n)]
    for a, b in _sorting_network(n):
        hi, lo = jnp.maximum(lists[a], lists[b]), jnp.minimum(lists[a], lists[b])
        lists[a], lists[b] = hi, lo
    lists.append(jnp.full_like(lists[0], -jnp.inf))
    vals = []
    for it in range(k):
        m = jnp.max(lists[0], axis=0, keepdims=True)
        vals.append(m)
        took = lists[0] == m
        for r in range(min(n, k - it - 1)):
            lists[r] = jnp.where(took, lists[r + 1], lists[r])
    return vals


def _peer_route_kernel(x_ref, mod_ref, ng_ref, wq_ref, key_hi_ref, key_lo_ref,
                       ht_o, s0_o, s1_o, meta_o, q_hi_ref, q_lo_ref):
    h = _modnorm(x_ref[...], ng_ref[...], mod_ref[3:4, :], mod_ref[4:5, :])
    hb = h.T.astype(BF16)
    ht_o[...] = hb
    qt = _dot(wq_ref[...], hb)
    q_hi, q_lo = _split2(qt)
    q_hi_ref[...] = q_hi
    q_lo_ref[...] = q_lo
    nk = N_KEYS
    tb = x_ref.shape[0]
    zeros = jnp.zeros((4, tb), F32)

    def head(hd, carry):
        sv = []
        for p in range(2):
            hp = hd * 2 + p
            rows = pl.ds(pl.multiple_of(hp * nk, nk), nk)
            s = _dot3(key_hi_ref[hp], key_lo_ref[hp], q_hi_ref[rows, :], q_lo_ref[rows, :])
            (s0_o, s1_o)[p][hd] = s
            sv.append(_top_values_tiled(s, PEER_TOPK + 1))
        k, half = PEER_TOPK, PEER_TOPK // 2
        sv0 = jnp.concatenate(sv[0][:k], axis=0)
        sv1 = jnp.concatenate(sv[1][:k], axis=0)
        edge = jnp.concatenate([sv[0][k] + sv[1][0], sv[0][0] + sv[1][k],
                                jnp.full((6, tb), -jnp.inf, F32)], axis=0)
        cand = jnp.concatenate([sv[0][0] + sv1]
                               + [sv[0][a] + sv1[:half] for a in range(1, half)]
                               + [sv0[half:] + sv[1][0], edge], axis=0)
        top = _top_values(cand, k + 1)
        z = top[0] * 0.0
        for cval in top[:k]:
            z = z + jnp.exp(cval - top[0])
        tau = 0.5 * (top[k - 1] + top[k])
        meta_o[hd] = jnp.concatenate([tau, sv[0][0], sv[1][0], 1.0 / z, zeros], axis=0)
        return carry

    per_trip = 4

    def head_group(i, carry):
        for hh in range(per_trip):
            head(per_trip * i + hh, carry)
        return carry

    lax.fori_loop(0, PEER_HEADS // per_trip, head_group, 0)


def _peer_route(xs, mod, ng, w, ncb_tokens):
    nb, t, d = xs.shape
    tb = TB_ROUTE
    nq = w["wq"].shape[0]
    weights = [ng, w["wq"], w["key_hi"], w["key_lo"]]
    sspec = pl.BlockSpec((None, PEER_HEADS, N_KEYS, tb), lambda b, j: (b, 0, 0, j))
    return pl.pallas_call(
        _peer_route_kernel,
        grid=(nb, t // tb),
        in_specs=[pl.BlockSpec((None, tb, d), lambda b, j: (b, j, 0)),
                  _mod_spec(nb, d, ncb_tokens // tb)] + [_full(a) for a in weights],
        out_specs=[pl.BlockSpec((None, d, tb), lambda b, j: (b, 0, j)), sspec, sspec,
                   pl.BlockSpec((None, PEER_HEADS, 8, tb), lambda b, j: (b, 0, 0, j))],
        out_shape=[jax.ShapeDtypeStruct((nb, d, t), BF16),
                   jax.ShapeDtypeStruct((nb, PEER_HEADS, N_KEYS, t), F32),
                   jax.ShapeDtypeStruct((nb, PEER_HEADS, N_KEYS, t), F32),
                   jax.ShapeDtypeStruct((nb, PEER_HEADS, 8, t), F32)],
        scratch_shapes=[pltpu.VMEM((nq, tb), BF16), pltpu.VMEM((nq, tb), BF16)],
        compiler_params=_params(2),
        name="peer_route",
    )(xs, mod, *weights)


def _peer_dense_kernel(x_ref, ht_ref, s0_ref, s1_ref, meta_ref, modl_ref, modc_ref, u_ref, vt_ref, o_ref,
                       acc_ref, th_ref, e0_ref, e1_ref, act_ref, wa0_ref, wa1_ref,
                       *, nctx, nec, ntile, nsteps):
    g = pl.program_id(0)
    ne = u_ref.shape[0]
    d, tb = acc_ref.shape
    nk = N_KEYS
    nq = DENSE_QUARTERS
    qe, qd = ne // nq, d // nq
    qi = qe // nk
    g_b = jnp.clip(g - 1, 0, nsteps - 1)
    g_c = jnp.clip(g - 2, 0, nsteps - 1)
    e_b = g_b % nec
    e_c = g_c % nec
    tok0 = ((g_c // nec) % ntile) * tb

    @pl.when(g == 0)
    def _():
        for ref in (act_ref, wa0_ref, wa1_ref, th_ref, e0_ref):
            ref[...] = jnp.zeros_like(ref)

    @pl.when(e_b == 0)
    def _():
        for hd in range(PEER_HEADS):
            e1_ref[hd] = jnp.exp(s1_ref[hd] - meta_ref[hd, 2:3, :])
            s0 = s0_ref[hd]
            th = meta_ref[hd, 0:1, :] - s0
            e0 = jnp.exp(s0 - meta_ref[hd, 1:2, :]) * meta_ref[hd, 3:4, :]
            for p in range(nk // qi):
                th_ref[hd, p, 0:qi, :] = th[p * qi:(p + 1) * qi]
                e0_ref[hd, p, 0:qi, :] = e0[p * qi:(p + 1) * qi]

    @pl.when(e_c == 0)
    def _():
        acc_ref[...] = jnp.zeros_like(acc_ref)

    jr = 32

    new, old = g % 2, (g + 1) % 2

    def quarter(wa_new, wa_old, q, carry):
        r0 = pl.multiple_of(q * qe, qe)
        d0 = pl.multiple_of(q * qd, qd)
        pair = e_b * nq + q
        nj = nk // jr
        ka, kc = d // nj, ne // nj
        act_ref[new, pl.ds(r0, qe), :] = jnp.zeros((qe, tb), F32)
        for jq in range(nj):
            act_ref[new, pl.ds(r0, qe), :] += _dot(u_ref[pl.ds(r0, qe), jq * ka:(jq + 1) * ka],
                                                   ht_ref[jq * ka:(jq + 1) * ka, :])
            acc_ref[pl.ds(d0, qd), :] += _dot(vt_ref[pl.ds(d0, qd), jq * kc:(jq + 1) * kc],
                                              wa_old[jq * kc:(jq + 1) * kc, :])
            rows = slice(jq * jr, (jq + 1) * jr)
            accs = [None] * qi
            for hd in range(PEER_HEADS):
                s1 = s1_ref[hd, rows, :]
                e1 = e1_ref[hd, rows, :]
                th = th_ref[hd, pair]
                e0 = e0_ref[hd, pair]
                for ii in range(qi):
                    term = jnp.where(s1 >= th[ii:ii + 1], e1, 0.0) * e0[ii:ii + 1]
                    accs[ii] = term if accs[ii] is None else accs[ii] + term
            for ii in range(qi):
                er = pl.ds(pl.multiple_of(r0 + ii * nk + jq * jr, jr), jr)
                wa_new[er, :] = (accs[ii] * act_ref[old, er, :]).astype(BF16)

        a = act_ref[new, pl.ds(r0, qe), :]
        act_ref[new, pl.ds(r0, qe), :] = 0.5 * a * (1.0 + lax.erf(a * float(1.0 / np.sqrt(2.0))))
        return carry

    @pl.when(g % 2 == 0)
    def _():
        lax.fori_loop(0, nq, functools.partial(quarter, wa0_ref, wa1_ref), 0)

    @pl.when(g % 2 == 1)
    def _():
        lax.fori_loop(0, nq, functools.partial(quarter, wa1_ref, wa0_ref), 0)

    @pl.when((e_c == nec - 1) & (g >= 2))
    def _():
        tok = tok0 + lax.broadcasted_iota(jnp.int32, (tb, 1), 0)
        gate = jnp.where(tok < nctx, modc_ref[5:6, :], modl_ref[5:6, :])
        o_ref[...] = x_ref[...] + gate * acc_ref[...].T


def _peer_dense(xs, ht, s0, s1, meta, mod, w, ncb_tokens):
    nb, t, d = xs.shape
    tb = TB_DENSE
    ne = EC_DENSE
    n_exp = w["u"].shape[0]
    assert t % tb == 0 and n_exp % ne == 0 and ne % (DENSE_QUARTERS * N_KEYS) == 0
    ntile, nec = t // tb, n_exp // ne
    nsteps = nb * ntile * nec
    pairs = N_KEYS // (ne // DENSE_QUARTERS // N_KEYS)

    def at(lag):
        def f(g):
            s = jnp.clip(g - lag, 0, nsteps - 1)
            return s // (ntile * nec), (s // nec) % ntile, s % nec
        return f

    sa, sb, sc = at(0), at(1), at(2)
    sspec = pl.BlockSpec((None, PEER_HEADS, N_KEYS, tb), lambda g: (sb(g)[0], 0, 0, sb(g)[1]))
    return pl.pallas_call(
        functools.partial(_peer_dense_kernel, nctx=ncb_tokens, nec=nec, ntile=ntile, nsteps=nsteps),
        grid=(nsteps + 2,),
        in_specs=[pl.BlockSpec((None, tb, d), lambda g: (sc(g)[0], sc(g)[1], 0)),
                  pl.BlockSpec((None, d, tb), lambda g: (sa(g)[0], 0, sa(g)[1])),
                  sspec, sspec,
                  pl.BlockSpec((None, PEER_HEADS, 8, tb), lambda g: (sb(g)[0], 0, 0, sb(g)[1])),
                  pl.BlockSpec((None, 6, d), lambda g: (sc(g)[0], 0, 0)),
                  pl.BlockSpec((None, 6, d), lambda g: (nb, 0, 0)),
                  pl.BlockSpec((ne, d), lambda g: (sa(g)[2], 0)),
                  pl.BlockSpec((d, ne), lambda g: (0, sc(g)[2]))],
        out_specs=pl.BlockSpec((None, tb, d), lambda g: (sc(g)[0], sc(g)[1], 0)),
        out_shape=jax.ShapeDtypeStruct((nb, t, d), F32),
        scratch_shapes=[pltpu.VMEM((d, tb), F32),
                        pltpu.VMEM((PEER_HEADS, pairs, 8, tb), F32),
                        pltpu.VMEM((PEER_HEADS, pairs, 8, tb), F32),
                        pltpu.VMEM((PEER_HEADS, N_KEYS, tb), F32),
                        pltpu.VMEM((2, ne, tb), F32),
                        pltpu.VMEM((ne, tb), BF16), pltpu.VMEM((ne, tb), BF16)],
        compiler_params=_params(1),
        name="peer_dense",
    )(xs, ht, s0, s1, meta, mod, mod, w["u"], w["vt"])


def _final_norm_kernel(x_ref, g_ref, o_ref):
    x = x_ref[...]
    ms = jnp.mean(x * x, axis=-1, keepdims=True)
    o_ref[...] = x * lax.rsqrt(ms + NORM_EPS) * g_ref[...]


def _final_norm(xs, g, ncb_tokens):
    nb, t, d = xs.shape
    tb = TB_SCAN
    ncb = ncb_tokens // tb
    return pl.pallas_call(
        _final_norm_kernel,
        grid=(nb, t // tb - ncb),
        in_specs=[pl.BlockSpec((None, tb, d), lambda b, j: (b, j + ncb, 0)), _full(g)],
        out_specs=pl.BlockSpec((None, tb, d), lambda b, j: (b, j, 0)),
        out_shape=jax.ShapeDtypeStruct((nb, t - ncb_tokens, d), F32),
        compiler_params=_params(2),
        name="final_norm",
    )(xs, g)


def _row(a):
    return a.reshape(1, -1)


def _pad_dir(w2):
    z = jnp.zeros_like(w2[0])
    return jnp.stack([jnp.concatenate([w2[0], z], axis=0), jnp.concatenate([z, w2[1]], axis=0)]).astype(BF16)


def kernel(x, c, ctx, c_ctx, ada_w, ada_b, norm_mix, norm_ffn, rw_mix, rw_w_rkv, rw_w0, rw_w1, rw_w2, rw_a0, rw_a1, rw_a2, rw_g1, rw_g2, rw_k_k, rw_k_a, rw_r_k, rw_ln_w, rw_ln_b, rw_w_o, gla_w_in, gla_conv, gla_gk_up, gla_gk_b, gla_head_norm, gla_w_o, peer_wq, peer_keys, peer_u, peer_v, final_norm):
    nb, seq, d = x.shape
    nctx = ctx.shape[1]
    depth = ada_w.shape[0]
    assert nctx == TB_SCAN and seq % TB_SCAN == 0 and d % LANES == 0

    xs = jnp.concatenate([ctx, x], axis=1)
    cc = jnp.zeros((16, d), F32).at[:nb].set(c).at[nb].set(c_ctx)
    mods = _modulation(cc, ada_w, ada_b)[:, :nb + 1].reshape(depth, nb + 1, 6, d)

    bd_head = _block_diag_ones(RWKV_HEAD)
    ci = np.arange(GLA_CHUNK)
    tri = jnp.asarray(np.stack([ci[None, :] <= ci[:, None], ci[None, :] >= ci[:, None]]), dtype=BF16)

    for i in range(depth):
        mod = mods[i]
        j = i // 2
        if i % 2 == 0:
            w = dict(
                mix=rw_mix[j], w_rkv=rw_w_rkv[j].astype(BF16),
                w1=jnp.concatenate([rw_w1[j, 0], rw_w1[j, 1]], axis=1).astype(BF16),
                a1=jnp.concatenate([rw_a1[j, 0], rw_a1[j, 1]], axis=1).astype(BF16),
                g1=rw_g1[j].astype(BF16), w2=_pad_dir(rw_w2[j]), a2=_pad_dir(rw_a2[j]),
                g2=rw_g2[j].astype(BF16), w0=rw_w0[j], a0=rw_a0[j],
                k_k=_row(rw_k_k[j]), k_a=_row(rw_k_a[j]), r_k=_row(rw_r_k[j]),
                ln_w=_row(rw_ln_w[j]), ln_b=_row(rw_ln_b[j]), w_o=rw_w_o[j].astype(BF16), bd=bd_head)
            P = _rw_project(xs, mod, _row(norm_mix[i]), w, nctx)
            yf, yb = _rw_scan(P, bd_head, nctx)
            xs = _rw_readout(xs, yf, yb, P[3], P[4], mod, w, nctx)
        else:
            w_in = gla_w_in[j]
            qk_width = gla_gk_up.shape[-1] * 2
            conv_ch = qk_width + d
            dk = gla_gk_up.shape[-1] // GLA_HEADS
            dv = d // GLA_HEADS
            rank = gla_gk_up.shape[2]
            lr0 = conv_ch + d
            w_lr = jnp.zeros((d, LANES), F32).at[:, :2 * rank].set(w_in[:, lr0:lr0 + 2 * rank])
            up = jnp.zeros((LANES, d), F32)
            up = up.at[:rank, :qk_width // 2].set(gla_gk_up[j, 0]).at[rank:2 * rank, qk_width // 2:].set(gla_gk_up[j, 1])
            w = dict(w_qkv=w_in[:, :conv_ch].astype(BF16), w_g=w_in[:, conv_ch:lr0].astype(BF16),
                     w_lr=w_lr.astype(BF16), up=up.astype(BF16),
                     gk_b=jnp.concatenate([gla_gk_b[j, 0], gla_gk_b[j, 1]]).reshape(1, d),
                     head_norm=_row(gla_head_norm[j]), w_o=gla_w_o[j].astype(BF16))
            z, g, la = _gla_project(xs, mod, _row(norm_mix[i]), w, nctx)
            qkv = _gla_conv(z, gla_conv[j].reshape(9, conv_ch), nctx, qk_width, float(dk) ** -0.5)
            of, ob = _gla_scan(qkv, la, tri, nctx, dk, dv)
            xs = _gla_readout(xs, of, ob, g, mod, w, nctx, dv)

        keys = peer_keys[i].reshape(PEER_HEADS * 2, N_KEYS, -1)
        key_hi = keys.astype(BF16)
        pw = dict(wq=peer_wq[i].T.astype(BF16),
                  key_hi=key_hi, key_lo=(keys - key_hi.astype(F32)).astype(BF16),
                  u=peer_u[i].astype(BF16), vt=peer_v[i].T.astype(BF16))
        ht, s0, s1, meta = _peer_route(xs, mod, _row(norm_ffn[i]), pw, nctx)
        xs = _peer_dense(xs, ht, s0, s1, meta, mod, pw, nctx)

    return _final_norm(xs, _row(final_norm), nctx)
```

```python
import functools

import numpy as np
import jax
import jax.numpy as jnp
from jax import lax
from jax.experimental import pallas as pl
from jax.experimental.pallas import tpu as pltpu

F32 = jnp.float32
BF16 = jnp.bfloat16

NORM_EPS = 1e-6
GRID_W = 64
RWKV_HEAD = 64
RWKV_GN_EPS = 64e-5
DECAY_LORA = 64
GLA_HEADS = 4
GLA_GATE_RANK = 16
GLA_GATE_NORM = 16.0
GLA_CHUNK = 64
PEER_HEADS = 8
N_KEYS = 128
PEER_TOPK = 16

LANES = 128
VMEM_LIMIT = 56 * 1024 * 1024

TB_PROJ = 128
TB_SCAN = 256
TB_ROUTE = 256
TB_DENSE = 256
EC_DENSE = 1024
DENSE_QUARTERS = 2


def _params(n_axes):
    return pltpu.CompilerParams(
        dimension_semantics=("arbitrary",) * n_axes,
        vmem_limit_bytes=VMEM_LIMIT)


def _full(a):
    nd = a.ndim
    return pl.BlockSpec(a.shape, lambda *_: (0,) * nd)


def _sigmoid(x):
    return 1.0 / (1.0 + jnp.exp(-x))


def _silu(x):
    return x * _sigmoid(x)


def _log_sigmoid(x):
    return jnp.minimum(x, 0.0) - jnp.log(1.0 + jnp.exp(-jnp.abs(x)))


def _modnorm(x, g, shift, scale):
    ms = jnp.mean(x * x, axis=-1, keepdims=True)
    return x * lax.rsqrt(ms + NORM_EPS) * (g * (1.0 + scale)) + shift


def _split2(x):
    hi = x.astype(BF16)
    lo = (x - hi.astype(F32)).astype(BF16)
    return hi, lo


def _split3(x):
    x1 = x.astype(BF16)
    r1 = x - x1.astype(F32)
    x2 = r1.astype(BF16)
    x3 = (r1 - x2.astype(F32)).astype(BF16)
    return x1, x2, x3


def _dot(a, b):
    return jnp.dot(a, b, preferred_element_type=F32)


def _dot_nt(a, b):
    return lax.dot_general(a, b, (((1,), (1,)), ((), ())), preferred_element_type=F32)


def _dot_tn(a, b):
    return lax.dot_general(a, b, (((0,), (0,)), ((), ())), preferred_element_type=F32)


def _dot3(a_hi, a_lo, b_hi, b_lo):
    return _dot(a_hi, b_hi) + _dot(a_hi, b_lo) + _dot(a_lo, b_hi)


def _seg_sum(x, bd):
    outs = []
    for c in range(x.shape[-1] // LANES):
        hi, lo = _split2(x[:, c * LANES:(c + 1) * LANES])
        outs.append(_dot(hi, bd) + _dot(lo, bd))
    return jnp.concatenate(outs, axis=-1)


def _block_diag_ones(group):
    i = np.arange(LANES)
    return jnp.asarray((i[:, None] // group) == (i[None, :] // group), dtype=BF16)


def _mod_kernel(c_ref, w_ref, b_ref, o_ref):
    a = _silu(c_ref[...])
    o_ref[...] = jnp.dot(a, w_ref[...], preferred_element_type=F32,
                         precision=lax.Precision.HIGHEST) + b_ref[...]


def _modulation(cc, ada_w, ada_b):
    depth, d, n = ada_w.shape
    nt = 768
    return pl.pallas_call(
        _mod_kernel,
        grid=(depth, n // nt),
        in_specs=[pl.BlockSpec(cc.shape, lambda i, k: (0, 0)),
                  pl.BlockSpec((None, d, nt), lambda i, k: (i, 0, k)),
                  pl.BlockSpec((None, 1, nt), lambda i, k: (i, 0, k))],
        out_specs=pl.BlockSpec((None, cc.shape[0], nt), lambda i, k: (i, 0, k)),
        out_shape=jax.ShapeDtypeStruct((depth, cc.shape[0], n), F32),
        compiler_params=_params(2),
        name="modulation",
    )(cc, ada_w, ada_b.reshape(depth, 1, n))


def _mod_spec(nb, d, ncb):
    return pl.BlockSpec((None, 6, d), lambda b, j: (jnp.where(j < ncb, nb, b), 0, 0))


def _rw_proj_kernel(x_ref, xp_ref, xn_ref, mod_ref, ng_ref, mix_ref, wrkv_ref,
                    w1_ref, a1_ref, g1_ref, w2_ref, a2_ref, g2_ref, w0_ref, a0_ref,
                    kk_w_ref, ka_ref, rk_ref, bd_ref,
                    r_o, v_o, kk_o, g_o, bon_o, w0_o, k0_o, b0_o, w1_o, k1_o, b1_o,
                    *, ncb, nblk):
    j = pl.program_id(1)
    tb = x_ref.shape[0]
    shift = mod_ref[0:1, :]
    scale = mod_ref[1:2, :]
    g = ng_ref[...]
    h = _modnorm(x_ref[...], g, shift, scale)
    hp = _modnorm(xp_ref[7:8, :], g, shift, scale)
    hn = _modnorm(xn_ref[0:1, :], g, shift, scale)
    hp = jnp.where((j != 0) & (j != ncb), hp, 0.0)
    hn = jnp.where((j != ncb - 1) & (j != nblk - 1), hn, 0.0)
    row = lax.broadcasted_iota(jnp.int32, (tb, 1), 0)
    prev = jnp.where(row == 0, hp, pltpu.roll(h, 1, 0))
    nxt = jnp.where(row == tb - 1, hn, pltpu.roll(h, tb - 1, 0))
    xx = 0.5 * (prev + nxt) - h
    xr, xw, xk, xv, xa, xg = (h + xx * mix_ref[i:i + 1, :] for i in range(6))

    r = _dot(xr.astype(BF16), wrkv_ref[0])
    k = _dot(xk.astype(BF16), wrkv_ref[1])
    v = _dot(xv.astype(BF16), wrkv_ref[2])
    gate = _dot(_sigmoid(_dot(xg.astype(BF16), g1_ref[...])).astype(BF16), g2_ref[...])
    tw = jnp.tanh(_dot(xw.astype(BF16), w1_ref[...])).astype(BF16)
    ta = _dot(xa.astype(BF16), a1_ref[...]).astype(BF16)

    bd = bd_ref[...]
    kk = k * kk_w_ref[...]
    kk = kk / jnp.maximum(jnp.sqrt(_seg_sum(kk * kk, bd)), 1e-12)

    r_o[...] = r
    v_o[...] = v
    kk_o[...] = kk
    g_o[...] = gate
    ksum = None
    for d, (w_o, k_o, b_o) in enumerate(((w0_o, k0_o, b0_o), (w1_o, k1_o, b1_o))):
        pre = w0_ref[d:d + 1, :] + _dot(tw, w2_ref[d])
        w_o[...] = jnp.exp(-_sigmoid(pre) * float(np.exp(-0.5)))
        a = _sigmoid(a0_ref[d:d + 1, :] + _dot(ta, a2_ref[d]))
        kd = k * (1.0 + (a - 1.0) * ka_ref[...])
        k_o[...] = kd
        b_o[...] = kk * a
        ksum = kd if ksum is None else ksum + kd
    bon_o[...] = _seg_sum(r * ksum * rk_ref[...], bd) * v


def _rw_project(xs, mod, ng, w, ncb_tokens):
    nb, t, d = xs.shape
    tb = TB_PROJ
    nblk, ncb = t // tb, ncb_tokens // tb
    sl = tb // 8
    weights = [ng, w["mix"], w["w_rkv"], w["w1"], w["a1"], w["g1"], w["w2"], w["a2"], w["g2"],
               w["w0"], w["a0"], w["k_k"], w["k_a"], w["r_k"], w["bd"]]
    tok = pl.BlockSpec((None, tb, d), lambda b, j: (b, j, 0))
    return pl.pallas_call(
        functools.partial(_rw_proj_kernel, ncb=ncb, nblk=nblk),
        grid=(nb, nblk),
        in_specs=[tok,
                  pl.BlockSpec((None, 8, d), lambda b, j: (b, jnp.maximum(j * sl - 1, 0), 0)),
                  pl.BlockSpec((None, 8, d), lambda b, j: (b, jnp.minimum((j + 1) * sl, t // 8 - 1), 0)),
                  _mod_spec(nb, d, ncb)] + [_full(a) for a in weights],
        out_specs=[tok] * 11,
        out_shape=[jax.ShapeDtypeStruct((nb, t, d), F32)] * 11,
        compiler_params=_params(2),
        name="rwkv_project",
    )(xs, xs, xs, mod, *weights)


def _rw_scan_kernel(rf, wf, kf, vf, kkf, bf, rb, wb, kb, vb, kkb, bb, bd_ref,
                    yf_o, yb_o, s_ref):
    tb = rf.shape[0]
    nch = rf.shape[1] // LANES

    @pl.when(pl.program_id(1) == 0)
    def _():
        s_ref[...] = jnp.zeros_like(s_ref)

    bd = bd_ref[...]
    hw = RWKV_HEAD
    vi = lax.broadcasted_iota(jnp.int32, (hw, LANES), 0)
    li = lax.broadcasted_iota(jnp.int32, (hw, LANES), 1)
    diag = ((li & (hw - 1)) == vi).astype(F32)
    lane = lax.broadcasted_iota(jnp.int32, (1, LANES), 1)
    first_head = lane < hw
    head_mask = (first_head.astype(F32), 1.0 - first_head.astype(F32))
    sub = 16

    gch = 4
    groups = [(d, list(range(c0, c0 + gch))) for c0 in range(0, nch, gch) for d in (0, 1)]
    fwd = (rf, wf, kf, vf, kkf, bf)
    bwd = (rb, wb, kb, vb, kkb, bb)

    def rows_of(s, i):
        return s[i * hw:(i + 1) * hw]

    def body(g, carry):
        base = (pl.multiple_of(g * sub, sub), pl.multiple_of(tb - sub - g * sub, sub))
        lanes = [slice(c * LANES, (c + 1) * LANES) for c in range(nch)]
        tiles = [[[ref[pl.ds(base[d], sub), ls] for ref in refs] for ls in lanes]
                 for d, refs in enumerate((fwd, bwd))]

        def rows(d, c, n):
            m = n if d == 0 else sub - 1 - n
            return [t[m:m + 1] for t in tiles[d][c]]

        def start(d, cs, s, n):
            vcol = _dot(jnp.concatenate([rows(d, c, n)[3] * diag for c in cs], axis=0).astype(BF16), bd)
            sa = _dot(jnp.concatenate([s[i] * rows(d, c, n)[4] for i, c in enumerate(cs)], axis=0).astype(BF16), bd)
            return vcol, sa

        state = [[s_ref[d, c] for c in cs] for d, cs in groups]
        pend = [start(d, cs, state[gi], 0) for gi, (d, cs) in enumerate(groups)]
        ys = [[[] for _ in range(nch)] for _ in range(2)]
        for n in range(sub):
            for gi, (d, cs) in enumerate(groups):
                vcol, sa = pend[gi]
                s = []
                for i, c in enumerate(cs):
                    r, w, k, v, kk, b = rows(d, c, n)
                    s.append(state[gi][i] * w - rows_of(sa, i) * b + rows_of(vcol, i) * k)
                state[gi] = s
                if n + 1 < sub:
                    pend[gi] = start(d, cs, s, n + 1)
                lhs = jnp.concatenate([rows(d, c, n)[0] * head_mask[h] for c in cs for h in range(2)], axis=0)
                out = _dot_nt(lhs.astype(BF16), jnp.concatenate(s, axis=0).astype(BF16))
                for m in range(gch // 2):
                    x = out[:, m * LANES:(m + 1) * LANES]
                    xr = pltpu.roll(x, hw, 1)
                    r0 = 4 * m
                    ys[d][cs[2 * m]].append(jnp.where(first_head, x[r0:r0 + 1], xr[r0 + 1:r0 + 2]))
                    ys[d][cs[2 * m + 1]].append(jnp.where(first_head, xr[r0 + 2:r0 + 3], x[r0 + 3:r0 + 4]))
        for gi, (d, cs) in enumerate(groups):
            for i, c in enumerate(cs):
                s_ref[d, c] = state[gi][i]
        for c, ls in enumerate(lanes):
            yf_o[pl.ds(base[0], sub), ls] = jnp.concatenate(ys[0][c], axis=0)
            yb_o[pl.ds(base[1], sub), ls] = jnp.concatenate(ys[1][c][::-1], axis=0)
        return carry

    lax.fori_loop(0, tb // sub, body, 0)


def _bwd_block(j, ncb, nblk):
    return jnp.where(j < ncb, ncb - 1 - j, nblk - 1 - (j - ncb))


def _rw_scan(P, bd, ncb_tokens):
    r, v, kk, _, _, w0, k0, b0, w1, k1, b1 = P
    nb, t, d = r.shape
    tb = TB_SCAN
    nblk, ncb = t // tb, ncb_tokens // tb
    fs = pl.BlockSpec((None, tb, d), lambda b, j: (b, j, 0))
    bs = pl.BlockSpec((None, tb, d), lambda b, j: (b, _bwd_block(j, ncb, nblk), 0))
    return pl.pallas_call(
        _rw_scan_kernel,
        grid=(nb, nblk),
        in_specs=[fs] * 6 + [bs] * 6 + [_full(bd)],
        out_specs=[fs, bs],
        out_shape=[jax.ShapeDtypeStruct((nb, t, d), F32)] * 2,
        scratch_shapes=[pltpu.VMEM((2, d // LANES, RWKV_HEAD, LANES), F32)],
        compiler_params=_params(2),
        name="rwkv_scan",
    )(r, w0, k0, v, kk, b0, r, w1, k1, v, kk, b1, bd)


def _rw_out_kernel(x_ref, yf_ref, yb_ref, g_ref, bon_ref, mod_ref, lnw_ref, lnb_ref, wo_ref, bd_ref, o_ref):
    bd = bd_ref[...]
    y = yf_ref[...] + yb_ref[...]
    mu = _seg_sum(y, bd) * (1.0 / RWKV_HEAD)
    yc = y - mu
    var = _seg_sum(yc * yc, bd) * (1.0 / RWKV_HEAD)
    yn = yc * lax.rsqrt(var + RWKV_GN_EPS) * lnw_ref[...] + lnb_ref[...]
    z = ((yn + bon_ref[...]) * g_ref[...]).astype(BF16)
    o_ref[...] = x_ref[...] + mod_ref[2:3, :] * _dot(z, wo_ref[...])


def _rw_readout(xs, yf, yb, gate, bon, mod, w, ncb_tokens):
    nb, t, d = xs.shape
    tb = TB_SCAN
    tok = pl.BlockSpec((None, tb, d), lambda b, j: (b, j, 0))
    weights = [w["ln_w"], w["ln_b"], w["w_o"], w["bd"]]
    return pl.pallas_call(
        _rw_out_kernel,
        grid=(nb, t // tb),
        in_specs=[tok] * 5 + [_mod_spec(nb, d, ncb_tokens // tb)] + [_full(a) for a in weights],
        out_specs=tok,
        out_shape=jax.ShapeDtypeStruct((nb, t, d), F32),
        compiler_params=_params(2),
        name="rwkv_readout",
    )(xs, yf, yb, gate, bon, mod, *weights)


def _gla_proj_kernel(x_ref, mod_ref, ng_ref, wqkv_ref, wg_ref, wlr_ref, up_ref, gkb_ref,
                     z_o, g_o, la_o):
    h = _modnorm(x_ref[...], ng_ref[...], mod_ref[0:1, :], mod_ref[1:2, :])
    hb = h.astype(BF16)
    z_o[...] = _dot(hb, wqkv_ref[...])
    g_o[...] = _dot(hb, wg_ref[...])
    lr = _dot(hb, wlr_ref[...]).astype(BF16)
    la_o[...] = _log_sigmoid(_dot(lr, up_ref[...]) + gkb_ref[...]) * (1.0 / GLA_GATE_NORM)


def _gla_project(xs, mod, ng, w, ncb_tokens):
    nb, t, d = xs.shape
    tb = TB_SCAN
    weights = [ng, w["w_qkv"], w["w_g"], w["w_lr"], w["up"], w["gk_b"]]
    tok = pl.BlockSpec((None, tb, d), lambda b, j: (b, j, 0))
    tok2 = pl.BlockSpec((None, tb, 2 * d), lambda b, j: (b, j, 0))
    return pl.pallas_call(
        _gla_proj_kernel,
        grid=(nb, t // tb),
        in_specs=[tok, _mod_spec(nb, d, ncb_tokens // tb)] + [_full(a) for a in weights],
        out_specs=[tok2, tok, tok],
        out_shape=[jax.ShapeDtypeStruct((nb, t, 2 * d), F32),
                   jax.ShapeDtypeStruct((nb, t, d), F32),
                   jax.ShapeDtypeStruct((nb, t, d), F32)],
        compiler_params=_params(2),
        name="gla_project",
    )(xs, mod, *weights)


def _gla_conv_kernel(zc_ref, zp_ref, zn_ref, cw_ref, o_ref, *, nblk, qk_width, q_scale):
    j = pl.program_id(1)
    tb, ch = zc_ref.shape
    is_ctx = j == 0
    up_ok = jnp.where(j > 1, 1.0, 0.0)
    dn_ok = jnp.where((j > 0) & (j < nblk - 1), 1.0, 0.0)
    vert = jnp.where(is_ctx, 0.0, 1.0)
    ext = tb + 2 * GRID_W
    pos = lax.broadcasted_iota(jnp.int32, (ext, 1), 0) - GRID_W
    col = jnp.where(is_ctx, pos, pos & (GRID_W - 1))
    no_left = col == 0
    no_right = col == jnp.where(is_ctx, tb - 1, GRID_W - 1)
    cc = 256
    for c in range(ch // cc):
        cs = slice(c * cc, (c + 1) * cc)
        e = jnp.concatenate([zp_ref[:, cs] * up_ok, zc_ref[:, cs], zn_ref[:, cs] * dn_ok], axis=0)
        em = jnp.where(no_left, 0.0, pltpu.roll(e, 1, 0))
        ep = jnp.where(no_right, 0.0, pltpu.roll(e, ext - 1, 0))
        acc = None
        for dy in range(3):
            lo = dy * GRID_W
            for dx, src in enumerate((em, e, ep)):
                wt = cw_ref[dy * 3 + dx:dy * 3 + dx + 1, cs]
                if dy != 1:
                    wt = wt * vert
                term = src[lo:lo + tb] * wt
                acc = term if acc is None else acc + term
        y = _silu(acc)
        if (c + 1) * cc <= qk_width // 2:
            y = y * q_scale
        o_ref[:, cs] = y


def _gla_conv(z, cw, ncb_tokens, qk_width, q_scale):
    nb, t, ch = z.shape
    tb = TB_SCAN
    assert ncb_tokens == tb and tb % GRID_W == 0
    sl = tb // GRID_W
    nblk = t // tb
    return pl.pallas_call(
        functools.partial(_gla_conv_kernel, nblk=nblk, qk_width=qk_width, q_scale=q_scale),
        grid=(nb, nblk),
        in_specs=[pl.BlockSpec((None, tb, ch), lambda b, j: (b, j, 0)),
                  pl.BlockSpec((None, GRID_W, ch), lambda b, j: (b, jnp.maximum(j * sl - 1, 0), 0)),
                  pl.BlockSpec((None, GRID_W, ch),
                               lambda b, j: (b, jnp.minimum((j + 1) * sl, t // GRID_W - 1), 0)),
                  _full(cw)],
        out_specs=pl.BlockSpec((None, tb, ch), lambda b, j: (b, j, 0)),
        out_shape=jax.ShapeDtypeStruct((nb, t, ch), F32),
        compiler_params=_params(2),
        name="gla_conv",
    )(z, z, z, cw)


def _gla_scan_kernel(qf_ref, laf_ref, qb_ref, lab_ref, tri_ref, of_o, ob_o, st_ref, *, dk, dv):
    tb = qf_ref.shape[0]
    c = GLA_CHUNK
    nh = GLA_HEADS
    qkw = nh * dk

    @pl.when(pl.program_id(1) == 0)
    def _():
        st_ref[...] = jnp.zeros_like(st_ref)

    for d, (q_ref, la_ref, o_o) in enumerate(((qf_ref, laf_ref, of_o), (qb_ref, lab_ref, ob_o))):
        tri = tri_ref[d]
        keep = tri > 0
        ref_row = c // 2 if d == 0 else c - 1 - c // 2
        last_row = c - 1 if d == 0 else 0
        chunks = range(tb // c) if d == 0 else range(tb // c - 1, -1, -1)
        for n in chunks:
            rows = slice(n * c, (n + 1) * c)
            for hh in range(nh):
                q = q_ref[rows, hh * dk:(hh + 1) * dk]
                k = q_ref[rows, qkw + hh * dk:qkw + (hh + 1) * dk]
                v = q_ref[rows, 2 * qkw + hh * dv:2 * qkw + (hh + 1) * dv]
                la = la_ref[rows, d * qkw + hh * dk:d * qkw + (hh + 1) * dk]
                l1, l2, l3 = _split3(la)
                bcum = _dot(tri, l1) + _dot(tri, l2) + _dot(tri, l3)
                ref = bcum[ref_row:ref_row + 1]
                last = bcum[last_row:last_row + 1]
                a = _dot_nt((q * jnp.exp(bcum - ref)).astype(BF16), (k * jnp.exp(ref - bcum)).astype(BF16))
                a = jnp.where(keep, a, 0.0)
                vb = v.astype(BF16)
                st = st_ref[d, hh]
                o = _dot(a.astype(BF16), vb) + _dot_nt((q * jnp.exp(bcum)).astype(BF16), st.astype(BF16))
                o_o[rows, hh * dv:(hh + 1) * dv] = o
                kd = (k * jnp.exp(last - bcum)).astype(BF16)
                st_ref[d, hh] = st * jnp.exp(last) + _dot_tn(vb, kd)


def _gla_scan(qkv, la, tri, ncb_tokens, dk, dv):
    nb, t, ch = qkv.shape
    d = la.shape[-1]
    tb = TB_SCAN
    nblk, ncb = t // tb, ncb_tokens // tb
    f2 = pl.BlockSpec((None, tb, ch), lambda b, j: (b, j, 0))
    b2 = pl.BlockSpec((None, tb, ch), lambda b, j: (b, _bwd_block(j, ncb, nblk), 0))
    f1 = pl.BlockSpec((None, tb, d), lambda b, j: (b, j, 0))
    b1 = pl.BlockSpec((None, tb, d), lambda b, j: (b, _bwd_block(j, ncb, nblk), 0))
    return pl.pallas_call(
        functools.partial(_gla_scan_kernel, dk=dk, dv=dv),
        grid=(nb, nblk),
        in_specs=[f2, f1, b2, b1, _full(tri)],
        out_specs=[f1, b1],
        out_shape=[jax.ShapeDtypeStruct((nb, t, d), F32)] * 2,
        scratch_shapes=[pltpu.VMEM((2, GLA_HEADS, dv, dk), F32)],
        compiler_params=_params(2),
        name="gla_scan",
    )(qkv, la, qkv, la, tri)


def _gla_out_kernel(x_ref, of_ref, ob_ref, g_ref, mod_ref, hn_ref, wo_ref, o_ref, *, dv):
    o = of_ref[...] + ob_ref[...]
    parts = []
    for hh in range(GLA_HEADS):
        oh = o[:, hh * dv:(hh + 1) * dv]
        ms = jnp.mean(oh * oh, axis=-1, keepdims=True)
        parts.append(oh * lax.rsqrt(ms + NORM_EPS) * hn_ref[...])
    on = jnp.concatenate(parts, axis=-1)
    z = (on * _silu(g_ref[...])).astype(BF16)
    o_ref[...] = x_ref[...] + mod_ref[2:3, :] * _dot(z, wo_ref[...])


def _gla_readout(xs, of, ob, g, mod, w, ncb_tokens, dv):
    nb, t, d = xs.shape
    tb = TB_SCAN
    tok = pl.BlockSpec((None, tb, d), lambda b, j: (b, j, 0))
    weights = [w["head_norm"], w["w_o"]]
    return pl.pallas_call(
        functools.partial(_gla_out_kernel, dv=dv),
        grid=(nb, t // tb),
        in_specs=[tok] * 4 + [_mod_spec(nb, d, ncb_tokens // tb)] + [_full(a) for a in weights],
        out_specs=tok,
        out_shape=jax.ShapeDtypeStruct((nb, t, d), F32),
        compiler_params=_params(2),
        name="gla_readout",
    )(xs, of, ob, g, mod, *weights)


def _top_values(s, k):
    vals = []
    cur = s
    for _ in range(k):
        m = jnp.max(cur, axis=0, keepdims=True)
        vals.append(m)
        cur = jnp.where(cur == m, -jnp.inf, cur)
    return vals


def _sorting_network(n):
    pairs = []
    p = 1
    while p < n:
        k = p
        while k >= 1:
            for j in range(k % p, n - k, 2 * k):
                for i in range(min(k, n - j - k)):
                    if (i + j) // (2 * p) == (i + j + k) // (2 * p):
                        pairs.append((i + j, i + j + k))
            k //= 2
        p *= 2
    return pairs


def _top_values_tiled(s, k):
    sub = 8
    n = s.shape[0] // sub
    lists = [s[g * sub:(g + 1) * sub] for g in range(n)]
    for a, b in _sorting_network(n):
        hi, lo = jnp.maximum(lists[a], lists[b]), jnp.minimum(lists[a], lists[b])
        lists[a], lists[b] = hi, lo
    lists.append(jnp.full_like(lists[0], -jnp.inf))
    vals = []
    for it in range(k):
        m = jnp.max(lists[0], axis=0, keepdims=True)
        vals.append(m)
        took = lists[0] == m
        for r in range(min(n, k - it - 1)):
            lists[r] = jnp.where(took, lists[r + 1], lists[r])
    return vals


def _peer_route_kernel(x_ref, mod_ref, ng_ref, wq_ref, key_hi_ref, key_lo_ref,
                       ht_o, c0_o, e0_o, r1_o, e1_o, q_hi_ref, q_lo_ref):
    h = _modnorm(x_ref[...], ng_ref[...], mod_ref[3:4, :], mod_ref[4:5, :])
    hb = h.T.astype(BF16)
    ht_o[...] = hb
    qt = _dot(wq_ref[...], hb)
    q_hi, q_lo = _split2(qt)
    q_hi_ref[...] = q_hi
    q_lo_ref[...] = q_lo
    nk = N_KEYS
    tb = x_ref.shape[0]

    def head(hd, carry):
        sv, sc = [], []
        for p in range(2):
            hp = hd * 2 + p
            rows = pl.ds(pl.multiple_of(hp * nk, nk), nk)
            s = _dot3(key_hi_ref[hp], key_lo_ref[hp], q_hi_ref[rows, :], q_lo_ref[rows, :])
            sc.append(s)
            sv.append(_top_values_tiled(s, PEER_TOPK + 1))
        k, half = PEER_TOPK, PEER_TOPK // 2
        sv0 = jnp.concatenate(sv[0][:k], axis=0)
        sv1 = jnp.concatenate(sv[1][:k], axis=0)
        edge = jnp.concatenate([sv[0][k] + sv[1][0], sv[0][0] + sv[1][k],
                                jnp.full((6, tb), -jnp.inf, F32)], axis=0)
        cand = jnp.concatenate([sv[0][0] + sv1]
                               + [sv[0][a] + sv1[:half] for a in range(1, half)]
                               + [sv0[half:] + sv[1][0], edge], axis=0)
        top = _top_values(cand, k + 1)
        z = top[0] * 0.0
        for cval in top[:k]:
            z = z + jnp.exp(cval - top[0])
        tau = 0.5 * (top[k - 1] + top[k])
        theta = tau - sc[0]
        c0 = jnp.zeros_like(theta)
        r1 = jnp.full_like(theta, float(k))
        for a in range(k):
            c0 = jnp.where(sv[1][a] >= theta, float(a + 1), c0)
            r1 = jnp.where(sc[1] == sv[1][a], float(a), r1)
        c0_o[hd] = c0
        r1_o[hd] = r1.astype(BF16)
        e0_o[hd] = jnp.exp(sc[0] - sv[0][0]) * (1.0 / z)
        e1_o[hd] = jnp.exp(sc[1] - sv[1][0]).astype(BF16)
        return carry

    per_trip = 4

    def head_group(i, carry):
        for hh in range(per_trip):
            head(per_trip * i + hh, carry)
        return carry

    lax.fori_loop(0, PEER_HEADS // per_trip, head_group, 0)


def _peer_route(xs, mod, ng, w, ncb_tokens):
    nb, t, d = xs.shape
    tb = TB_ROUTE
    nq = w["wq"].shape[0]
    weights = [ng, w["wq"], w["key_hi"], w["key_lo"]]
    sspec = pl.BlockSpec((None, PEER_HEADS, N_KEYS, tb), lambda b, j: (b, 0, 0, j))
    return pl.pallas_call(
        _peer_route_kernel,
        grid=(nb, t // tb),
        in_specs=[pl.BlockSpec((None, tb, d), lambda b, j: (b, j, 0)),
                  _mod_spec(nb, d, ncb_tokens // tb)] + [_full(a) for a in weights],
        out_specs=[pl.BlockSpec((None, d, tb), lambda b, j: (b, 0, j)), sspec, sspec, sspec, sspec],
        out_shape=[jax.ShapeDtypeStruct((nb, d, t), BF16),
                   jax.ShapeDtypeStruct((nb, PEER_HEADS, N_KEYS, t), F32),
                   jax.ShapeDtypeStruct((nb, PEER_HEADS, N_KEYS, t), F32),
                   jax.ShapeDtypeStruct((nb, PEER_HEADS, N_KEYS, t), BF16),
                   jax.ShapeDtypeStruct((nb, PEER_HEADS, N_KEYS, t), BF16)],
        scratch_shapes=[pltpu.VMEM((nq, tb), BF16), pltpu.VMEM((nq, tb), BF16)],
        compiler_params=_params(2),
        name="peer_route",
    )(xs, mod, *weights)


def _peer_dense_kernel(x_ref, ht_ref, c0_ref, e0_ref, r1_ref, e1_ref, modl_ref, modc_ref, u_ref, vt_ref, o_ref,
                       acc_ref, cp_ref, ep_ref, act_ref, wa0_ref, wa1_ref,
                       *, nctx, nec, ntile, nsteps):
    g = pl.program_id(0)
    ne = u_ref.shape[0]
    d, tb = acc_ref.shape
    nk = N_KEYS
    nq = DENSE_QUARTERS
    qe, qd = ne // nq, d // nq
    qi = qe // nk
    g_b = jnp.clip(g - 1, 0, nsteps - 1)
    g_c = jnp.clip(g - 2, 0, nsteps - 1)
    e_b = g_b % nec
    e_c = g_c % nec
    tok0 = ((g_c // nec) % ntile) * tb

    @pl.when(g == 0)
    def _():
        for ref in (act_ref, wa0_ref, wa1_ref, cp_ref, ep_ref):
            ref[...] = jnp.zeros_like(ref)

    @pl.when(e_b == 0)
    def _():
        for hd in range(PEER_HEADS):
            c0 = c0_ref[hd]
            e0 = e0_ref[hd]
            for p in range(nk // qi):
                cp_ref[hd, p, 0:qi, :] = c0[p * qi:(p + 1) * qi]
                ep_ref[hd, p, 0:qi, :] = e0[p * qi:(p + 1) * qi]

    @pl.when(e_c == 0)
    def _():
        acc_ref[...] = jnp.zeros_like(acc_ref)

    jr = 32

    new, old = g % 2, (g + 1) % 2

    def quarter(wa_new, wa_old, q, carry):
        r0 = pl.multiple_of(q * qe, qe)
        d0 = pl.multiple_of(q * qd, qd)
        pair = e_b * nq + q
        nj = nk // jr
        ka, kc = d // nj, ne // nj
        act_ref[new, pl.ds(r0, qe), :] = jnp.zeros((qe, tb), F32)
        half = jr // 2
        cb, eb = [], []
        for hd in range(PEER_HEADS):
            ct = cp_ref[hd, pair]
            et = ep_ref[hd, pair]
            for ii in range(qi):
                c16 = jnp.broadcast_to(ct[ii:ii + 1], (half, tb)).astype(BF16)
                e16 = jnp.broadcast_to(et[ii:ii + 1], (half, tb)).astype(BF16)
                cb.append(jnp.concatenate([c16, c16], axis=0))
                eb.append(jnp.concatenate([e16, e16], axis=0))
        for jq in range(nj):
            act_ref[new, pl.ds(r0, qe), :] += _dot(u_ref[pl.ds(r0, qe), jq * ka:(jq + 1) * ka],
                                                   ht_ref[jq * ka:(jq + 1) * ka, :])
            acc_ref[pl.ds(d0, qd), :] += _dot(vt_ref[pl.ds(d0, qd), jq * kc:(jq + 1) * kc],
                                              wa_old[jq * kc:(jq + 1) * kc, :])
            rows = slice(jq * jr, (jq + 1) * jr)
            accs = [None] * qi
            for hd in range(PEER_HEADS):
                r1 = r1_ref[hd, rows, :]
                e1 = e1_ref[hd, rows, :]
                for ii in range(qi):
                    term = jnp.where(r1 < cb[hd * qi + ii], e1, jnp.zeros_like(e1)) * eb[hd * qi + ii]
                    accs[ii] = term if accs[ii] is None else accs[ii] + term
            for ii in range(qi):
                er = pl.ds(pl.multiple_of(r0 + ii * nk + jq * jr, jr), jr)
                wa_new[er, :] = (accs[ii].astype(F32) * act_ref[old, er, :]).astype(BF16)

        a = act_ref[new, pl.ds(r0, qe), :]
        act_ref[new, pl.ds(r0, qe), :] = 0.5 * a * (1.0 + lax.erf(a * float(1.0 / np.sqrt(2.0))))
        return carry

    @pl.when(g % 2 == 0)
    def _():
        lax.fori_loop(0, nq, functools.partial(quarter, wa0_ref, wa1_ref), 0)

    @pl.when(g % 2 == 1)
    def _():
        lax.fori_loop(0, nq, functools.partial(quarter, wa1_ref, wa0_ref), 0)

    @pl.when((e_c == nec - 1) & (g >= 2))
    def _():
        tok = tok0 + lax.broadcasted_iota(jnp.int32, (tb, 1), 0)
        gate = jnp.where(tok < nctx, modc_ref[5:6, :], modl_ref[5:6, :])
        o_ref[...] = x_ref[...] + gate * acc_ref[...].T


def _peer_dense(xs, ht, c0, e0, r1, e1, mod, w, ncb_tokens):
    nb, t, d = xs.shape
    tb = TB_DENSE
    ne = EC_DENSE
    n_exp = w["u"].shape[0]
    assert t % tb == 0 and n_exp % ne == 0 and ne % (DENSE_QUARTERS * N_KEYS) == 0
    ntile, nec = t // tb, n_exp // ne
    nsteps = nb * ntile * nec
    pairs = N_KEYS // (ne // DENSE_QUARTERS // N_KEYS)

    def at(lag):
        def f(g):
            s = jnp.clip(g - lag, 0, nsteps - 1)
            return s // (ntile * nec), (s // nec) % ntile, s % nec
        return f

    sa, sb, sc = at(0), at(1), at(2)
    sspec = pl.BlockSpec((None, PEER_HEADS, N_KEYS, tb), lambda g: (sb(g)[0], 0, 0, sb(g)[1]))
    return pl.pallas_call(
        functools.partial(_peer_dense_kernel, nctx=ncb_tokens, nec=nec, ntile=ntile, nsteps=nsteps),
        grid=(nsteps + 2,),
        in_specs=[pl.BlockSpec((None, tb, d), lambda g: (sc(g)[0], sc(g)[1], 0)),
                  pl.BlockSpec((None, d, tb), lambda g: (sa(g)[0], 0, sa(g)[1])),
                  sspec, sspec, sspec, sspec,
                  pl.BlockSpec((None, 6, d), lambda g: (sc(g)[0], 0, 0)),
                  pl.BlockSpec((None, 6, d), lambda g: (nb, 0, 0)),
                  pl.BlockSpec((ne, d), lambda g: (sa(g)[2], 0)),
                  pl.BlockSpec((d, ne), lambda g: (0, sc(g)[2]))],
        out_specs=pl.BlockSpec((None, tb, d), lambda g: (sc(g)[0], sc(g)[1], 0)),
        out_shape=jax.ShapeDtypeStruct((nb, t, d), F32),
        scratch_shapes=[pltpu.VMEM((d, tb), F32),
                        pltpu.VMEM((PEER_HEADS, pairs, 8, tb), F32),
                        pltpu.VMEM((PEER_HEADS, pairs, 8, tb), F32),
                        pltpu.VMEM((2, ne, tb), F32),
                        pltpu.VMEM((ne, tb), BF16), pltpu.VMEM((ne, tb), BF16)],
        compiler_params=_params(1),
        name="peer_dense",
    )(xs, ht, c0, e0, r1, e1, mod, mod, w["u"], w["vt"])


def _final_norm_kernel(x_ref, g_ref, o_ref):
    x = x_ref[...]
    ms = jnp.mean(x * x, axis=-1, keepdims=True)
    o_ref[...] = x * lax.rsqrt(ms + NORM_EPS) * g_ref[...]


def _final_norm(xs, g, ncb_tokens):
    nb, t, d = xs.shape
    tb = TB_SCAN
    ncb = ncb_tokens // tb
    return pl.pallas_call(
        _final_norm_kernel,
        grid=(nb, t // tb - ncb),
        in_specs=[pl.BlockSpec((None, tb, d), lambda b, j: (b, j + ncb, 0)), _full(g)],
        out_specs=pl.BlockSpec((None, tb, d), lambda b, j: (b, j, 0)),
        out_shape=jax.ShapeDtypeStruct((nb, t - ncb_tokens, d), F32),
        compiler_params=_params(2),
        name="final_norm",
    )(xs, g)


def _row(a):
    return a.reshape(1, -1)


def _pad_dir(w2):
    z = jnp.zeros_like(w2[0])
    return jnp.stack([jnp.concatenate([w2[0], z], axis=0), jnp.concatenate([z, w2[1]], axis=0)]).astype(BF16)


def kernel(x, c, ctx, c_ctx, ada_w, ada_b, norm_mix, norm_ffn, rw_mix, rw_w_rkv, rw_w0, rw_w1, rw_w2, rw_a0, rw_a1, rw_a2, rw_g1, rw_g2, rw_k_k, rw_k_a, rw_r_k, rw_ln_w, rw_ln_b, rw_w_o, gla_w_in, gla_conv, gla_gk_up, gla_gk_b, gla_head_norm, gla_w_o, peer_wq, peer_keys, peer_u, peer_v, final_norm):
    nb, seq, d = x.shape
    nctx = ctx.shape[1]
    depth = ada_w.shape[0]
    assert nctx == TB_SCAN and seq % TB_SCAN == 0 and d % LANES == 0

    xs = jnp.concatenate([ctx, x], axis=1)
    cc = jnp.zeros((16, d), F32).at[:nb].set(c).at[nb].set(c_ctx)
    mods = _modulation(cc, ada_w, ada_b)[:, :nb + 1].reshape(depth, nb + 1, 6, d)

    bd_head = _block_diag_ones(RWKV_HEAD)
    ci = np.arange(GLA_CHUNK)
    tri = jnp.asarray(np.stack([ci[None, :] <= ci[:, None], ci[None, :] >= ci[:, None]]), dtype=BF16)

    for i in range(depth):
        mod = mods[i]
        j = i // 2
        if i % 2 == 0:
            w = dict(
                mix=rw_mix[j], w_rkv=rw_w_rkv[j].astype(BF16),
                w1=jnp.concatenate([rw_w1[j, 0], rw_w1[j, 1]], axis=1).astype(BF16),
                a1=jnp.concatenate([rw_a1[j, 0], rw_a1[j, 1]], axis=1).astype(BF16),
                g1=rw_g1[j].astype(BF16), w2=_pad_dir(rw_w2[j]), a2=_pad_dir(rw_a2[j]),
                g2=rw_g2[j].astype(BF16), w0=rw_w0[j], a0=rw_a0[j],
                k_k=_row(rw_k_k[j]), k_a=_row(rw_k_a[j]), r_k=_row(rw_r_k[j]),
                ln_w=_row(rw_ln_w[j]), ln_b=_row(rw_ln_b[j]), w_o=rw_w_o[j].astype(BF16), bd=bd_head)
            P = _rw_project(xs, mod, _row(norm_mix[i]), w, nctx)
            yf, yb = _rw_scan(P, bd_head, nctx)
            xs = _rw_readout(xs, yf, yb, P[3], P[4], mod, w, nctx)
        else:
            w_in = gla_w_in[j]
            qk_width = gla_gk_up.shape[-1] * 2
            conv_ch = qk_width + d
            dk = gla_gk_up.shape[-1] // GLA_HEADS
            dv = d // GLA_HEADS
            rank = gla_gk_up.shape[2]
            lr0 = conv_ch + d
            w_lr = jnp.zeros((d, LANES), F32).at[:, :2 * rank].set(w_in[:, lr0:lr0 + 2 * rank])
            up = jnp.zeros((LANES, d), F32)
            up = up.at[:rank, :qk_width // 2].set(gla_gk_up[j, 0]).at[rank:2 * rank, qk_width // 2:].set(gla_gk_up[j, 1])
            w = dict(w_qkv=w_in[:, :conv_ch].astype(BF16), w_g=w_in[:, conv_ch:lr0].astype(BF16),
                     w_lr=w_lr.astype(BF16), up=up.astype(BF16),
                     gk_b=jnp.concatenate([gla_gk_b[j, 0], gla_gk_b[j, 1]]).reshape(1, d),
                     head_norm=_row(gla_head_norm[j]), w_o=gla_w_o[j].astype(BF16))
            z, g, la = _gla_project(xs, mod, _row(norm_mix[i]), w, nctx)
            qkv = _gla_conv(z, gla_conv[j].reshape(9, conv_ch), nctx, qk_width, float(dk) ** -0.5)
            of, ob = _gla_scan(qkv, la, tri, nctx, dk, dv)
            xs = _gla_readout(xs, of, ob, g, mod, w, nctx, dv)

        keys = peer_keys[i].reshape(PEER_HEADS * 2, N_KEYS, -1)
        key_hi = keys.astype(BF16)
        pw = dict(wq=peer_wq[i].T.astype(BF16),
                  key_hi=key_hi, key_lo=(keys - key_hi.astype(F32)).astype(BF16),
                  u=peer_u[i].astype(BF16), vt=peer_v[i].T.astype(BF16))
        ht, c0, e0, r1, e1 = _peer_route(xs, mod, _row(norm_ffn[i]), pw, nctx)
        xs = _peer_dense(xs, ht, c0, e0, r1, e1, mod, pw, nctx)

    return _final_norm(xs, _row(final_norm), nctx)
```

```python
import functools

import numpy as np
import jax
import jax.numpy as jnp
from jax import lax
from jax.experimental import pallas as pl
from jax.experimental.pallas import tpu as pltpu

F32 = jnp.float32
BF16 = jnp.bfloat16

NORM_EPS = 1e-6
GRID_W = 64
RWKV_HEAD = 64
RWKV_GN_EPS = 64e-5
DECAY_LORA = 64
GLA_HEADS = 4
GLA_GATE_RANK = 16
GLA_GATE_NORM = 16.0
GLA_CHUNK = 64
PEER_HEADS = 8
N_KEYS = 128
PEER_TOPK = 16

LANES = 128
VMEM_LIMIT = 56 * 1024 * 1024

TB_PROJ = 128
TB_SCAN = 256
TB_ROUTE = 256
TB_DENSE = 256
EC_DENSE = 1024
DENSE_QUARTERS = 2


def _params(n_axes):
    return pltpu.CompilerParams(
        dimension_semantics=("arbitrary",) * n_axes,
        vmem_limit_bytes=VMEM_LIMIT)


def _full(a):
    nd = a.ndim
    return pl.BlockSpec(a.shape, lambda *_: (0,) * nd)


def _sigmoid(x):
    return 1.0 / (1.0 + jnp.exp(-x))


def _silu(x):
    return x * _sigmoid(x)


def _log_sigmoid(x):
    return jnp.minimum(x, 0.0) - jnp.log(1.0 + jnp.exp(-jnp.abs(x)))


def _modnorm(x, g, shift, scale):
    ms = jnp.mean(x * x, axis=-1, keepdims=True)
    return x * lax.rsqrt(ms + NORM_EPS) * (g * (1.0 + scale)) + shift


def _split2(x):
    hi = x.astype(BF16)
    lo = (x - hi.astype(F32)).astype(BF16)
    return hi, lo


def _split3(x):
    x1 = x.astype(BF16)
    r1 = x - x1.astype(F32)
    x2 = r1.astype(BF16)
    x3 = (r1 - x2.astype(F32)).astype(BF16)
    return x1, x2, x3


def _dot(a, b):
    return jnp.dot(a, b, preferred_element_type=F32)


def _dot_nt(a, b):
    return lax.dot_general(a, b, (((1,), (1,)), ((), ())), preferred_element_type=F32)


def _dot_tn(a, b):
    return lax.dot_general(a, b, (((0,), (0,)), ((), ())), preferred_element_type=F32)


def _dot3(a_hi, a_lo, b_hi, b_lo):
    return _dot(a_hi, b_hi) + _dot(a_hi, b_lo) + _dot(a_lo, b_hi)


def _seg_sum(x, bd):
    outs = []
    for c in range(x.shape[-1] // LANES):
        hi, lo = _split2(x[:, c * LANES:(c + 1) * LANES])
        outs.append(_dot(hi, bd) + _dot(lo, bd))
    return jnp.concatenate(outs, axis=-1)


def _block_diag_ones(group):
    i = np.arange(LANES)
    return jnp.asarray((i[:, None] // group) == (i[None, :] // group), dtype=BF16)


def _mod_kernel(c_ref, w_ref, b_ref, o_ref):
    a = _silu(c_ref[...])
    o_ref[...] = jnp.dot(a, w_ref[...], preferred_element_type=F32,
                         precision=lax.Precision.HIGHEST) + b_ref[...]


def _modulation(cc, ada_w, ada_b):
    depth, d, n = ada_w.shape
    nt = 768
    return pl.pallas_call(
        _mod_kernel,
        grid=(depth, n // nt),
        in_specs=[pl.BlockSpec(cc.shape, lambda i, k: (0, 0)),
                  pl.BlockSpec((None, d, nt), lambda i, k: (i, 0, k)),
                  pl.BlockSpec((None, 1, nt), lambda i, k: (i, 0, k))],
        out_specs=pl.BlockSpec((None, cc.shape[0], nt), lambda i, k: (i, 0, k)),
        out_shape=jax.ShapeDtypeStruct((depth, cc.shape[0], n), F32),
        compiler_params=_params(2),
        name="modulation",
    )(cc, ada_w, ada_b.reshape(depth, 1, n))


def _mod_spec(nb, d, ncb):
    return pl.BlockSpec((None, 6, d), lambda b, j: (jnp.where(j < ncb, nb, b), 0, 0))


def _rw_proj_kernel(x_ref, xp_ref, xn_ref, mod_ref, ng_ref, mix_ref, wrkv_ref,
                    w1_ref, a1_ref, g1_ref, w2_ref, a2_ref, g2_ref, w0_ref, a0_ref,
                    kk_w_ref, ka_ref, rk_ref, bd_ref,
                    r_o, v_o, kk_o, g_o, bon_o, w0_o, k0_o, b0_o, w1_o, k1_o, b1_o,
                    *, ncb, nblk):
    j = pl.program_id(1)
    tb = x_ref.shape[0]
    shift = mod_ref[0:1, :]
    scale = mod_ref[1:2, :]
    g = ng_ref[...]
    h = _modnorm(x_ref[...], g, shift, scale)
    hp = _modnorm(xp_ref[7:8, :], g, shift, scale)
    hn = _modnorm(xn_ref[0:1, :], g, shift, scale)
    hp = jnp.where((j != 0) & (j != ncb), hp, 0.0)
    hn = jnp.where((j != ncb - 1) & (j != nblk - 1), hn, 0.0)
    row = lax.broadcasted_iota(jnp.int32, (tb, 1), 0)
    prev = jnp.where(row == 0, hp, pltpu.roll(h, 1, 0))
    nxt = jnp.where(row == tb - 1, hn, pltpu.roll(h, tb - 1, 0))
    xx = 0.5 * (prev + nxt) - h
    xr, xw, xk, xv, xa, xg = (h + xx * mix_ref[i:i + 1, :] for i in range(6))

    r = _dot(xr.astype(BF16), wrkv_ref[0])
    k = _dot(xk.astype(BF16), wrkv_ref[1])
    v = _dot(xv.astype(BF16), wrkv_ref[2])
    gate = _dot(_sigmoid(_dot(xg.astype(BF16), g1_ref[...])).astype(BF16), g2_ref[...])
    tw = jnp.tanh(_dot(xw.astype(BF16), w1_ref[...])).astype(BF16)
    ta = _dot(xa.astype(BF16), a1_ref[...]).astype(BF16)

    bd = bd_ref[...]
    kk = k * kk_w_ref[...]
    kk = kk / jnp.maximum(jnp.sqrt(_seg_sum(kk * kk, bd)), 1e-12)

    r_o[...] = r
    v_o[...] = v
    kk_o[...] = kk
    g_o[...] = gate
    ksum = None
    for d, (w_o, k_o, b_o) in enumerate(((w0_o, k0_o, b0_o), (w1_o, k1_o, b1_o))):
        pre = w0_ref[d:d + 1, :] + _dot(tw, w2_ref[d])
        w_o[...] = jnp.exp(-_sigmoid(pre) * float(np.exp(-0.5)))
        a = _sigmoid(a0_ref[d:d + 1, :] + _dot(ta, a2_ref[d]))
        kd = k * (1.0 + (a - 1.0) * ka_ref[...])
        k_o[...] = kd
        b_o[...] = kk * a
        ksum = kd if ksum is None else ksum + kd
    bon_o[...] = _seg_sum(r * ksum * rk_ref[...], bd) * v


def _rw_project(xs, mod, ng, w, ncb_tokens):
    nb, t, d = xs.shape
    tb = TB_PROJ
    nblk, ncb = t // tb, ncb_tokens // tb
    sl = tb // 8
    weights = [ng, w["mix"], w["w_rkv"], w["w1"], w["a1"], w["g1"], w["w2"], w["a2"], w["g2"],
               w["w0"], w["a0"], w["k_k"], w["k_a"], w["r_k"], w["bd"]]
    tok = pl.BlockSpec((None, tb, d), lambda b, j: (b, j, 0))
    return pl.pallas_call(
        functools.partial(_rw_proj_kernel, ncb=ncb, nblk=nblk),
        grid=(nb, nblk),
        in_specs=[tok,
                  pl.BlockSpec((None, 8, d), lambda b, j: (b, jnp.maximum(j * sl - 1, 0), 0)),
                  pl.BlockSpec((None, 8, d), lambda b, j: (b, jnp.minimum((j + 1) * sl, t // 8 - 1), 0)),
                  _mod_spec(nb, d, ncb)] + [_full(a) for a in weights],
        out_specs=[tok] * 11,
        out_shape=[jax.ShapeDtypeStruct((nb, t, d), F32)] * 11,
        compiler_params=_params(2),
        name="rwkv_project",
    )(xs, xs, xs, mod, *weights)


def _rw_scan_kernel(rf, wf, kf, vf, kkf, bf, rb, wb, kb, vb, kkb, bb, bd_ref,
                    yf_o, yb_o, s_ref):
    tb = rf.shape[0]
    nch = rf.shape[1] // LANES

    @pl.when(pl.program_id(1) == 0)
    def _():
        s_ref[...] = jnp.zeros_like(s_ref)

    bd = bd_ref[...]
    hw = RWKV_HEAD
    vi = lax.broadcasted_iota(jnp.int32, (hw, LANES), 0)
    li = lax.broadcasted_iota(jnp.int32, (hw, LANES), 1)
    diag_bf = ((li & (hw - 1)) == vi).astype(F32).astype(BF16)
    lane = lax.broadcasted_iota(jnp.int32, (1, LANES), 1)
    first_head = lane < hw
    head_mask = (first_head.astype(F32), 1.0 - first_head.astype(F32))
    sub = 16

    gch = 4
    groups = [(d, list(range(c0, c0 + gch))) for c0 in range(0, nch, gch) for d in (0, 1)]
    fwd = (rf, wf, kf, vf, kkf, bf)
    bwd = (rb, wb, kb, vb, kkb, bb)

    def rows_of(s, i):
        return s[i * hw:(i + 1) * hw]

    def body(g, carry):
        base = (pl.multiple_of(g * sub, sub), pl.multiple_of(tb - sub - g * sub, sub))
        lanes = [slice(c * LANES, (c + 1) * LANES) for c in range(nch)]
        tiles = [[[ref[pl.ds(base[d], sub), ls] for ref in refs] for ls in lanes]
                 for d, refs in enumerate((fwd, bwd))]

        def rows(d, c, n):
            m = n if d == 0 else sub - 1 - n
            return [t[m:m + 1] for t in tiles[d][c]]

        def tile_bf(row):
            t = jnp.broadcast_to(row, (2 * 8, LANES)).astype(BF16)
            return jnp.concatenate([t] * (hw // 16), axis=0)

        def start(d, cs, sb, n):
            vcol = _dot(jnp.concatenate([tile_bf(rows(d, c, n)[3]) * diag_bf for c in cs], axis=0), bd)
            sa = _dot(sb * jnp.concatenate([tile_bf(rows(d, c, n)[4]) for c in cs], axis=0), bd)
            return vcol, sa

        state = [[s_ref[d, c] for c in cs] for d, cs in groups]
        pend = [start(d, cs, jnp.concatenate(state[gi], axis=0).astype(BF16), 0)
                for gi, (d, cs) in enumerate(groups)]
        ys = [[[] for _ in range(nch)] for _ in range(2)]
        for n in range(sub):
            for gi, (d, cs) in enumerate(groups):
                vcol, sa = pend[gi]
                s = []
                for i, c in enumerate(cs):
                    r, w, k, v, kk, b = rows(d, c, n)
                    s.append(state[gi][i] * w - rows_of(sa, i) * b + rows_of(vcol, i) * k)
                state[gi] = s
                sb = jnp.concatenate(s, axis=0).astype(BF16)
                if n + 1 < sub:
                    pend[gi] = start(d, cs, sb, n + 1)
                lhs = jnp.concatenate([rows(d, c, n)[0] * head_mask[h] for c in cs for h in range(2)], axis=0)
                out = _dot_nt(lhs.astype(BF16), sb)
                for m in range(gch // 2):
                    x = out[:, m * LANES:(m + 1) * LANES]
                    xr = pltpu.roll(x, hw, 1)
                    r0 = 4 * m
                    ys[d][cs[2 * m]].append(jnp.where(first_head, x[r0:r0 + 1], xr[r0 + 1:r0 + 2]))
                    ys[d][cs[2 * m + 1]].append(jnp.where(first_head, xr[r0 + 2:r0 + 3], x[r0 + 3:r0 + 4]))
        for gi, (d, cs) in enumerate(groups):
            for i, c in enumerate(cs):
                s_ref[d, c] = state[gi][i]
        for c, ls in enumerate(lanes):
            yf_o[pl.ds(base[0], sub), ls] = jnp.concatenate(ys[0][c], axis=0)
            yb_o[pl.ds(base[1], sub), ls] = jnp.concatenate(ys[1][c][::-1], axis=0)
        return carry

    lax.fori_loop(0, tb // sub, body, 0)


def _bwd_block(j, ncb, nblk):
    return jnp.where(j < ncb, ncb - 1 - j, nblk - 1 - (j - ncb))


def _rw_scan(P, bd, ncb_tokens):
    r, v, kk, _, _, w0, k0, b0, w1, k1, b1 = P
    nb, t, d = r.shape
    tb = TB_SCAN
    nblk, ncb = t // tb, ncb_tokens // tb
    fs = pl.BlockSpec((None, tb, d), lambda b, j: (b, j, 0))
    bs = pl.BlockSpec((None, tb, d), lambda b, j: (b, _bwd_block(j, ncb, nblk), 0))
    return pl.pallas_call(
        _rw_scan_kernel,
        grid=(nb, nblk),
        in_specs=[fs] * 6 + [bs] * 6 + [_full(bd)],
        out_specs=[fs, bs],
        out_shape=[jax.ShapeDtypeStruct((nb, t, d), F32)] * 2,
        scratch_shapes=[pltpu.VMEM((2, d // LANES, RWKV_HEAD, LANES), F32)],
        compiler_params=_params(2),
        name="rwkv_scan",
    )(r, w0, k0, v, kk, b0, r, w1, k1, v, kk, b1, bd)


def _rw_out_kernel(x_ref, yf_ref, yb_ref, g_ref, bon_ref, mod_ref, lnw_ref, lnb_ref, wo_ref, bd_ref, o_ref):
    bd = bd_ref[...]
    y = yf_ref[...] + yb_ref[...]
    mu = _seg_sum(y, bd) * (1.0 / RWKV_HEAD)
    yc = y - mu
    var = _seg_sum(yc * yc, bd) * (1.0 / RWKV_HEAD)
    yn = yc * lax.rsqrt(var + RWKV_GN_EPS) * lnw_ref[...] + lnb_ref[...]
    z = ((yn + bon_ref[...]) * g_ref[...]).astype(BF16)
    o_ref[...] = x_ref[...] + mod_ref[2:3, :] * _dot(z, wo_ref[...])


def _rw_readout(xs, yf, yb, gate, bon, mod, w, ncb_tokens):
    nb, t, d = xs.shape
    tb = TB_SCAN
    tok = pl.BlockSpec((None, tb, d), lambda b, j: (b, j, 0))
    weights = [w["ln_w"], w["ln_b"], w["w_o"], w["bd"]]
    return pl.pallas_call(
        _rw_out_kernel,
        grid=(nb, t // tb),
        in_specs=[tok] * 5 + [_mod_spec(nb, d, ncb_tokens // tb)] + [_full(a) for a in weights],
        out_specs=tok,
        out_shape=jax.ShapeDtypeStruct((nb, t, d), F32),
        compiler_params=_params(2),
        name="rwkv_readout",
    )(xs, yf, yb, gate, bon, mod, *weights)


def _gla_proj_kernel(x_ref, mod_ref, ng_ref, wqkv_ref, wg_ref, wlr_ref, up_ref, gkb_ref,
                     z_o, g_o, la_o):
    h = _modnorm(x_ref[...], ng_ref[...], mod_ref[0:1, :], mod_ref[1:2, :])
    hb = h.astype(BF16)
    z_o[...] = _dot(hb, wqkv_ref[...])
    g_o[...] = _dot(hb, wg_ref[...])
    lr = _dot(hb, wlr_ref[...]).astype(BF16)
    la_o[...] = _log_sigmoid(_dot(lr, up_ref[...]) + gkb_ref[...]) * (1.0 / GLA_GATE_NORM)


def _gla_project(xs, mod, ng, w, ncb_tokens):
    nb, t, d = xs.shape
    tb = TB_SCAN
    weights = [ng, w["w_qkv"], w["w_g"], w["w_lr"], w["up"], w["gk_b"]]
    tok = pl.BlockSpec((None, tb, d), lambda b, j: (b, j, 0))
    tok2 = pl.BlockSpec((None, tb, 2 * d), lambda b, j: (b, j, 0))
    return pl.pallas_call(
        _gla_proj_kernel,
        grid=(nb, t // tb),
        in_specs=[tok, _mod_spec(nb, d, ncb_tokens // tb)] + [_full(a) for a in weights],
        out_specs=[tok2, tok, tok],
        out_shape=[jax.ShapeDtypeStruct((nb, t, 2 * d), F32),
                   jax.ShapeDtypeStruct((nb, t, d), F32),
                   jax.ShapeDtypeStruct((nb, t, d), F32)],
        compiler_params=_params(2),
        name="gla_project",
    )(xs, mod, *weights)


def _gla_conv_kernel(zc_ref, zp_ref, zn_ref, cw_ref, o_ref, *, nblk, qk_width, q_scale):
    j = pl.program_id(1)
    tb, ch = zc_ref.shape
    is_ctx = j == 0
    up_ok = jnp.where(j > 1, 1.0, 0.0)
    dn_ok = jnp.where((j > 0) & (j < nblk - 1), 1.0, 0.0)
    vert = jnp.where(is_ctx, 0.0, 1.0)
    ext = tb + 2 * GRID_W
    pos = lax.broadcasted_iota(jnp.int32, (ext, 1), 0) - GRID_W
    col = jnp.where(is_ctx, pos, pos & (GRID_W - 1))
    no_left = col == 0
    no_right = col == jnp.where(is_ctx, tb - 1, GRID_W - 1)
    cc = 256
    for c in range(ch // cc):
        cs = slice(c * cc, (c + 1) * cc)
        e = jnp.concatenate([zp_ref[:, cs] * up_ok, zc_ref[:, cs], zn_ref[:, cs] * dn_ok], axis=0)
        em = jnp.where(no_left, 0.0, pltpu.roll(e, 1, 0))
        ep = jnp.where(no_right, 0.0, pltpu.roll(e, ext - 1, 0))
        acc = None
        for dy in range(3):
            lo = dy * GRID_W
            for dx, src in enumerate((em, e, ep)):
                wt = cw_ref[dy * 3 + dx:dy * 3 + dx + 1, cs]
                if dy != 1:
                    wt = wt * vert
                term = src[lo:lo + tb] * wt
                acc = term if acc is None else acc + term
        y = _silu(acc)
        if (c + 1) * cc <= qk_width // 2:
            y = y * q_scale
        o_ref[:, cs] = y


def _gla_conv(z, cw, ncb_tokens, qk_width, q_scale):
    nb, t, ch = z.shape
    tb = TB_SCAN
    assert ncb_tokens == tb and tb % GRID_W == 0
    sl = tb // GRID_W
    nblk = t // tb
    return pl.pallas_call(
        functools.partial(_gla_conv_kernel, nblk=nblk, qk_width=qk_width, q_scale=q_scale),
        grid=(nb, nblk),
        in_specs=[pl.BlockSpec((None, tb, ch), lambda b, j: (b, j, 0)),
                  pl.BlockSpec((None, GRID_W, ch), lambda b, j: (b, jnp.maximum(j * sl - 1, 0), 0)),
                  pl.BlockSpec((None, GRID_W, ch),
                               lambda b, j: (b, jnp.minimum((j + 1) * sl, t // GRID_W - 1), 0)),
                  _full(cw)],
        out_specs=pl.BlockSpec((None, tb, ch), lambda b, j: (b, j, 0)),
        out_shape=jax.ShapeDtypeStruct((nb, t, ch), F32),
        compiler_params=_params(2),
        name="gla_conv",
    )(z, z, z, cw)


def _gla_scan_kernel(qf_ref, laf_ref, qb_ref, lab_ref, tri_ref, of_o, ob_o, st_ref, *, dk, dv):
    tb = qf_ref.shape[0]
    c = GLA_CHUNK
    nh = GLA_HEADS
    qkw = nh * dk

    @pl.when(pl.program_id(1) == 0)
    def _():
        st_ref[...] = jnp.zeros_like(st_ref)

    dirs = ((qf_ref, laf_ref, of_o), (qb_ref, lab_ref, ob_o))
    nchunk = tb // c
    for p in range(nchunk):
        units = [(d, hh) for d in range(2) for hh in range(nh)]

        def load(d, hh):
            q_ref, la_ref, _ = dirs[d]
            n = p if d == 0 else nchunk - 1 - p
            rows = slice(n * c, (n + 1) * c)
            return (rows, q_ref[rows, hh * dk:(hh + 1) * dk], q_ref[rows, qkw + hh * dk:qkw + (hh + 1) * dk],
                    q_ref[rows, 2 * qkw + hh * dv:2 * qkw + (hh + 1) * dv].astype(BF16),
                    la_ref[rows, d * qkw + hh * dk:d * qkw + (hh + 1) * dk])

        data = [load(d, hh) for d, hh in units]
        bcums = []
        for (d, hh), (_, _, _, _, la) in zip(units, data):
            tri = tri_ref[d]
            l1, l2, l3 = _split3(la)
            bcums.append(_dot(tri, l1) + _dot(tri, l2) + _dot(tri, l3))
        intra = []
        for (d, hh), (_, q, k, _, _), bcum in zip(units, data, bcums):
            ref_row = c // 2 if d == 0 else c - 1 - c // 2
            ref = bcum[ref_row:ref_row + 1]
            a = _dot_nt((q * jnp.exp(bcum - ref)).astype(BF16), (k * jnp.exp(ref - bcum)).astype(BF16))
            intra.append(jnp.where(tri_ref[d] > 0, a, 0.0).astype(BF16))
        for (d, hh), (rows, q, k, vb, _), bcum, a in zip(units, data, bcums, intra):
            last_row = c - 1 if d == 0 else 0
            last = bcum[last_row:last_row + 1]
            st = st_ref[d, hh]
            o = _dot(a, vb) + _dot_nt((q * jnp.exp(bcum)).astype(BF16), st.astype(BF16))
            dirs[d][2][rows, hh * dv:(hh + 1) * dv] = o
            kd = (k * jnp.exp(last - bcum)).astype(BF16)
            st_ref[d, hh] = st * jnp.exp(last) + _dot_tn(vb, kd)


def _gla_scan(qkv, la, tri, ncb_tokens, dk, dv):
    nb, t, ch = qkv.shape
    d = la.shape[-1]
    tb = TB_SCAN
    nblk, ncb = t // tb, ncb_tokens // tb
    f2 = pl.BlockSpec((None, tb, ch), lambda b, j: (b, j, 0))
    b2 = pl.BlockSpec((None, tb, ch), lambda b, j: (b, _bwd_block(j, ncb, nblk), 0))
    f1 = pl.BlockSpec((None, tb, d), lambda b, j: (b, j, 0))
    b1 = pl.BlockSpec((None, tb, d), lambda b, j: (b, _bwd_block(j, ncb, nblk), 0))
    return pl.pallas_call(
        functools.partial(_gla_scan_kernel, dk=dk, dv=dv),
        grid=(nb, nblk),
        in_specs=[f2, f1, b2, b1, _full(tri)],
        out_specs=[f1, b1],
        out_shape=[jax.ShapeDtypeStruct((nb, t, d), F32)] * 2,
        scratch_shapes=[pltpu.VMEM((2, GLA_HEADS, dv, dk), F32)],
        compiler_params=_params(2),
        name="gla_scan",
    )(qkv, la, qkv, la, tri)


def _gla_out_kernel(x_ref, of_ref, ob_ref, g_ref, mod_ref, hn_ref, wo_ref, o_ref, *, dv):
    o = of_ref[...] + ob_ref[...]
    parts = []
    for hh in range(GLA_HEADS):
        oh = o[:, hh * dv:(hh + 1) * dv]
        ms = jnp.mean(oh * oh, axis=-1, keepdims=True)
        parts.append(oh * lax.rsqrt(ms + NORM_EPS) * hn_ref[...])
    on = jnp.concatenate(parts, axis=-1)
    z = (on * _silu(g_ref[...])).astype(BF16)
    o_ref[...] = x_ref[...] + mod_ref[2:3, :] * _dot(z, wo_ref[...])


def _gla_readout(xs, of, ob, g, mod, w, ncb_tokens, dv):
    nb, t, d = xs.shape
    tb = TB_SCAN
    tok = pl.BlockSpec((None, tb, d), lambda b, j: (b, j, 0))
    weights = [w["head_norm"], w["w_o"]]
    return pl.pallas_call(
        functools.partial(_gla_out_kernel, dv=dv),
        grid=(nb, t // tb),
        in_specs=[tok] * 4 + [_mod_spec(nb, d, ncb_tokens // tb)] + [_full(a) for a in weights],
        out_specs=tok,
        out_shape=jax.ShapeDtypeStruct((nb, t, d), F32),
        compiler_params=_params(2),
        name="gla_readout",
    )(xs, of, ob, g, mod, *weights)


def _top_values(s, k):
    vals = []
    cur = s
    for _ in range(k):
        m = jnp.max(cur, axis=0, keepdims=True)
        vals.append(m)
        cur = jnp.where(cur == m, -jnp.inf, cur)
    return vals


def _sorting_network(n):
    pairs = []
    p = 1
    while p < n:
        k = p
        while k >= 1:
            for j in range(k % p, n - k, 2 * k):
                for i in range(min(k, n - j - k)):
                    if (i + j) // (2 * p) == (i + j + k) // (2 * p):
                        pairs.append((i + j, i + j + k))
            k //= 2
        p *= 2
    return pairs


def _top_values_tiled(s, k):
    sub = 8
    n = s.shape[0] // sub
    lists = [s[g * sub:(g + 1) * sub] for g in range(n)]
    for a, b in _sorting_network(n):
        hi, lo = jnp.maximum(lists[a], lists[b]), jnp.minimum(lists[a], lists[b])
        lists[a], lists[b] = hi, lo
    lists.append(jnp.full_like(lists[0], -jnp.inf))
    vals = []
    for it in range(k):
        m = jnp.max(lists[0], axis=0, keepdims=True)
        vals.append(m)
        took = lists[0] == m
        for r in range(min(n, k - it - 1)):
            lists[r] = jnp.where(took, lists[r + 1], lists[r])
    return vals


def _peer_route_kernel(x_ref, mod_ref, ng_ref, wq_ref, key_hi_ref, key_lo_ref,
                       ht_o, c0_o, e0_o, r1_o, e1_o, q_hi_ref, q_lo_ref):
    h = _modnorm(x_ref[...], ng_ref[...], mod_ref[3:4, :], mod_ref[4:5, :])
    hb = h.T.astype(BF16)
    ht_o[...] = hb
    qt = _dot(wq_ref[...], hb)
    q_hi, q_lo = _split2(qt)
    q_hi_ref[...] = q_hi
    q_lo_ref[...] = q_lo
    nk = N_KEYS
    tb = x_ref.shape[0]

    def head(hd, carry):
        sv, sc = [], []
        for p in range(2):
            hp = hd * 2 + p
            rows = pl.ds(pl.multiple_of(hp * nk, nk), nk)
            s = _dot3(key_hi_ref[hp], key_lo_ref[hp], q_hi_ref[rows, :], q_lo_ref[rows, :])
            sc.append(s)
            sv.append(_top_values_tiled(s, PEER_TOPK + 1))
        k, half = PEER_TOPK, PEER_TOPK // 2
        sv0 = jnp.concatenate(sv[0][:k], axis=0)
        sv1 = jnp.concatenate(sv[1][:k], axis=0)
        edge = jnp.concatenate([sv[0][k] + sv[1][0], sv[0][0] + sv[1][k],
                                jnp.full((6, tb), -jnp.inf, F32)], axis=0)
        cand = jnp.concatenate([sv[0][0] + sv1]
                               + [sv[0][a] + sv1[:half] for a in range(1, half)]
                               + [sv0[half:] + sv[1][0], edge], axis=0)
        top = _top_values(cand, k + 1)
        z = top[0] * 0.0
        for cval in top[:k]:
            z = z + jnp.exp(cval - top[0])
        tau = 0.5 * (top[k - 1] + top[k])
        theta = tau - sc[0]
        c0 = jnp.zeros_like(theta)
        r1 = jnp.full_like(theta, float(k))
        for a in range(k):
            c0 = jnp.where(sv[1][a] >= theta, float(a + 1), c0)
            r1 = jnp.where(sc[1] == sv[1][a], float(a), r1)
        c0_o[hd] = c0
        r1_o[hd] = r1.astype(BF16)
        e0_o[hd] = jnp.exp(sc[0] - sv[0][0]) * (1.0 / z)
        e1_o[hd] = jnp.exp(sc[1] - sv[1][0]).astype(BF16)
        return carry

    per_trip = 4

    def head_group(i, carry):
        for hh in range(per_trip):
            head(per_trip * i + hh, carry)
        return carry

    lax.fori_loop(0, PEER_HEADS // per_trip, head_group, 0)


def _peer_route(xs, mod, ng, w, ncb_tokens):
    nb, t, d = xs.shape
    tb = TB_ROUTE
    nq = w["wq"].shape[0]
    weights = [ng, w["wq"], w["key_hi"], w["key_lo"]]
    sspec = pl.BlockSpec((None, PEER_HEADS, N_KEYS, tb), lambda b, j: (b, 0, 0, j))
    return pl.pallas_call(
        _peer_route_kernel,
        grid=(nb, t // tb),
        in_specs=[pl.BlockSpec((None, tb, d), lambda b, j: (b, j, 0)),
                  _mod_spec(nb, d, ncb_tokens // tb)] + [_full(a) for a in weights],
        out_specs=[pl.BlockSpec((None, d, tb), lambda b, j: (b, 0, j)), sspec, sspec, sspec, sspec],
        out_shape=[jax.ShapeDtypeStruct((nb, d, t), BF16),
                   jax.ShapeDtypeStruct((nb, PEER_HEADS, N_KEYS, t), F32),
                   jax.ShapeDtypeStruct((nb, PEER_HEADS, N_KEYS, t), F32),
                   jax.ShapeDtypeStruct((nb, PEER_HEADS, N_KEYS, t), BF16),
                   jax.ShapeDtypeStruct((nb, PEER_HEADS, N_KEYS, t), BF16)],
        scratch_shapes=[pltpu.VMEM((nq, tb), BF16), pltpu.VMEM((nq, tb), BF16)],
        compiler_params=_params(2),
        name="peer_route",
    )(xs, mod, *weights)


def _peer_dense_kernel(x_ref, ht_ref, c0_ref, e0_ref, r1_ref, e1_ref, modl_ref, modc_ref, u_ref, vt_ref, o_ref,
                       acc_ref, cp_ref, ep_ref, act_ref, wa0_ref, wa1_ref,
                       *, nctx, nec, ntile, nsteps):
    g = pl.program_id(0)
    ne = u_ref.shape[0]
    d, tb = acc_ref.shape
    nk = N_KEYS
    nq = DENSE_QUARTERS
    qe, qd = ne // nq, d // nq
    qi = qe // nk
    g_b = jnp.clip(g - 1, 0, nsteps - 1)
    g_c = jnp.clip(g - 2, 0, nsteps - 1)
    e_b = g_b % nec
    e_c = g_c % nec
    tok0 = ((g_c // nec) % ntile) * tb

    @pl.when(g == 0)
    def _():
        for ref in (act_ref, wa0_ref, wa1_ref, cp_ref, ep_ref):
            ref[...] = jnp.zeros_like(ref)

    @pl.when(e_b == 0)
    def _():
        for hd in range(PEER_HEADS):
            c0 = c0_ref[hd]
            e0 = e0_ref[hd]
            for p in range(nk // qi):
                cp_ref[hd, p, 0:qi, :] = c0[p * qi:(p + 1) * qi]
                ep_ref[hd, p, 0:qi, :] = e0[p * qi:(p + 1) * qi]

    @pl.when(e_c == 0)
    def _():
        acc_ref[...] = jnp.zeros_like(acc_ref)

    jr = 32

    new, old = g % 2, (g + 1) % 2

    def quarter(wa_new, wa_old, q, carry):
        r0 = pl.multiple_of(q * qe, qe)
        d0 = pl.multiple_of(q * qd, qd)
        pair = e_b * nq + q
        nj = nk // jr
        ka, kc = d // nj, ne // nj
        act_ref[new, pl.ds(r0, qe), :] = jnp.zeros((qe, tb), F32)
        half = jr // 2
        cb, eb = [], []
        for hd in range(PEER_HEADS):
            ct = cp_ref[hd, pair]
            et = ep_ref[hd, pair]
            for ii in range(qi):
                c16 = jnp.broadcast_to(ct[ii:ii + 1], (half, tb)).astype(BF16)
                e16 = jnp.broadcast_to(et[ii:ii + 1], (half, tb)).astype(BF16)
                cb.append(jnp.concatenate([c16, c16], axis=0))
                eb.append(jnp.concatenate([e16, e16], axis=0))
        for jq in range(nj):
            act_ref[new, pl.ds(r0, qe), :] += _dot(u_ref[pl.ds(r0, qe), jq * ka:(jq + 1) * ka],
                                                   ht_ref[jq * ka:(jq + 1) * ka, :])
            acc_ref[pl.ds(d0, qd), :] += _dot(vt_ref[pl.ds(d0, qd), jq * kc:(jq + 1) * kc],
                                              wa_old[jq * kc:(jq + 1) * kc, :])
            rows = slice(jq * jr, (jq + 1) * jr)
            accs = [None] * qi
            for hd in range(PEER_HEADS):
                r1 = r1_ref[hd, rows, :]
                e1 = e1_ref[hd, rows, :]
                for ii in range(qi):
                    term = jnp.where(r1 < cb[hd * qi + ii], e1, jnp.zeros_like(e1)) * eb[hd * qi + ii]
                    accs[ii] = term if accs[ii] is None else accs[ii] + term
            for ii in range(qi):
                er = pl.ds(pl.multiple_of(r0 + ii * nk + jq * jr, jr), jr)
                a = act_ref[old, er, :]
                act = 0.5 * a * (1.0 + lax.erf(a * float(1.0 / np.sqrt(2.0))))
                wa_new[er, :] = (accs[ii].astype(F32) * act).astype(BF16)
        return carry

    @pl.when(g % 2 == 0)
    def _():
        lax.fori_loop(0, nq, functools.partial(quarter, wa0_ref, wa1_ref), 0)

    @pl.when(g % 2 == 1)
    def _():
        lax.fori_loop(0, nq, functools.partial(quarter, wa1_ref, wa0_ref), 0)

    @pl.when((e_c == nec - 1) & (g >= 2))
    def _():
        tok = tok0 + lax.broadcasted_iota(jnp.int32, (tb, 1), 0)
        gate = jnp.where(tok < nctx, modc_ref[5:6, :], modl_ref[5:6, :])
        o_ref[...] = x_ref[...] + gate * acc_ref[...].T


def _peer_dense(xs, ht, c0, e0, r1, e1, mod, w, ncb_tokens):
    nb, t, d = xs.shape
    tb = TB_DENSE
    ne = EC_DENSE
    n_exp = w["u"].shape[0]
    assert t % tb == 0 and n_exp % ne == 0 and ne % (DENSE_QUARTERS * N_KEYS) == 0
    ntile, nec = t // tb, n_exp // ne
    nsteps = nb * ntile * nec
    pairs = N_KEYS // (ne // DENSE_QUARTERS // N_KEYS)

    def at(lag):
        def f(g):
            s = jnp.clip(g - lag, 0, nsteps - 1)
            return s // (ntile * nec), (s // nec) % ntile, s % nec
        return f

    sa, sb, sc = at(0), at(1), at(2)
    sspec = pl.BlockSpec((None, PEER_HEADS, N_KEYS, tb), lambda g: (sb(g)[0], 0, 0, sb(g)[1]))
    return pl.pallas_call(
        functools.partial(_peer_dense_kernel, nctx=ncb_tokens, nec=nec, ntile=ntile, nsteps=nsteps),
        grid=(nsteps + 2,),
        in_specs=[pl.BlockSpec((None, tb, d), lambda g: (sc(g)[0], sc(g)[1], 0)),
                  pl.BlockSpec((None, d, tb), lambda g: (sa(g)[0], 0, sa(g)[1])),
                  sspec, sspec, sspec, sspec,
                  pl.BlockSpec((None, 6, d), lambda g: (sc(g)[0], 0, 0)),
                  pl.BlockSpec((None, 6, d), lambda g: (nb, 0, 0)),
                  pl.BlockSpec((ne, d), lambda g: (sa(g)[2], 0)),
                  pl.BlockSpec((d, ne), lambda g: (0, sc(g)[2]))],
        out_specs=pl.BlockSpec((None, tb, d), lambda g: (sc(g)[0], sc(g)[1], 0)),
        out_shape=jax.ShapeDtypeStruct((nb, t, d), F32),
        scratch_shapes=[pltpu.VMEM((d, tb), F32),
                        pltpu.VMEM((PEER_HEADS, pairs, 8, tb), F32),
                        pltpu.VMEM((PEER_HEADS, pairs, 8, tb), F32),
                        pltpu.VMEM((2, ne, tb), F32),
                        pltpu.VMEM((ne, tb), BF16), pltpu.VMEM((ne, tb), BF16)],
        compiler_params=_params(1),
        name="peer_dense",
    )(xs, ht, c0, e0, r1, e1, mod, mod, w["u"], w["vt"])


def _final_norm_kernel(x_ref, g_ref, o_ref):
    x = x_ref[...]
    ms = jnp.mean(x * x, axis=-1, keepdims=True)
    o_ref[...] = x * lax.rsqrt(ms + NORM_EPS) * g_ref[...]


def _final_norm(xs, g, ncb_tokens):
    nb, t, d = xs.shape
    tb = TB_SCAN
    ncb = ncb_tokens // tb
    return pl.pallas_call(
        _final_norm_kernel,
        grid=(nb, t // tb - ncb),
        in_specs=[pl.BlockSpec((None, tb, d), lambda b, j: (b, j + ncb, 0)), _full(g)],
        out_specs=pl.BlockSpec((None, tb, d), lambda b, j: (b, j, 0)),
        out_shape=jax.ShapeDtypeStruct((nb, t - ncb_tokens, d), F32),
        compiler_params=_params(2),
        name="final_norm",
    )(xs, g)


def _row(a):
    return a.reshape(1, -1)


def _pad_dir(w2):
    z = jnp.zeros_like(w2[0])
    return jnp.stack([jnp.concatenate([w2[0], z], axis=0), jnp.concatenate([z, w2[1]], axis=0)]).astype(BF16)


def kernel(x, c, ctx, c_ctx, ada_w, ada_b, norm_mix, norm_ffn, rw_mix, rw_w_rkv, rw_w0, rw_w1, rw_w2, rw_a0, rw_a1, rw_a2, rw_g1, rw_g2, rw_k_k, rw_k_a, rw_r_k, rw_ln_w, rw_ln_b, rw_w_o, gla_w_in, gla_conv, gla_gk_up, gla_gk_b, gla_head_norm, gla_w_o, peer_wq, peer_keys, peer_u, peer_v, final_norm):
    nb, seq, d = x.shape
    nctx = ctx.shape[1]
    depth = ada_w.shape[0]
    assert nctx == TB_SCAN and seq % TB_SCAN == 0 and d % LANES == 0

    xs = jnp.concatenate([ctx, x], axis=1)
    cc = jnp.zeros((16, d), F32).at[:nb].set(c).at[nb].set(c_ctx)
    mods = _modulation(cc, ada_w, ada_b)[:, :nb + 1].reshape(depth, nb + 1, 6, d)

    bd_head = _block_diag_ones(RWKV_HEAD)
    ci = np.arange(GLA_CHUNK)
    tri = jnp.asarray(np.stack([ci[None, :] <= ci[:, None], ci[None, :] >= ci[:, None]]), dtype=BF16)

    for i in range(depth):
        mod = mods[i]
        j = i // 2
        if i % 2 == 0:
            w = dict(
                mix=rw_mix[j], w_rkv=rw_w_rkv[j].astype(BF16),
                w1=jnp.concatenate([rw_w1[j, 0], rw_w1[j, 1]], axis=1).astype(BF16),
                a1=jnp.concatenate([rw_a1[j, 0], rw_a1[j, 1]], axis=1).astype(BF16),
                g1=rw_g1[j].astype(BF16), w2=_pad_dir(rw_w2[j]), a2=_pad_dir(rw_a2[j]),
                g2=rw_g2[j].astype(BF16), w0=rw_w0[j], a0=rw_a0[j],
                k_k=_row(rw_k_k[j]), k_a=_row(rw_k_a[j]), r_k=_row(rw_r_k[j]),
                ln_w=_row(rw_ln_w[j]), ln_b=_row(rw_ln_b[j]), w_o=rw_w_o[j].astype(BF16), bd=bd_head)
            P = _rw_project(xs, mod, _row(norm_mix[i]), w, nctx)
            yf, yb = _rw_scan(P, bd_head, nctx)
            xs = _rw_readout(xs, yf, yb, P[3], P[4], mod, w, nctx)
        else:
            w_in = gla_w_in[j]
            qk_width = gla_gk_up.shape[-1] * 2
            conv_ch = qk_width + d
            dk = gla_gk_up.shape[-1] // GLA_HEADS
            dv = d // GLA_HEADS
            rank = gla_gk_up.shape[2]
            lr0 = conv_ch + d
            w_lr = jnp.zeros((d, LANES), F32).at[:, :2 * rank].set(w_in[:, lr0:lr0 + 2 * rank])
            up = jnp.zeros((LANES, d), F32)
            up = up.at[:rank, :qk_width // 2].set(gla_gk_up[j, 0]).at[rank:2 * rank, qk_width // 2:].set(gla_gk_up[j, 1])
            w = dict(w_qkv=w_in[:, :conv_ch].astype(BF16), w_g=w_in[:, conv_ch:lr0].astype(BF16),
                     w_lr=w_lr.astype(BF16), up=up.astype(BF16),
                     gk_b=jnp.concatenate([gla_gk_b[j, 0], gla_gk_b[j, 1]]).reshape(1, d),
                     head_norm=_row(gla_head_norm[j]), w_o=gla_w_o[j].astype(BF16))
            z, g, la = _gla_project(xs, mod, _row(norm_mix[i]), w, nctx)
            qkv = _gla_conv(z, gla_conv[j].reshape(9, conv_ch), nctx, qk_width, float(dk) ** -0.5)
            of, ob = _gla_scan(qkv, la, tri, nctx, dk, dv)
            xs = _gla_readout(xs, of, ob, g, mod, w, nctx, dv)

        keys = peer_keys[i].reshape(PEER_HEADS * 2, N_KEYS, -1)
        key_hi = keys.astype(BF16)
        pw = dict(wq=peer_wq[i].T.astype(BF16),
                  key_hi=key_hi, key_lo=(keys - key_hi.astype(F32)).astype(BF16),
                  u=peer_u[i].astype(BF16), vt=peer_v[i].T.astype(BF16))
        ht, c0, e0, r1, e1 = _peer_route(xs, mod, _row(norm_ffn[i]), pw, nctx)
        xs = _peer_dense(xs, ht, c0, e0, r1, e1, mod, pw, nctx)

    return _final_norm(xs, _row(final_norm), nctx)
```

```python
import functools

import numpy as np
import jax
import jax.numpy as jnp
from jax import lax
from jax.experimental import pallas as pl
from jax.experimental.pallas import tpu as pltpu

F32 = jnp.float32
BF16 = jnp.bfloat16

NORM_EPS = 1e-6
GRID_W = 64
RWKV_HEAD = 64
RWKV_GN_EPS = 64e-5
DECAY_LORA = 64
GLA_HEADS = 4
GLA_GATE_RANK = 16
GLA_GATE_NORM = 16.0
GLA_CHUNK = 64
PEER_HEADS = 8
N_KEYS = 128
PEER_TOPK = 16

LANES = 128
VMEM_LIMIT = 56 * 1024 * 1024

TB_PROJ = 128
TB_SCAN = 256
TB_ROUTE = 256
TB_DENSE = 256
EC_DENSE = 2048
DENSE_QUARTERS = 4


def _params(n_axes):
    return pltpu.CompilerParams(
        dimension_semantics=("arbitrary",) * n_axes,
        vmem_limit_bytes=VMEM_LIMIT)


def _full(a):
    nd = a.ndim
    return pl.BlockSpec(a.shape, lambda *_: (0,) * nd)


def _sigmoid(x):
    return 1.0 / (1.0 + jnp.exp(-x))


def _silu(x):
    return x * _sigmoid(x)


def _log_sigmoid(x):
    return jnp.minimum(x, 0.0) - jnp.log(1.0 + jnp.exp(-jnp.abs(x)))


def _modnorm(x, g, shift, scale):
    ms = jnp.mean(x * x, axis=-1, keepdims=True)
    return x * lax.rsqrt(ms + NORM_EPS) * (g * (1.0 + scale)) + shift


def _split2(x):
    hi = x.astype(BF16)
    lo = (x - hi.astype(F32)).astype(BF16)
    return hi, lo


def _split3(x):
    x1 = x.astype(BF16)
    r1 = x - x1.astype(F32)
    x2 = r1.astype(BF16)
    x3 = (r1 - x2.astype(F32)).astype(BF16)
    return x1, x2, x3


def _dot(a, b):
    return jnp.dot(a, b, preferred_element_type=F32)


def _dot_nt(a, b):
    return lax.dot_general(a, b, (((1,), (1,)), ((), ())), preferred_element_type=F32)


def _dot_tn(a, b):
    return lax.dot_general(a, b, (((0,), (0,)), ((), ())), preferred_element_type=F32)


def _dot3(a_hi, a_lo, b_hi, b_lo):
    return _dot(a_hi, b_hi) + _dot(a_hi, b_lo) + _dot(a_lo, b_hi)


def _seg_sum(x, bd):
    outs = []
    for c in range(x.shape[-1] // LANES):
        hi, lo = _split2(x[:, c * LANES:(c + 1) * LANES])
        outs.append(_dot(hi, bd) + _dot(lo, bd))
    return jnp.concatenate(outs, axis=-1)


def _block_diag_ones(group):
    i = np.arange(LANES)
    return jnp.asarray((i[:, None] // group) == (i[None, :] // group), dtype=BF16)


def _mod_kernel(c_ref, w_ref, b_ref, o_ref):
    a = _silu(c_ref[...])
    o_ref[...] = jnp.dot(a, w_ref[...], preferred_element_type=F32,
                         precision=lax.Precision.HIGHEST) + b_ref[...]


def _modulation(cc, ada_w, ada_b):
    depth, d, n = ada_w.shape
    nt = 768
    return pl.pallas_call(
        _mod_kernel,
        grid=(depth, n // nt),
        in_specs=[pl.BlockSpec(cc.shape, lambda i, k: (0, 0)),
                  pl.BlockSpec((None, d, nt), lambda i, k: (i, 0, k)),
                  pl.BlockSpec((None, 1, nt), lambda i, k: (i, 0, k))],
        out_specs=pl.BlockSpec((None, cc.shape[0], nt), lambda i, k: (i, 0, k)),
        out_shape=jax.ShapeDtypeStruct((depth, cc.shape[0], n), F32),
        compiler_params=_params(2),
        name="modulation",
    )(cc, ada_w, ada_b.reshape(depth, 1, n))


def _mod_spec(nb, d, ncb):
    return pl.BlockSpec((None, 6, d), lambda b, j: (jnp.where(j < ncb, nb, b), 0, 0))


def _rw_proj_kernel(x_ref, xp_ref, xn_ref, mod_ref, ng_ref, mix_ref, wrkv_ref,
                    w1_ref, a1_ref, g1_ref, w2_ref, a2_ref, g2_ref, w0_ref, a0_ref,
                    kk_w_ref, ka_ref, rk_ref, bd_ref,
                    r_o, v_o, kk_o, g_o, bon_o, w0_o, k0_o, b0_o, w1_o, k1_o, b1_o,
                    *, ncb, nblk):
    j = pl.program_id(1)
    tb = x_ref.shape[0]
    shift = mod_ref[0:1, :]
    scale = mod_ref[1:2, :]
    g = ng_ref[...]
    h = _modnorm(x_ref[...], g, shift, scale)
    hp = _modnorm(xp_ref[7:8, :], g, shift, scale)
    hn = _modnorm(xn_ref[0:1, :], g, shift, scale)
    hp = jnp.where((j != 0) & (j != ncb), hp, 0.0)
    hn = jnp.where((j != ncb - 1) & (j != nblk - 1), hn, 0.0)
    row = lax.broadcasted_iota(jnp.int32, (tb, 1), 0)
    prev = jnp.where(row == 0, hp, pltpu.roll(h, 1, 0))
    nxt = jnp.where(row == tb - 1, hn, pltpu.roll(h, tb - 1, 0))
    xx = 0.5 * (prev + nxt) - h
    xr, xw, xk, xv, xa, xg = (h + xx * mix_ref[i:i + 1, :] for i in range(6))

    r = _dot(xr.astype(BF16), wrkv_ref[0])
    k = _dot(xk.astype(BF16), wrkv_ref[1])
    v = _dot(xv.astype(BF16), wrkv_ref[2])
    gate = _dot(_sigmoid(_dot(xg.astype(BF16), g1_ref[...])).astype(BF16), g2_ref[...])
    tw = jnp.tanh(_dot(xw.astype(BF16), w1_ref[...])).astype(BF16)
    ta = _dot(xa.astype(BF16), a1_ref[...]).astype(BF16)

    bd = bd_ref[...]
    kk = k * kk_w_ref[...]
    kk = kk / jnp.maximum(jnp.sqrt(_seg_sum(kk * kk, bd)), 1e-12)

    r_o[...] = r
    v_o[...] = v
    kk_o[...] = kk
    g_o[...] = gate
    ksum = None
    for d, (w_o, k_o, b_o) in enumerate(((w0_o, k0_o, b0_o), (w1_o, k1_o, b1_o))):
        pre = w0_ref[d:d + 1, :] + _dot(tw, w2_ref[d])
        w_o[...] = jnp.exp(-_sigmoid(pre) * float(np.exp(-0.5)))
        a = _sigmoid(a0_ref[d:d + 1, :] + _dot(ta, a2_ref[d]))
        kd = k * (1.0 + (a - 1.0) * ka_ref[...])
        k_o[...] = kd
        b_o[...] = kk * a
        ksum = kd if ksum is None else ksum + kd
    bon_o[...] = _seg_sum(r * ksum * rk_ref[...], bd) * v


def _rw_project(xs, mod, ng, w, ncb_tokens):
    nb, t, d = xs.shape
    tb = TB_PROJ
    nblk, ncb = t // tb, ncb_tokens // tb
    sl = tb // 8
    weights = [ng, w["mix"], w["w_rkv"], w["w1"], w["a1"], w["g1"], w["w2"], w["a2"], w["g2"],
               w["w0"], w["a0"], w["k_k"], w["k_a"], w["r_k"], w["bd"]]
    tok = pl.BlockSpec((None, tb, d), lambda b, j: (b, j, 0))
    return pl.pallas_call(
        functools.partial(_rw_proj_kernel, ncb=ncb, nblk=nblk),
        grid=(nb, nblk),
        in_specs=[tok,
                  pl.BlockSpec((None, 8, d), lambda b, j: (b, jnp.maximum(j * sl - 1, 0), 0)),
                  pl.BlockSpec((None, 8, d), lambda b, j: (b, jnp.minimum((j + 1) * sl, t // 8 - 1), 0)),
                  _mod_spec(nb, d, ncb)] + [_full(a) for a in weights],
        out_specs=[tok] * 11,
        out_shape=[jax.ShapeDtypeStruct((nb, t, d), F32)] * 11,
        compiler_params=_params(2),
        name="rwkv_project",
    )(xs, xs, xs, mod, *weights)


def _rw_scan_kernel(rf, wf, kf, vf, kkf, bf, rb, wb, kb, vb, kkb, bb, bd_ref,
                    yf_o, yb_o, s_ref):
    tb = rf.shape[0]
    nch = rf.shape[1] // LANES

    @pl.when(pl.program_id(1) == 0)
    def _():
        s_ref[...] = jnp.zeros_like(s_ref)

    bd = bd_ref[...]
    hw = RWKV_HEAD
    vi = lax.broadcasted_iota(jnp.int32, (hw, LANES), 0)
    li = lax.broadcasted_iota(jnp.int32, (hw, LANES), 1)
    diag_bf = ((li & (hw - 1)) == vi).astype(F32).astype(BF16)
    lane = lax.broadcasted_iota(jnp.int32, (1, LANES), 1)
    first_head = lane < hw
    head_mask = (first_head.astype(F32), 1.0 - first_head.astype(F32))
    sub = 16

    gch = 4
    groups = [(d, list(range(c0, c0 + gch))) for c0 in range(0, nch, gch) for d in (0, 1)]
    fwd = (rf, wf, kf, vf, kkf, bf)
    bwd = (rb, wb, kb, vb, kkb, bb)

    def rows_of(s, i):
        return s[i * hw:(i + 1) * hw]

    def body(g, carry):
        base = (pl.multiple_of(g * sub, sub), pl.multiple_of(tb - sub - g * sub, sub))
        lanes = [slice(c * LANES, (c + 1) * LANES) for c in range(nch)]
        tiles = [[[ref[pl.ds(base[d], sub), ls] for ref in refs] for ls in lanes]
                 for d, refs in enumerate((fwd, bwd))]

        def rows(d, c, n):
            m = n if d == 0 else sub - 1 - n
            return [t[m:m + 1] for t in tiles[d][c]]

        def tile_bf(row):
            t = jnp.broadcast_to(row, (2 * 8, LANES)).astype(BF16)
            return jnp.concatenate([t] * (hw // 16), axis=0)

        def start(gi, sb, n):
            d, cs = groups[gi]
            vcol = _dot(jnp.concatenate([tile_bf(rows(d, c, n)[3]) * diag_bf for c in cs], axis=0), bd)
            sa = _dot(sb * jnp.concatenate([tile_bf(rows(d, c, n)[4]) for c in cs], axis=0), bd)
            return vcol, sa

        state = [[s_ref[d, c] for c in cs] for d, cs in groups]
        pend = [start(gi, jnp.concatenate(state[gi], axis=0).astype(BF16), 0) for gi in range(len(groups))]
        ys = [[[] for _ in range(nch)] for _ in range(2)]
        for n in range(sub):
            for gi, (d, cs) in enumerate(groups):
                vcol, sa = pend[gi]
                s = []
                for i, c in enumerate(cs):
                    r, w, k, v, kk, b = rows(d, c, n)
                    s.append(state[gi][i] * w - rows_of(sa, i) * b + rows_of(vcol, i) * k)
                state[gi] = s
                sb = jnp.concatenate(s, axis=0).astype(BF16)
                if n + 1 < sub:
                    pend[gi] = start(gi, sb, n + 1)
                lhs = jnp.concatenate([rows(d, c, n)[0] * head_mask[h] for c in cs for h in range(2)], axis=0)
                out = _dot_nt(lhs.astype(BF16), sb)
                for m in range(gch // 2):
                    x = out[:, m * LANES:(m + 1) * LANES]
                    xr = pltpu.roll(x, hw, 1)
                    r0 = 4 * m
                    ys[d][cs[2 * m]].append(jnp.where(first_head, x[r0:r0 + 1], xr[r0 + 1:r0 + 2]))
                    ys[d][cs[2 * m + 1]].append(jnp.where(first_head, xr[r0 + 2:r0 + 3], x[r0 + 3:r0 + 4]))
        for gi, (d, cs) in enumerate(groups):
            for i, c in enumerate(cs):
                s_ref[d, c] = state[gi][i]
        for c, ls in enumerate(lanes):
            yf_o[pl.ds(base[0], sub), ls] = jnp.concatenate(ys[0][c], axis=0)
            yb_o[pl.ds(base[1], sub), ls] = jnp.concatenate(ys[1][c][::-1], axis=0)
        return carry

    lax.fori_loop(0, tb // sub, body, 0)


def _bwd_block(j, ncb, nblk):
    return jnp.where(j < ncb, ncb - 1 - j, nblk - 1 - (j - ncb))


def _rw_scan(P, bd, ncb_tokens):
    r, v, kk, _, _, w0, k0, b0, w1, k1, b1 = P
    nb, t, d = r.shape
    tb = TB_SCAN
    nblk, ncb = t // tb, ncb_tokens // tb
    fs = pl.BlockSpec((None, tb, d), lambda b, j: (b, j, 0))
    bs = pl.BlockSpec((None, tb, d), lambda b, j: (b, _bwd_block(j, ncb, nblk), 0))
    return pl.pallas_call(
        _rw_scan_kernel,
        grid=(nb, nblk),
        in_specs=[fs] * 6 + [bs] * 6 + [_full(bd)],
        out_specs=[fs, bs],
        out_shape=[jax.ShapeDtypeStruct((nb, t, d), F32)] * 2,
        scratch_shapes=[pltpu.VMEM((2, d // LANES, RWKV_HEAD, LANES), F32)],
        compiler_params=_params(2),
        name="rwkv_scan",
    )(r, w0, k0, v, kk, b0, r, w1, k1, v, kk, b1, bd)


def _rw_out_kernel(x_ref, yf_ref, yb_ref, g_ref, bon_ref, mod_ref, lnw_ref, lnb_ref, wo_ref, bd_ref, o_ref):
    bd = bd_ref[...]
    y = yf_ref[...] + yb_ref[...]
    mu = _seg_sum(y, bd) * (1.0 / RWKV_HEAD)
    yc = y - mu
    var = _seg_sum(yc * yc, bd) * (1.0 / RWKV_HEAD)
    yn = yc * lax.rsqrt(var + RWKV_GN_EPS) * lnw_ref[...] + lnb_ref[...]
    z = ((yn + bon_ref[...]) * g_ref[...]).astype(BF16)
    o_ref[...] = x_ref[...] + mod_ref[2:3, :] * _dot(z, wo_ref[...])


def _rw_readout(xs, yf, yb, gate, bon, mod, w, ncb_tokens):
    nb, t, d = xs.shape
    tb = TB_SCAN
    tok = pl.BlockSpec((None, tb, d), lambda b, j: (b, j, 0))
    weights = [w["ln_w"], w["ln_b"], w["w_o"], w["bd"]]
    return pl.pallas_call(
        _rw_out_kernel,
        grid=(nb, t // tb),
        in_specs=[tok] * 5 + [_mod_spec(nb, d, ncb_tokens // tb)] + [_full(a) for a in weights],
        out_specs=tok,
        out_shape=jax.ShapeDtypeStruct((nb, t, d), F32),
        compiler_params=_params(2),
        name="rwkv_readout",
    )(xs, yf, yb, gate, bon, mod, *weights)


def _gla_proj_kernel(x_ref, mod_ref, ng_ref, wqkv_ref, wg_ref, wlr_ref, up_ref, gkb_ref,
                     z_o, g_o, la_o):
    h = _modnorm(x_ref[...], ng_ref[...], mod_ref[0:1, :], mod_ref[1:2, :])
    hb = h.astype(BF16)
    z_o[...] = _dot(hb, wqkv_ref[...])
    g_o[...] = _dot(hb, wg_ref[...])
    lr = _dot(hb, wlr_ref[...]).astype(BF16)
    la_o[...] = _log_sigmoid(_dot(lr, up_ref[...]) + gkb_ref[...]) * (1.0 / GLA_GATE_NORM)


def _gla_project(xs, mod, ng, w, ncb_tokens):
    nb, t, d = xs.shape
    tb = TB_SCAN
    weights = [ng, w["w_qkv"], w["w_g"], w["w_lr"], w["up"], w["gk_b"]]
    tok = pl.BlockSpec((None, tb, d), lambda b, j: (b, j, 0))
    tok2 = pl.BlockSpec((None, tb, 2 * d), lambda b, j: (b, j, 0))
    return pl.pallas_call(
        _gla_proj_kernel,
        grid=(nb, t // tb),
        in_specs=[tok, _mod_spec(nb, d, ncb_tokens // tb)] + [_full(a) for a in weights],
        out_specs=[tok2, tok, tok],
        out_shape=[jax.ShapeDtypeStruct((nb, t, 2 * d), F32),
                   jax.ShapeDtypeStruct((nb, t, d), F32),
                   jax.ShapeDtypeStruct((nb, t, d), F32)],
        compiler_params=_params(2),
        name="gla_project",
    )(xs, mod, *weights)


def _gla_conv_kernel(zc_ref, zp_ref, zn_ref, cw_ref, o_ref, *, nblk, qk_width, q_scale):
    j = pl.program_id(1)
    tb, ch = zc_ref.shape
    is_ctx = j == 0
    up_ok = jnp.where(j > 1, 1.0, 0.0)
    dn_ok = jnp.where((j > 0) & (j < nblk - 1), 1.0, 0.0)
    vert = jnp.where(is_ctx, 0.0, 1.0)
    ext = tb + 2 * GRID_W
    pos = lax.broadcasted_iota(jnp.int32, (ext, 1), 0) - GRID_W
    col = jnp.where(is_ctx, pos, pos & (GRID_W - 1))
    no_left = col == 0
    no_right = col == jnp.where(is_ctx, tb - 1, GRID_W - 1)
    cc = 256
    for c in range(ch // cc):
        cs = slice(c * cc, (c + 1) * cc)
        e = jnp.concatenate([zp_ref[:, cs] * up_ok, zc_ref[:, cs], zn_ref[:, cs] * dn_ok], axis=0)
        em = jnp.where(no_left, 0.0, pltpu.roll(e, 1, 0))
        ep = jnp.where(no_right, 0.0, pltpu.roll(e, ext - 1, 0))
        acc = None
        for dy in range(3):
            lo = dy * GRID_W
            for dx, src in enumerate((em, e, ep)):
                wt = cw_ref[dy * 3 + dx:dy * 3 + dx + 1, cs]
                if dy != 1:
                    wt = wt * vert
                term = src[lo:lo + tb] * wt
                acc = term if acc is None else acc + term
        y = _silu(acc)
        if (c + 1) * cc <= qk_width // 2:
            y = y * q_scale
        o_ref[:, cs] = y


def _gla_conv(z, cw, ncb_tokens, qk_width, q_scale):
    nb, t, ch = z.shape
    tb = TB_SCAN
    assert ncb_tokens == tb and tb % GRID_W == 0
    sl = tb // GRID_W
    nblk = t // tb
    return pl.pallas_call(
        functools.partial(_gla_conv_kernel, nblk=nblk, qk_width=qk_width, q_scale=q_scale),
        grid=(nb, nblk),
        in_specs=[pl.BlockSpec((None, tb, ch), lambda b, j: (b, j, 0)),
                  pl.BlockSpec((None, GRID_W, ch), lambda b, j: (b, jnp.maximum(j * sl - 1, 0), 0)),
                  pl.BlockSpec((None, GRID_W, ch),
                               lambda b, j: (b, jnp.minimum((j + 1) * sl, t // GRID_W - 1), 0)),
                  _full(cw)],
        out_specs=pl.BlockSpec((None, tb, ch), lambda b, j: (b, j, 0)),
        out_shape=jax.ShapeDtypeStruct((nb, t, ch), F32),
        compiler_params=_params(2),
        name="gla_conv",
    )(z, z, z, cw)


def _gla_scan_kernel(qf_ref, laf_ref, qb_ref, lab_ref, tri_ref, of_o, ob_o, st_ref, *, dk, dv):
    tb = qf_ref.shape[0]
    c = GLA_CHUNK
    nh = GLA_HEADS
    qkw = nh * dk

    @pl.when(pl.program_id(1) == 0)
    def _():
        st_ref[...] = jnp.zeros_like(st_ref)

    dirs = ((qf_ref, laf_ref, of_o), (qb_ref, lab_ref, ob_o))
    nchunk = tb // c
    for p in range(nchunk):
        units = [(d, hh) for d in range(2) for hh in range(nh)]

        def load(d, hh):
            q_ref, la_ref, _ = dirs[d]
            n = p if d == 0 else nchunk - 1 - p
            rows = slice(n * c, (n + 1) * c)
            return (rows, q_ref[rows, hh * dk:(hh + 1) * dk], q_ref[rows, qkw + hh * dk:qkw + (hh + 1) * dk],
                    q_ref[rows, 2 * qkw + hh * dv:2 * qkw + (hh + 1) * dv].astype(BF16),
                    la_ref[rows, d * qkw + hh * dk:d * qkw + (hh + 1) * dk])

        data = [load(d, hh) for d, hh in units]
        bcums = []
        for (d, hh), (_, _, _, _, la) in zip(units, data):
            tri = tri_ref[d]
            l1, l2, l3 = _split3(la)
            bcums.append(_dot(tri, l1) + _dot(tri, l2) + _dot(tri, l3))
        intra = []
        for (d, hh), (_, q, k, _, _), bcum in zip(units, data, bcums):
            ref_row = c // 2 if d == 0 else c - 1 - c // 2
            ref = bcum[ref_row:ref_row + 1]
            a = _dot_nt((q * jnp.exp(bcum - ref)).astype(BF16), (k * jnp.exp(ref - bcum)).astype(BF16))
            intra.append(jnp.where(tri_ref[d] > 0, a, 0.0).astype(BF16))
        for (d, hh), (rows, q, k, vb, _), bcum, a in zip(units, data, bcums, intra):
            last_row = c - 1 if d == 0 else 0
            last = bcum[last_row:last_row + 1]
            st = st_ref[d, hh]
            o = _dot(a, vb) + _dot_nt((q * jnp.exp(bcum)).astype(BF16), st.astype(BF16))
            dirs[d][2][rows, hh * dv:(hh + 1) * dv] = o
            kd = (k * jnp.exp(last - bcum)).astype(BF16)
            st_ref[d, hh] = st * jnp.exp(last) + _dot_tn(vb, kd)


def _gla_scan(qkv, la, tri, ncb_tokens, dk, dv):
    nb, t, ch = qkv.shape
    d = la.shape[-1]
    tb = TB_SCAN
    nblk, ncb = t // tb, ncb_tokens // tb
    f2 = pl.BlockSpec((None, tb, ch), lambda b, j: (b, j, 0))
    b2 = pl.BlockSpec((None, tb, ch), lambda b, j: (b, _bwd_block(j, ncb, nblk), 0))
    f1 = pl.BlockSpec((None, tb, d), lambda b, j: (b, j, 0))
    b1 = pl.BlockSpec((None, tb, d), lambda b, j: (b, _bwd_block(j, ncb, nblk), 0))
    return pl.pallas_call(
        functools.partial(_gla_scan_kernel, dk=dk, dv=dv),
        grid=(nb, nblk),
        in_specs=[f2, f1, b2, b1, _full(tri)],
        out_specs=[f1, b1],
        out_shape=[jax.ShapeDtypeStruct((nb, t, d), F32)] * 2,
        scratch_shapes=[pltpu.VMEM((2, GLA_HEADS, dv, dk), F32)],
        compiler_params=_params(2),
        name="gla_scan",
    )(qkv, la, qkv, la, tri)


def _gla_out_kernel(x_ref, of_ref, ob_ref, g_ref, mod_ref, hn_ref, wo_ref, o_ref, *, dv):
    o = of_ref[...] + ob_ref[...]
    parts = []
    for hh in range(GLA_HEADS):
        oh = o[:, hh * dv:(hh + 1) * dv]
        ms = jnp.mean(oh * oh, axis=-1, keepdims=True)
        parts.append(oh * lax.rsqrt(ms + NORM_EPS) * hn_ref[...])
    on = jnp.concatenate(parts, axis=-1)
    z = (on * _silu(g_ref[...])).astype(BF16)
    o_ref[...] = x_ref[...] + mod_ref[2:3, :] * _dot(z, wo_ref[...])


def _gla_readout(xs, of, ob, g, mod, w, ncb_tokens, dv):
    nb, t, d = xs.shape
    tb = TB_SCAN
    tok = pl.BlockSpec((None, tb, d), lambda b, j: (b, j, 0))
    weights = [w["head_norm"], w["w_o"]]
    return pl.pallas_call(
        functools.partial(_gla_out_kernel, dv=dv),
        grid=(nb, t // tb),
        in_specs=[tok] * 4 + [_mod_spec(nb, d, ncb_tokens // tb)] + [_full(a) for a in weights],
        out_specs=tok,
        out_shape=jax.ShapeDtypeStruct((nb, t, d), F32),
        compiler_params=_params(2),
        name="gla_readout",
    )(xs, of, ob, g, mod, *weights)


def _top_values(s, k):
    vals = []
    cur = s
    for _ in range(k):
        m = jnp.max(cur, axis=0, keepdims=True)
        vals.append(m)
        cur = jnp.where(cur == m, -jnp.inf, cur)
    return vals


def _sorting_network(n):
    pairs = []
    p = 1
    while p < n:
        k = p
        while k >= 1:
            for j in range(k % p, n - k, 2 * k):
                for i in range(min(k, n - j - k)):
                    if (i + j) // (2 * p) == (i + j + k) // (2 * p):
                        pairs.append((i + j, i + j + k))
            k //= 2
        p *= 2
    return pairs


def _top_values_tiled(s, k):
    sub = 8
    n = s.shape[0] // sub
    lists = [s[g * sub:(g + 1) * sub] for g in range(n)]
    for a, b in _sorting_network(n):
        hi, lo = jnp.maximum(lists[a], lists[b]), jnp.minimum(lists[a], lists[b])
        lists[a], lists[b] = hi, lo
    lists.append(jnp.full_like(lists[0], -jnp.inf))
    vals = []
    for it in range(k):
        m = jnp.max(lists[0], axis=0, keepdims=True)
        vals.append(m)
        took = lists[0] == m
        for r in range(min(n, k - it - 1)):
            lists[r] = jnp.where(took, lists[r + 1], lists[r])
    return vals


def _peer_route_kernel(x_ref, mod_ref, ng_ref, wq_ref, key_hi_ref, key_lo_ref,
                       ht_o, c0_o, e0_o, r1_o, e1_o, q_hi_ref, q_lo_ref):
    h = _modnorm(x_ref[...], ng_ref[...], mod_ref[3:4, :], mod_ref[4:5, :])
    hb = h.T.astype(BF16)
    ht_o[...] = hb
    qt = _dot(wq_ref[...], hb)
    q_hi, q_lo = _split2(qt)
    q_hi_ref[...] = q_hi
    q_lo_ref[...] = q_lo
    nk = N_KEYS
    tb = x_ref.shape[0]

    def head(hd, carry):
        sv, sc = [], []
        for p in range(2):
            hp = hd * 2 + p
            rows = pl.ds(pl.multiple_of(hp * nk, nk), nk)
            s = _dot3(key_hi_ref[hp], key_lo_ref[hp], q_hi_ref[rows, :], q_lo_ref[rows, :])
            sc.append(s)
            sv.append(_top_values_tiled(s, PEER_TOPK + 1))
        k, half = PEER_TOPK, PEER_TOPK // 2
        sv0 = jnp.concatenate(sv[0][:k], axis=0)
        sv1 = jnp.concatenate(sv[1][:k], axis=0)
        edge = jnp.concatenate([sv[0][k] + sv[1][0], sv[0][0] + sv[1][k],
                                jnp.full((6, tb), -jnp.inf, F32)], axis=0)
        cand = jnp.concatenate([sv[0][0] + sv1]
                               + [sv[0][a] + sv1[:half] for a in range(1, half)]
                               + [sv0[half:] + sv[1][0], edge], axis=0)
        top = _top_values(cand, k + 1)
        z = top[0] * 0.0
        for cval in top[:k]:
            z = z + jnp.exp(cval - top[0])
        tau = 0.5 * (top[k - 1] + top[k])
        theta = tau - sc[0]
        c0 = jnp.zeros_like(theta)
        r1 = jnp.full_like(theta, float(k))
        for a in range(k):
            c0 = jnp.where(sv[1][a] >= theta, float(a + 1), c0)
            r1 = jnp.where(sc[1] == sv[1][a], float(a), r1)
        c0_o[hd] = c0
        r1_o[hd] = r1.astype(BF16)
        e0_o[hd] = jnp.exp(sc[0] - sv[0][0]) * (1.0 / z)
        e1_o[hd] = jnp.exp(sc[1] - sv[1][0]).astype(BF16)
        return carry

    per_trip = 4

    def head_group(i, carry):
        for hh in range(per_trip):
            head(per_trip * i + hh, carry)
        return carry

    lax.fori_loop(0, PEER_HEADS // per_trip, head_group, 0)


def _peer_route(xs, mod, ng, w, ncb_tokens):
    nb, t, d = xs.shape
    tb = TB_ROUTE
    nq = w["wq"].shape[0]
    weights = [ng, w["wq"], w["key_hi"], w["key_lo"]]
    sspec = pl.BlockSpec((None, PEER_HEADS, N_KEYS, tb), lambda b, j: (b, 0, 0, j))
    return pl.pallas_call(
        _peer_route_kernel,
        grid=(nb, t // tb),
        in_specs=[pl.BlockSpec((None, tb, d), lambda b, j: (b, j, 0)),
                  _mod_spec(nb, d, ncb_tokens // tb)] + [_full(a) for a in weights],
        out_specs=[pl.BlockSpec((None, d, tb), lambda b, j: (b, 0, j)), sspec, sspec, sspec, sspec],
        out_shape=[jax.ShapeDtypeStruct((nb, d, t), BF16),
                   jax.ShapeDtypeStruct((nb, PEER_HEADS, N_KEYS, t), F32),
                   jax.ShapeDtypeStruct((nb, PEER_HEADS, N_KEYS, t), F32),
                   jax.ShapeDtypeStruct((nb, PEER_HEADS, N_KEYS, t), BF16),
                   jax.ShapeDtypeStruct((nb, PEER_HEADS, N_KEYS, t), BF16)],
        scratch_shapes=[pltpu.VMEM((nq, tb), BF16), pltpu.VMEM((nq, tb), BF16)],
        compiler_params=_params(2),
        name="peer_route",
    )(xs, mod, *weights)


def _peer_dense_kernel(x_ref, ht_ref, c0_ref, e0_ref, r1_ref, e1_ref, modl_ref, modc_ref, u_ref, vt_ref, o_ref,
                       acc_ref, cp_ref, ep_ref, act_ref, wa0_ref, wa1_ref,
                       *, nctx, nec, ntile, nsteps):
    g = pl.program_id(0)
    ne = u_ref.shape[0]
    d, tb = acc_ref.shape
    nk = N_KEYS
    nq = DENSE_QUARTERS
    qe, qd = ne // nq, d // nq
    qi = qe // nk
    g_b = jnp.clip(g - 1, 0, nsteps - 1)
    g_c = jnp.clip(g - 2, 0, nsteps - 1)
    e_b = g_b % nec
    e_c = g_c % nec
    tok0 = ((g_c // nec) % ntile) * tb

    @pl.when(g == 0)
    def _():
        for ref in (act_ref, wa0_ref, wa1_ref, cp_ref, ep_ref):
            ref[...] = jnp.zeros_like(ref)

    @pl.when(e_b == 0)
    def _():
        for hd in range(PEER_HEADS):
            c0 = c0_ref[hd]
            e0 = e0_ref[hd]
            for p in range(nk // qi):
                cp_ref[hd, p, 0:qi, :] = c0[p * qi:(p + 1) * qi]
                ep_ref[hd, p, 0:qi, :] = e0[p * qi:(p + 1) * qi]

    @pl.when(e_c == 0)
    def _():
        acc_ref[...] = jnp.zeros_like(acc_ref)

    jr = 32

    new, old = g % 2, (g + 1) % 2

    def quarter(wa_new, wa_old, q, carry):
        r0 = pl.multiple_of(q * qe, qe)
        d0 = pl.multiple_of(q * qd, qd)
        pair = e_b * nq + q
        nj = nk // jr
        ka, kc = d // nj, ne // nj
        act_ref[new, pl.ds(r0, qe), :] = jnp.zeros((qe, tb), F32)
        half = jr // 2
        cb, eb = [], []
        for hd in range(PEER_HEADS):
            ct = cp_ref[hd, pair]
            et = ep_ref[hd, pair]
            for ii in range(qi):
                c16 = jnp.broadcast_to(ct[ii:ii + 1], (half, tb)).astype(BF16)
                e16 = jnp.broadcast_to(et[ii:ii + 1], (half, tb)).astype(BF16)
                cb.append(jnp.concatenate([c16, c16], axis=0))
                eb.append(jnp.concatenate([e16, e16], axis=0))
        for jq in range(nj):
            act_ref[new, pl.ds(r0, qe), :] += _dot(u_ref[pl.ds(r0, qe), jq * ka:(jq + 1) * ka],
                                                   ht_ref[jq * ka:(jq + 1) * ka, :])
            acc_ref[pl.ds(d0, qd), :] += _dot(vt_ref[pl.ds(d0, qd), jq * kc:(jq + 1) * kc],
                                              wa_old[jq * kc:(jq + 1) * kc, :])
            rows = slice(jq * jr, (jq + 1) * jr)
            accs = [None] * qi
            for hd in range(PEER_HEADS):
                r1 = r1_ref[hd, rows, :]
                e1 = e1_ref[hd, rows, :]
                for ii in range(qi):
                    term = jnp.where(r1 < cb[hd * qi + ii], e1, jnp.zeros_like(e1)) * eb[hd * qi + ii]
                    accs[ii] = term if accs[ii] is None else accs[ii] + term
            for ii in range(qi):
                er = pl.ds(pl.multiple_of(r0 + ii * nk + jq * jr, jr), jr)
                a = act_ref[old, er, :]
                act = 0.5 * a * (1.0 + lax.erf(a * float(1.0 / np.sqrt(2.0))))
                wa_new[er, :] = (accs[ii].astype(F32) * act).astype(BF16)
        return carry

    @pl.when(g % 2 == 0)
    def _():
        lax.fori_loop(0, nq, functools.partial(quarter, wa0_ref, wa1_ref), 0)

    @pl.when(g % 2 == 1)
    def _():
        lax.fori_loop(0, nq, functools.partial(quarter, wa1_ref, wa0_ref), 0)

    @pl.when((e_c == nec - 1) & (g >= 2))
    def _():
        tok = tok0 + lax.broadcasted_iota(jnp.int32, (tb, 1), 0)
        gate = jnp.where(tok < nctx, modc_ref[5:6, :], modl_ref[5:6, :])
        o_ref[...] = x_ref[...] + gate * acc_ref[...].T


def _peer_dense(xs, ht, c0, e0, r1, e1, mod, w, ncb_tokens):
    nb, t, d = xs.shape
    tb = TB_DENSE
    ne = EC_DENSE
    n_exp = w["u"].shape[0]
    assert t % tb == 0 and n_exp % ne == 0 and ne % (DENSE_QUARTERS * N_KEYS) == 0
    ntile, nec = t // tb, n_exp // ne
    nsteps = nb * ntile * nec
    pairs = N_KEYS // (ne // DENSE_QUARTERS // N_KEYS)

    def at(lag):
        def f(g):
            s = jnp.clip(g - lag, 0, nsteps - 1)
            return s // (ntile * nec), (s // nec) % ntile, s % nec
        return f

    sa, sb, sc = at(0), at(1), at(2)
    sspec = pl.BlockSpec((None, PEER_HEADS, N_KEYS, tb), lambda g: (sb(g)[0], 0, 0, sb(g)[1]))
    return pl.pallas_call(
        functools.partial(_peer_dense_kernel, nctx=ncb_tokens, nec=nec, ntile=ntile, nsteps=nsteps),
        grid=(nsteps + 2,),
        in_specs=[pl.BlockSpec((None, tb, d), lambda g: (sc(g)[0], sc(g)[1], 0)),
                  pl.BlockSpec((None, d, tb), lambda g: (sa(g)[0], 0, sa(g)[1])),
                  sspec, sspec, sspec, sspec,
                  pl.BlockSpec((None, 6, d), lambda g: (sc(g)[0], 0, 0)),
                  pl.BlockSpec((None, 6, d), lambda g: (nb, 0, 0)),
                  pl.BlockSpec((ne, d), lambda g: (sa(g)[2], 0)),
                  pl.BlockSpec((d, ne), lambda g: (0, sc(g)[2]))],
        out_specs=pl.BlockSpec((None, tb, d), lambda g: (sc(g)[0], sc(g)[1], 0)),
        out_shape=jax.ShapeDtypeStruct((nb, t, d), F32),
        scratch_shapes=[pltpu.VMEM((d, tb), F32),
                        pltpu.VMEM((PEER_HEADS, pairs, 8, tb), F32),
                        pltpu.VMEM((PEER_HEADS, pairs, 8, tb), F32),
                        pltpu.VMEM((2, ne, tb), F32),
                        pltpu.VMEM((ne, tb), BF16), pltpu.VMEM((ne, tb), BF16)],
        compiler_params=_params(1),
        name="peer_dense",
    )(xs, ht, c0, e0, r1, e1, mod, mod, w["u"], w["vt"])


def _final_norm_kernel(x_ref, g_ref, o_ref):
    x = x_ref[...]
    ms = jnp.mean(x * x, axis=-1, keepdims=True)
    o_ref[...] = x * lax.rsqrt(ms + NORM_EPS) * g_ref[...]


def _final_norm(xs, g, ncb_tokens):
    nb, t, d = xs.shape
    tb = TB_SCAN
    ncb = ncb_tokens // tb
    return pl.pallas_call(
        _final_norm_kernel,
        grid=(nb, t // tb - ncb),
        in_specs=[pl.BlockSpec((None, tb, d), lambda b, j: (b, j + ncb, 0)), _full(g)],
        out_specs=pl.BlockSpec((None, tb, d), lambda b, j: (b, j, 0)),
        out_shape=jax.ShapeDtypeStruct((nb, t - ncb_tokens, d), F32),
        compiler_params=_params(2),
        name="final_norm",
    )(xs, g)


def _row(a):
    return a.reshape(1, -1)


def _pad_dir(w2):
    z = jnp.zeros_like(w2[0])
    return jnp.stack([jnp.concatenate([w2[0], z], axis=0), jnp.concatenate([z, w2[1]], axis=0)]).astype(BF16)


def kernel(x, c, ctx, c_ctx, ada_w, ada_b, norm_mix, norm_ffn, rw_mix, rw_w_rkv, rw_w0, rw_w1, rw_w2, rw_a0, rw_a1, rw_a2, rw_g1, rw_g2, rw_k_k, rw_k_a, rw_r_k, rw_ln_w, rw_ln_b, rw_w_o, gla_w_in, gla_conv, gla_gk_up, gla_gk_b, gla_head_norm, gla_w_o, peer_wq, peer_keys, peer_u, peer_v, final_norm):
    nb, seq, d = x.shape
    nctx = ctx.shape[1]
    depth = ada_w.shape[0]
    assert nctx == TB_SCAN and seq % TB_SCAN == 0 and d % LANES == 0

    xs = jnp.concatenate([ctx, x], axis=1)
    cc = jnp.zeros((16, d), F32).at[:nb].set(c).at[nb].set(c_ctx)
    mods = _modulation(cc, ada_w, ada_b)[:, :nb + 1].reshape(depth, nb + 1, 6, d)

    bd_head = _block_diag_ones(RWKV_HEAD)
    ci = np.arange(GLA_CHUNK)
    tri = jnp.asarray(np.stack([ci[None, :] <= ci[:, None], ci[None, :] >= ci[:, None]]), dtype=BF16)

    for i in range(depth):
        mod = mods[i]
        j = i // 2
        if i % 2 == 0:
            w = dict(
                mix=rw_mix[j], w_rkv=rw_w_rkv[j].astype(BF16),
                w1=jnp.concatenate([rw_w1[j, 0], rw_w1[j, 1]], axis=1).astype(BF16),
                a1=jnp.concatenate([rw_a1[j, 0], rw_a1[j, 1]], axis=1).astype(BF16),
                g1=rw_g1[j].astype(BF16), w2=_pad_dir(rw_w2[j]), a2=_pad_dir(rw_a2[j]),
                g2=rw_g2[j].astype(BF16), w0=rw_w0[j], a0=rw_a0[j],
                k_k=_row(rw_k_k[j]), k_a=_row(rw_k_a[j]), r_k=_row(rw_r_k[j]),
                ln_w=_row(rw_ln_w[j]), ln_b=_row(rw_ln_b[j]), w_o=rw_w_o[j].astype(BF16), bd=bd_head)
            P = _rw_project(xs, mod, _row(norm_mix[i]), w, nctx)
            yf, yb = _rw_scan(P, bd_head, nctx)
            xs = _rw_readout(xs, yf, yb, P[3], P[4], mod, w, nctx)
        else:
            w_in = gla_w_in[j]
            qk_width = gla_gk_up.shape[-1] * 2
            conv_ch = qk_width + d
            dk = gla_gk_up.shape[-1] // GLA_HEADS
            dv = d // GLA_HEADS
            rank = gla_gk_up.shape[2]
            lr0 = conv_ch + d
            w_lr = jnp.zeros((d, LANES), F32).at[:, :2 * rank].set(w_in[:, lr0:lr0 + 2 * rank])
            up = jnp.zeros((LANES, d), F32)
            up = up.at[:rank, :qk_width // 2].set(gla_gk_up[j, 0]).at[rank:2 * rank, qk_width // 2:].set(gla_gk_up[j, 1])
            w = dict(w_qkv=w_in[:, :conv_ch].astype(BF16), w_g=w_in[:, conv_ch:lr0].astype(BF16),
                     w_lr=w_lr.astype(BF16), up=up.astype(BF16),
                     gk_b=jnp.concatenate([gla_gk_b[j, 0], gla_gk_b[j, 1]]).reshape(1, d),
                     head_norm=_row(gla_head_norm[j]), w_o=gla_w_o[j].astype(BF16))
            z, g, la = _gla_project(xs, mod, _row(norm_mix[i]), w, nctx)
            qkv = _gla_conv(z, gla_conv[j].reshape(9, conv_ch), nctx, qk_width, float(dk) ** -0.5)
            of, ob = _gla_scan(qkv, la, tri, nctx, dk, dv)
            xs = _gla_readout(xs, of, ob, g, mod, w, nctx, dv)

        keys = peer_keys[i].reshape(PEER_HEADS * 2, N_KEYS, -1)
        key_hi = keys.astype(BF16)
        pw = dict(wq=peer_wq[i].T.astype(BF16),
                  key_hi=key_hi, key_lo=(keys - key_hi.astype(F32)).astype(BF16),
                  u=peer_u[i].astype(BF16), vt=peer_v[i].T.astype(BF16))
        ht, c0, e0, r1, e1 = _peer_route(xs, mod, _row(norm_ffn[i]), pw, nctx)
        xs = _peer_dense(xs, ht, c0, e0, r1, e1, mod, pw, nctx)

    return _final_norm(xs, _row(final_norm), nctx)
```

```python
import functools

import numpy as np
import jax
import jax.numpy as jnp
from jax import lax
from jax.experimental import pallas as pl
from jax.experimental.pallas import tpu as pltpu

F32 = jnp.float32
BF16 = jnp.bfloat16

NORM_EPS = 1e-6
GRID_W = 64
RWKV_HEAD = 64
RWKV_GN_EPS = 64e-5
DECAY_LORA = 64
GLA_HEADS = 4
GLA_GATE_RANK = 16
GLA_GATE_NORM = 16.0
GLA_CHUNK = 64
PEER_HEADS = 8
N_KEYS = 128
PEER_TOPK = 16

LANES = 128
VMEM_LIMIT = 56 * 1024 * 1024

TB_PROJ = 128
TB_SCAN = 256
TB_ROUTE = 256
TB_DENSE = 256
EC_DENSE = 2048
DENSE_QUARTERS = 4


def _params(n_axes):
    return pltpu.CompilerParams(
        dimension_semantics=("arbitrary",) * n_axes,
        vmem_limit_bytes=VMEM_LIMIT)


def _full(a):
    nd = a.ndim
    return pl.BlockSpec(a.shape, lambda *_: (0,) * nd)


def _sigmoid(x):
    return 1.0 / (1.0 + jnp.exp(-x))


def _silu(x):
    return x * _sigmoid(x)


def _log_sigmoid(x):
    return jnp.minimum(x, 0.0) - jnp.log(1.0 + jnp.exp(-jnp.abs(x)))


def _modnorm(x, g, shift, scale):
    ms = jnp.mean(x * x, axis=-1, keepdims=True)
    return x * lax.rsqrt(ms + NORM_EPS) * (g * (1.0 + scale)) + shift


def _split2(x):
    hi = x.astype(BF16)
    lo = (x - hi.astype(F32)).astype(BF16)
    return hi, lo


def _split3(x):
    x1 = x.astype(BF16)
    r1 = x - x1.astype(F32)
    x2 = r1.astype(BF16)
    x3 = (r1 - x2.astype(F32)).astype(BF16)
    return x1, x2, x3


def _dot(a, b):
    return jnp.dot(a, b, preferred_element_type=F32)


def _dot_nt(a, b):
    return lax.dot_general(a, b, (((1,), (1,)), ((), ())), preferred_element_type=F32)


def _dot_tn(a, b):
    return lax.dot_general(a, b, (((0,), (0,)), ((), ())), preferred_element_type=F32)


def _dot3(a_hi, a_lo, b_hi, b_lo):
    return _dot(a_hi, b_hi) + _dot(a_hi, b_lo) + _dot(a_lo, b_hi)


def _seg_sum(x, bd):
    outs = []
    for c in range(x.shape[-1] // LANES):
        hi, lo = _split2(x[:, c * LANES:(c + 1) * LANES])
        outs.append(_dot(hi, bd) + _dot(lo, bd))
    return jnp.concatenate(outs, axis=-1)


def _block_diag_ones(group):
    i = np.arange(LANES)
    return jnp.asarray((i[:, None] // group) == (i[None, :] // group), dtype=BF16)


def _mod_kernel(c_ref, w_ref, b_ref, o_ref):
    a = _silu(c_ref[...])
    o_ref[...] = jnp.dot(a, w_ref[...], preferred_element_type=F32,
                         precision=lax.Precision.HIGHEST) + b_ref[...]


def _modulation(cc, ada_w, ada_b):
    depth, d, n = ada_w.shape
    nt = 768
    return pl.pallas_call(
        _mod_kernel,
        grid=(depth, n // nt),
        in_specs=[pl.BlockSpec(cc.shape, lambda i, k: (0, 0)),
                  pl.BlockSpec((None, d, nt), lambda i, k: (i, 0, k)),
                  pl.BlockSpec((None, 1, nt), lambda i, k: (i, 0, k))],
        out_specs=pl.BlockSpec((None, cc.shape[0], nt), lambda i, k: (i, 0, k)),
        out_shape=jax.ShapeDtypeStruct((depth, cc.shape[0], n), F32),
        compiler_params=_params(2),
        name="modulation",
    )(cc, ada_w, ada_b.reshape(depth, 1, n))


def _mod_spec(nb, d, ncb):
    return pl.BlockSpec((None, 6, d), lambda b, j: (jnp.where(j < ncb, nb, b), 0, 0))


def _rw_proj_kernel(x_ref, xp_ref, xn_ref, mod_ref, ng_ref, mix_ref, wrkv_ref,
                    w1_ref, a1_ref, g1_ref, w2_ref, a2_ref, g2_ref, w0_ref, a0_ref,
                    kk_w_ref, ka_ref, rk_ref, bd_ref,
                    r_o, v_o, kk_o, g_o, bon_o, w0_o, k0_o, b0_o, w1_o, k1_o, b1_o,
                    *, ncb, nblk):
    j = pl.program_id(1)
    tb = x_ref.shape[0]
    shift = mod_ref[0:1, :]
    scale = mod_ref[1:2, :]
    g = ng_ref[...]
    h = _modnorm(x_ref[...], g, shift, scale)
    hp = _modnorm(xp_ref[7:8, :], g, shift, scale)
    hn = _modnorm(xn_ref[0:1, :], g, shift, scale)
    hp = jnp.where((j != 0) & (j != ncb), hp, 0.0)
    hn = jnp.where((j != ncb - 1) & (j != nblk - 1), hn, 0.0)
    row = lax.broadcasted_iota(jnp.int32, (tb, 1), 0)
    prev = jnp.where(row == 0, hp, pltpu.roll(h, 1, 0))
    nxt = jnp.where(row == tb - 1, hn, pltpu.roll(h, tb - 1, 0))
    xx = 0.5 * (prev + nxt) - h
    xr, xw, xk, xv, xa, xg = (h + xx * mix_ref[i:i + 1, :] for i in range(6))

    r = _dot(xr.astype(BF16), wrkv_ref[0])
    k = _dot(xk.astype(BF16), wrkv_ref[1])
    v = _dot(xv.astype(BF16), wrkv_ref[2])
    gate = _dot(_sigmoid(_dot(xg.astype(BF16), g1_ref[...])).astype(BF16), g2_ref[...])
    tw = jnp.tanh(_dot(xw.astype(BF16), w1_ref[...])).astype(BF16)
    ta = _dot(xa.astype(BF16), a1_ref[...]).astype(BF16)

    bd = bd_ref[...]
    kk = k * kk_w_ref[...]
    kk = kk / jnp.maximum(jnp.sqrt(_seg_sum(kk * kk, bd)), 1e-12)

    r_o[...] = r
    v_o[...] = v
    kk_o[...] = kk
    g_o[...] = gate
    ksum = None
    for d, (w_o, k_o, b_o) in enumerate(((w0_o, k0_o, b0_o), (w1_o, k1_o, b1_o))):
        pre = w0_ref[d:d + 1, :] + _dot(tw, w2_ref[d])
        w_o[...] = jnp.exp(-_sigmoid(pre) * float(np.exp(-0.5)))
        a = _sigmoid(a0_ref[d:d + 1, :] + _dot(ta, a2_ref[d]))
        kd = k * (1.0 + (a - 1.0) * ka_ref[...])
        k_o[...] = kd
        b_o[...] = kk * a
        ksum = kd if ksum is None else ksum + kd
    bon_o[...] = _seg_sum(r * ksum * rk_ref[...], bd) * v


def _rw_project(xs, mod, ng, w, ncb_tokens):
    nb, t, d = xs.shape
    tb = TB_PROJ
    nblk, ncb = t // tb, ncb_tokens // tb
    sl = tb // 8
    weights = [ng, w["mix"], w["w_rkv"], w["w1"], w["a1"], w["g1"], w["w2"], w["a2"], w["g2"],
               w["w0"], w["a0"], w["k_k"], w["k_a"], w["r_k"], w["bd"]]
    tok = pl.BlockSpec((None, tb, d), lambda b, j: (b, j, 0))
    return pl.pallas_call(
        functools.partial(_rw_proj_kernel, ncb=ncb, nblk=nblk),
        grid=(nb, nblk),
        in_specs=[tok,
                  pl.BlockSpec((None, 8, d), lambda b, j: (b, jnp.maximum(j * sl - 1, 0), 0)),
                  pl.BlockSpec((None, 8, d), lambda b, j: (b, jnp.minimum((j + 1) * sl, t // 8 - 1), 0)),
                  _mod_spec(nb, d, ncb)] + [_full(a) for a in weights],
        out_specs=[tok] * 11,
        out_shape=[jax.ShapeDtypeStruct((nb, t, d), F32)] * 11,
        compiler_params=_params(2),
        name="rwkv_project",
    )(xs, xs, xs, mod, *weights)


def _rw_scan_kernel(rf, wf, kf, vf, kkf, bf, rb, wb, kb, vb, kkb, bb, bd_ref,
                    yf_o, yb_o, s_ref):
    tb = rf.shape[0]
    nch = rf.shape[1] // LANES

    @pl.when(pl.program_id(1) == 0)
    def _():
        s_ref[...] = jnp.zeros_like(s_ref)

    bd = bd_ref[...]
    hw = RWKV_HEAD
    vi = lax.broadcasted_iota(jnp.int32, (hw, LANES), 0)
    li = lax.broadcasted_iota(jnp.int32, (hw, LANES), 1)
    diag_bf = ((li & (hw - 1)) == vi).astype(F32).astype(BF16)
    lane = lax.broadcasted_iota(jnp.int32, (1, LANES), 1)
    first_head = lane < hw
    head_mask = (first_head.astype(F32), 1.0 - first_head.astype(F32))
    sub = 32

    gch = 4
    groups = [(d, list(range(c0, c0 + gch))) for c0 in range(0, nch, gch) for d in (0, 1)]
    fwd = (rf, wf, kf, vf, kkf, bf)
    bwd = (rb, wb, kb, vb, kkb, bb)

    def rows_of(s, i):
        return s[i * hw:(i + 1) * hw]

    def body(g, carry):
        base = (pl.multiple_of(g * sub, sub), pl.multiple_of(tb - sub - g * sub, sub))
        lanes = [slice(c * LANES, (c + 1) * LANES) for c in range(nch)]
        tiles = [[[ref[pl.ds(base[d], sub), ls] for ref in refs] for ls in lanes]
                 for d, refs in enumerate((fwd, bwd))]

        def rows(d, c, n):
            m = n if d == 0 else sub - 1 - n
            return [t[m:m + 1] for t in tiles[d][c]]

        def tile_bf(row):
            t = jnp.broadcast_to(row, (2 * 8, LANES)).astype(BF16)
            return jnp.concatenate([t] * (hw // 16), axis=0)

        def start(gi, sb, n):
            d, cs = groups[gi]
            vcol = _dot(jnp.concatenate([tile_bf(rows(d, c, n)[3]) * diag_bf for c in cs], axis=0), bd)
            sa = _dot(sb * jnp.concatenate([tile_bf(rows(d, c, n)[4]) for c in cs], axis=0), bd)
            return vcol, sa

        state = [[s_ref[d, c] for c in cs] for d, cs in groups]
        pend = [start(gi, jnp.concatenate(state[gi], axis=0).astype(BF16), 0) for gi in range(len(groups))]
        ys = [[[] for _ in range(nch)] for _ in range(2)]
        for n in range(sub):
            for gi, (d, cs) in enumerate(groups):
                vcol, sa = pend[gi]
                s = []
                for i, c in enumerate(cs):
                    r, w, k, v, kk, b = rows(d, c, n)
                    s.append(state[gi][i] * w - rows_of(sa, i) * b + rows_of(vcol, i) * k)
                state[gi] = s
                sb = jnp.concatenate(s, axis=0).astype(BF16)
                if n + 1 < sub:
                    pend[gi] = start(gi, sb, n + 1)
                lhs = jnp.concatenate([rows(d, c, n)[0] * head_mask[h] for c in cs for h in range(2)], axis=0)
                out = _dot_nt(lhs.astype(BF16), sb)
                for m in range(gch // 2):
                    x = out[:, m * LANES:(m + 1) * LANES]
                    xr = pltpu.roll(x, hw, 1)
                    r0 = 4 * m
                    ys[d][cs[2 * m]].append(jnp.where(first_head, x[r0:r0 + 1], xr[r0 + 1:r0 + 2]))
                    ys[d][cs[2 * m + 1]].append(jnp.where(first_head, xr[r0 + 2:r0 + 3], x[r0 + 3:r0 + 4]))
        for gi, (d, cs) in enumerate(groups):
            for i, c in enumerate(cs):
                s_ref[d, c] = state[gi][i]
        for c, ls in enumerate(lanes):
            yf_o[pl.ds(base[0], sub), ls] = jnp.concatenate(ys[0][c], axis=0)
            yb_o[pl.ds(base[1], sub), ls] = jnp.concatenate(ys[1][c][::-1], axis=0)
        return carry

    lax.fori_loop(0, tb // sub, body, 0)


def _bwd_block(j, ncb, nblk):
    return jnp.where(j < ncb, ncb - 1 - j, nblk - 1 - (j - ncb))


def _rw_scan(P, bd, ncb_tokens):
    r, v, kk, _, _, w0, k0, b0, w1, k1, b1 = P
    nb, t, d = r.shape
    tb = TB_SCAN
    nblk, ncb = t // tb, ncb_tokens // tb
    fs = pl.BlockSpec((None, tb, d), lambda b, j: (b, j, 0))
    bs = pl.BlockSpec((None, tb, d), lambda b, j: (b, _bwd_block(j, ncb, nblk), 0))
    return pl.pallas_call(
        _rw_scan_kernel,
        grid=(nb, nblk),
        in_specs=[fs] * 6 + [bs] * 6 + [_full(bd)],
        out_specs=[fs, bs],
        out_shape=[jax.ShapeDtypeStruct((nb, t, d), F32)] * 2,
        scratch_shapes=[pltpu.VMEM((2, d // LANES, RWKV_HEAD, LANES), F32)],
        compiler_params=_params(2),
        name="rwkv_scan",
    )(r, w0, k0, v, kk, b0, r, w1, k1, v, kk, b1, bd)


def _rw_out_kernel(x_ref, yf_ref, yb_ref, g_ref, bon_ref, mod_ref, lnw_ref, lnb_ref, wo_ref, bd_ref, o_ref):
    bd = bd_ref[...]
    y = yf_ref[...] + yb_ref[...]
    mu = _seg_sum(y, bd) * (1.0 / RWKV_HEAD)
    yc = y - mu
    var = _seg_sum(yc * yc, bd) * (1.0 / RWKV_HEAD)
    yn = yc * lax.rsqrt(var + RWKV_GN_EPS) * lnw_ref[...] + lnb_ref[...]
    z = ((yn + bon_ref[...]) * g_ref[...]).astype(BF16)
    o_ref[...] = x_ref[...] + mod_ref[2:3, :] * _dot(z, wo_ref[...])


def _rw_readout(xs, yf, yb, gate, bon, mod, w, ncb_tokens):
    nb, t, d = xs.shape
    tb = TB_SCAN
    tok = pl.BlockSpec((None, tb, d), lambda b, j: (b, j, 0))
    weights = [w["ln_w"], w["ln_b"], w["w_o"], w["bd"]]
    return pl.pallas_call(
        _rw_out_kernel,
        grid=(nb, t // tb),
        in_specs=[tok] * 5 + [_mod_spec(nb, d, ncb_tokens // tb)] + [_full(a) for a in weights],
        out_specs=tok,
        out_shape=jax.ShapeDtypeStruct((nb, t, d), F32),
        compiler_params=_params(2),
        name="rwkv_readout",
    )(xs, yf, yb, gate, bon, mod, *weights)


def _gla_proj_kernel(x_ref, mod_ref, ng_ref, wqkv_ref, wg_ref, wlr_ref, up_ref, gkb_ref,
                     z_o, g_o, la_o):
    h = _modnorm(x_ref[...], ng_ref[...], mod_ref[0:1, :], mod_ref[1:2, :])
    hb = h.astype(BF16)
    z_o[...] = _dot(hb, wqkv_ref[...])
    g_o[...] = _dot(hb, wg_ref[...])
    lr = _dot(hb, wlr_ref[...]).astype(BF16)
    la_o[...] = _log_sigmoid(_dot(lr, up_ref[...]) + gkb_ref[...]) * (1.0 / GLA_GATE_NORM)


def _gla_project(xs, mod, ng, w, ncb_tokens):
    nb, t, d = xs.shape
    tb = TB_SCAN
    weights = [ng, w["w_qkv"], w["w_g"], w["w_lr"], w["up"], w["gk_b"]]
    tok = pl.BlockSpec((None, tb, d), lambda b, j: (b, j, 0))
    tok2 = pl.BlockSpec((None, tb, 2 * d), lambda b, j: (b, j, 0))
    return pl.pallas_call(
        _gla_proj_kernel,
        grid=(nb, t // tb),
        in_specs=[tok, _mod_spec(nb, d, ncb_tokens // tb)] + [_full(a) for a in weights],
        out_specs=[tok2, tok, tok],
        out_shape=[jax.ShapeDtypeStruct((nb, t, 2 * d), F32),
                   jax.ShapeDtypeStruct((nb, t, d), F32),
                   jax.ShapeDtypeStruct((nb, t, d), F32)],
        compiler_params=_params(2),
        name="gla_project",
    )(xs, mod, *weights)


def _gla_conv_kernel(zc_ref, zp_ref, zn_ref, cw_ref, o_ref, *, nblk, qk_width, q_scale):
    j = pl.program_id(1)
    tb, ch = zc_ref.shape
    is_ctx = j == 0
    up_ok = jnp.where(j > 1, 1.0, 0.0)
    dn_ok = jnp.where((j > 0) & (j < nblk - 1), 1.0, 0.0)
    vert = jnp.where(is_ctx, 0.0, 1.0)
    ext = tb + 2 * GRID_W
    pos = lax.broadcasted_iota(jnp.int32, (ext, 1), 0) - GRID_W
    col = jnp.where(is_ctx, pos, pos & (GRID_W - 1))
    no_left = col == 0
    no_right = col == jnp.where(is_ctx, tb - 1, GRID_W - 1)
    cc = 256
    for c in range(ch // cc):
        cs = slice(c * cc, (c + 1) * cc)
        e = jnp.concatenate([zp_ref[:, cs] * up_ok, zc_ref[:, cs], zn_ref[:, cs] * dn_ok], axis=0)
        em = jnp.where(no_left, 0.0, pltpu.roll(e, 1, 0))
        ep = jnp.where(no_right, 0.0, pltpu.roll(e, ext - 1, 0))
        acc = None
        for dy in range(3):
            lo = dy * GRID_W
            for dx, src in enumerate((em, e, ep)):
                wt = cw_ref[dy * 3 + dx:dy * 3 + dx + 1, cs]
                if dy != 1:
                    wt = wt * vert
                term = src[lo:lo + tb] * wt
                acc = term if acc is None else acc + term
        y = _silu(acc)
        if (c + 1) * cc <= qk_width // 2:
            y = y * q_scale
        o_ref[:, cs] = y


def _gla_conv(z, cw, ncb_tokens, qk_width, q_scale):
    nb, t, ch = z.shape
    tb = TB_SCAN
    assert ncb_tokens == tb and tb % GRID_W == 0
    sl = tb // GRID_W
    nblk = t // tb
    return pl.pallas_call(
        functools.partial(_gla_conv_kernel, nblk=nblk, qk_width=qk_width, q_scale=q_scale),
        grid=(nb, nblk),
        in_specs=[pl.BlockSpec((None, tb, ch), lambda b, j: (b, j, 0)),
                  pl.BlockSpec((None, GRID_W, ch), lambda b, j: (b, jnp.maximum(j * sl - 1, 0), 0)),
                  pl.BlockSpec((None, GRID_W, ch),
                               lambda b, j: (b, jnp.minimum((j + 1) * sl, t // GRID_W - 1), 0)),
                  _full(cw)],
        out_specs=pl.BlockSpec((None, tb, ch), lambda b, j: (b, j, 0)),
        out_shape=jax.ShapeDtypeStruct((nb, t, ch), F32),
        compiler_params=_params(2),
        name="gla_conv",
    )(z, z, z, cw)


def _gla_scan_kernel(qf_ref, laf_ref, qb_ref, lab_ref, tri_ref, of_o, ob_o, st_ref, *, dk, dv):
    tb = qf_ref.shape[0]
    c = GLA_CHUNK
    nh = GLA_HEADS
    qkw = nh * dk

    @pl.when(pl.program_id(1) == 0)
    def _():
        st_ref[...] = jnp.zeros_like(st_ref)

    dirs = ((qf_ref, laf_ref, of_o), (qb_ref, lab_ref, ob_o))
    nchunk = tb // c
    for p in range(nchunk):
        units = [(d, hh) for d in range(2) for hh in range(nh)]

        def load(d, hh):
            q_ref, la_ref, _ = dirs[d]
            n = p if d == 0 else nchunk - 1 - p
            rows = slice(n * c, (n + 1) * c)
            return (rows, q_ref[rows, hh * dk:(hh + 1) * dk], q_ref[rows, qkw + hh * dk:qkw + (hh + 1) * dk],
                    q_ref[rows, 2 * qkw + hh * dv:2 * qkw + (hh + 1) * dv].astype(BF16),
                    la_ref[rows, d * qkw + hh * dk:d * qkw + (hh + 1) * dk])

        data = [load(d, hh) for d, hh in units]
        bcums = []
        for (d, hh), (_, _, _, _, la) in zip(units, data):
            tri = tri_ref[d]
            l1, l2, l3 = _split3(la)
            bcums.append(_dot(tri, l1) + _dot(tri, l2) + _dot(tri, l3))
        intra = []
        for (d, hh), (_, q, k, _, _), bcum in zip(units, data, bcums):
            ref_row = c // 2 if d == 0 else c - 1 - c // 2
            ref = bcum[ref_row:ref_row + 1]
            a = _dot_nt((q * jnp.exp(bcum - ref)).astype(BF16), (k * jnp.exp(ref - bcum)).astype(BF16))
            intra.append(jnp.where(tri_ref[d] > 0, a, 0.0).astype(BF16))
        for (d, hh), (rows, q, k, vb, _), bcum, a in zip(units, data, bcums, intra):
            last_row = c - 1 if d == 0 else 0
            last = bcum[last_row:last_row + 1]
            st = st_ref[d, hh]
            o = _dot(a, vb) + _dot_nt((q * jnp.exp(bcum)).astype(BF16), st.astype(BF16))
            dirs[d][2][rows, hh * dv:(hh + 1) * dv] = o
            kd = (k * jnp.exp(last - bcum)).astype(BF16)
            st_ref[d, hh] = st * jnp.exp(last) + _dot_tn(vb, kd)


def _gla_scan(qkv, la, tri, ncb_tokens, dk, dv):
    nb, t, ch = qkv.shape
    d = la.shape[-1]
    tb = TB_SCAN
    nblk, ncb = t // tb, ncb_tokens // tb
    f2 = pl.BlockSpec((None, tb, ch), lambda b, j: (b, j, 0))
    b2 = pl.BlockSpec((None, tb, ch), lambda b, j: (b, _bwd_block(j, ncb, nblk), 0))
    f1 = pl.BlockSpec((None, tb, d), lambda b, j: (b, j, 0))
    b1 = pl.BlockSpec((None, tb, d), lambda b, j: (b, _bwd_block(j, ncb, nblk), 0))
    return pl.pallas_call(
        functools.partial(_gla_scan_kernel, dk=dk, dv=dv),
        grid=(nb, nblk),
        in_specs=[f2, f1, b2, b1, _full(tri)],
        out_specs=[f1, b1],
        out_shape=[jax.ShapeDtypeStruct((nb, t, d), F32)] * 2,
        scratch_shapes=[pltpu.VMEM((2, GLA_HEADS, dv, dk), F32)],
        compiler_params=_params(2),
        name="gla_scan",
    )(qkv, la, qkv, la, tri)


def _gla_out_kernel(x_ref, of_ref, ob_ref, g_ref, mod_ref, hn_ref, wo_ref, o_ref, *, dv):
    o = of_ref[...] + ob_ref[...]
    parts = []
    for hh in range(GLA_HEADS):
        oh = o[:, hh * dv:(hh + 1) * dv]
        ms = jnp.mean(oh * oh, axis=-1, keepdims=True)
        parts.append(oh * lax.rsqrt(ms + NORM_EPS) * hn_ref[...])
    on = jnp.concatenate(parts, axis=-1)
    z = (on * _silu(g_ref[...])).astype(BF16)
    o_ref[...] = x_ref[...] + mod_ref[2:3, :] * _dot(z, wo_ref[...])


def _gla_readout(xs, of, ob, g, mod, w, ncb_tokens, dv):
    nb, t, d = xs.shape
    tb = TB_SCAN
    tok = pl.BlockSpec((None, tb, d), lambda b, j: (b, j, 0))
    weights = [w["head_norm"], w["w_o"]]
    return pl.pallas_call(
        functools.partial(_gla_out_kernel, dv=dv),
        grid=(nb, t // tb),
        in_specs=[tok] * 4 + [_mod_spec(nb, d, ncb_tokens // tb)] + [_full(a) for a in weights],
        out_specs=tok,
        out_shape=jax.ShapeDtypeStruct((nb, t, d), F32),
        compiler_params=_params(2),
        name="gla_readout",
    )(xs, of, ob, g, mod, *weights)


def _top_values(s, k):
    vals = []
    cur = s
    for _ in range(k):
        m = jnp.max(cur, axis=0, keepdims=True)
        vals.append(m)
        cur = jnp.where(cur == m, -jnp.inf, cur)
    return vals


def _sorting_network(n):
    pairs = []
    p = 1
    while p < n:
        k = p
        while k >= 1:
            for j in range(k % p, n - k, 2 * k):
                for i in range(min(k, n - j - k)):
                    if (i + j) // (2 * p) == (i + j + k) // (2 * p):
                        pairs.append((i + j, i + j + k))
            k //= 2
        p *= 2
    return pairs


def _top_values_tiled(s, k):
    sub = 8
    n = s.shape[0] // sub
    lists = [s[g * sub:(g + 1) * sub] for g in range(n)]
    for a, b in _sorting_network(n):
        hi, lo = jnp.maximum(lists[a], lists[b]), jnp.minimum(lists[a], lists[b])
        lists[a], lists[b] = hi, lo
    lists.append(jnp.full_like(lists[0], -jnp.inf))
    vals = []
    for it in range(k):
        m = jnp.max(lists[0], axis=0, keepdims=True)
        vals.append(m)
        took = lists[0] == m
        for r in range(min(n, k - it - 1)):
            lists[r] = jnp.where(took, lists[r + 1], lists[r])
    return vals


def _peer_route_kernel(x_ref, mod_ref, ng_ref, wq_ref, key_hi_ref, key_lo_ref,
                       ht_o, c0_o, e0_o, r1_o, e1_o, q_hi_ref, q_lo_ref):
    h = _modnorm(x_ref[...], ng_ref[...], mod_ref[3:4, :], mod_ref[4:5, :])
    hb = h.T.astype(BF16)
    ht_o[...] = hb
    qt = _dot(wq_ref[...], hb)
    q_hi, q_lo = _split2(qt)
    q_hi_ref[...] = q_hi
    q_lo_ref[...] = q_lo
    nk = N_KEYS
    tb = x_ref.shape[0]

    def head(hd, carry):
        sv, sc = [], []
        for p in range(2):
            hp = hd * 2 + p
            rows = pl.ds(pl.multiple_of(hp * nk, nk), nk)
            s = _dot3(key_hi_ref[hp], key_lo_ref[hp], q_hi_ref[rows, :], q_lo_ref[rows, :])
            sc.append(s)
            sv.append(_top_values_tiled(s, PEER_TOPK + 1))
        k, half = PEER_TOPK, PEER_TOPK // 2
        sv0 = jnp.concatenate(sv[0][:k], axis=0)
        sv1 = jnp.concatenate(sv[1][:k], axis=0)
        edge = jnp.concatenate([sv[0][k] + sv[1][0], sv[0][0] + sv[1][k],
                                jnp.full((6, tb), -jnp.inf, F32)], axis=0)
        cand = jnp.concatenate([sv[0][0] + sv1]
                               + [sv[0][a] + sv1[:half] for a in range(1, half)]
                               + [sv0[half:] + sv[1][0], edge], axis=0)
        top = _top_values(cand, k + 1)
        z = top[0] * 0.0
        for cval in top[:k]:
            z = z + jnp.exp(cval - top[0])
        tau = 0.5 * (top[k - 1] + top[k])
        theta = tau - sc[0]
        c0 = jnp.zeros_like(theta)
        r1 = jnp.full_like(theta, float(k))
        for a in range(k):
            c0 = jnp.where(sv[1][a] >= theta, float(a + 1), c0)
            r1 = jnp.where(sc[1] == sv[1][a], float(a), r1)
        c0_o[hd] = c0
        r1_o[hd] = r1.astype(BF16)
        e0_o[hd] = jnp.exp(sc[0] - sv[0][0]) * (1.0 / z)
        e1_o[hd] = jnp.exp(sc[1] - sv[1][0]).astype(BF16)
        return carry

    per_trip = 4

    def head_group(i, carry):
        for hh in range(per_trip):
            head(per_trip * i + hh, carry)
        return carry

    lax.fori_loop(0, PEER_HEADS // per_trip, head_group, 0)


def _peer_route(xs, mod, ng, w, ncb_tokens):
    nb, t, d = xs.shape
    tb = TB_ROUTE
    nq = w["wq"].shape[0]
    weights = [ng, w["wq"], w["key_hi"], w["key_lo"]]
    sspec = pl.BlockSpec((None, PEER_HEADS, N_KEYS, tb), lambda b, j: (b, 0, 0, j))
    return pl.pallas_call(
        _peer_route_kernel,
        grid=(nb, t // tb),
        in_specs=[pl.BlockSpec((None, tb, d), lambda b, j: (b, j, 0)),
                  _mod_spec(nb, d, ncb_tokens // tb)] + [_full(a) for a in weights],
        out_specs=[pl.BlockSpec((None, d, tb), lambda b, j: (b, 0, j)), sspec, sspec, sspec, sspec],
        out_shape=[jax.ShapeDtypeStruct((nb, d, t), BF16),
                   jax.ShapeDtypeStruct((nb, PEER_HEADS, N_KEYS, t), F32),
                   jax.ShapeDtypeStruct((nb, PEER_HEADS, N_KEYS, t), F32),
                   jax.ShapeDtypeStruct((nb, PEER_HEADS, N_KEYS, t), BF16),
                   jax.ShapeDtypeStruct((nb, PEER_HEADS, N_KEYS, t), BF16)],
        scratch_shapes=[pltpu.VMEM((nq, tb), BF16), pltpu.VMEM((nq, tb), BF16)],
        compiler_params=_params(2),
        name="peer_route",
    )(xs, mod, *weights)


def _peer_dense_kernel(x_ref, ht_ref, c0_ref, e0_ref, r1_ref, e1_ref, modl_ref, modc_ref, u_ref, vt_ref, o_ref,
                       acc_ref, cp_ref, ep_ref, act_ref, wa0_ref, wa1_ref,
                       *, nctx, nec, ntile, nsteps):
    g = pl.program_id(0)
    ne = u_ref.shape[0]
    d, tb = acc_ref.shape
    nk = N_KEYS
    nq = DENSE_QUARTERS
    qe, qd = ne // nq, d // nq
    qi = qe // nk
    g_b = jnp.clip(g - 1, 0, nsteps - 1)
    g_c = jnp.clip(g - 2, 0, nsteps - 1)
    e_b = g_b % nec
    e_c = g_c % nec
    tok0 = ((g_c // nec) % ntile) * tb

    @pl.when(g == 0)
    def _():
        for ref in (act_ref, wa0_ref, wa1_ref, cp_ref, ep_ref):
            ref[...] = jnp.zeros_like(ref)

    @pl.when(e_b == 0)
    def _():
        for hd in range(PEER_HEADS):
            c0 = c0_ref[hd]
            e0 = e0_ref[hd]
            for p in range(nk // qi):
                cp_ref[hd, p, 0:qi, :] = c0[p * qi:(p + 1) * qi]
                ep_ref[hd, p, 0:qi, :] = e0[p * qi:(p + 1) * qi]

    @pl.when(e_c == 0)
    def _():
        acc_ref[...] = jnp.zeros_like(acc_ref)

    jr = 32

    new, old = g % 2, (g + 1) % 2

    def quarter(wa_new, wa_old, q, carry):
        r0 = pl.multiple_of(q * qe, qe)
        d0 = pl.multiple_of(q * qd, qd)
        pair = e_b * nq + q
        nj = nk // jr
        ka, kc = d // nj, ne // nj
        act_ref[new, pl.ds(r0, qe), :] = jnp.zeros((qe, tb), F32)
        half = jr // 2
        cb, eb = [], []
        for hd in range(PEER_HEADS):
            ct = cp_ref[hd, pair]
            et = ep_ref[hd, pair]
            for ii in range(qi):
                c16 = jnp.broadcast_to(ct[ii:ii + 1], (half, tb)).astype(BF16)
                e16 = jnp.broadcast_to(et[ii:ii + 1], (half, tb)).astype(BF16)
                cb.append(jnp.concatenate([c16, c16], axis=0))
                eb.append(jnp.concatenate([e16, e16], axis=0))
        for jq in range(nj):
            act_ref[new, pl.ds(r0, qe), :] += _dot(u_ref[pl.ds(r0, qe), jq * ka:(jq + 1) * ka],
                                                   ht_ref[jq * ka:(jq + 1) * ka, :])
            acc_ref[pl.ds(d0, qd), :] += _dot(vt_ref[pl.ds(d0, qd), jq * kc:(jq + 1) * kc],
                                              wa_old[jq * kc:(jq + 1) * kc, :])
            rows = slice(jq * jr, (jq + 1) * jr)
            accs = [None] * qi
            for hd in range(PEER_HEADS):
                r1 = r1_ref[hd, rows, :]
                e1 = e1_ref[hd, rows, :]
                for ii in range(qi):
                    term = jnp.where(r1 < cb[hd * qi + ii], e1, jnp.zeros_like(e1)) * eb[hd * qi + ii]
                    accs[ii] = term if accs[ii] is None else accs[ii] + term
            for ii in range(qi):
                er = pl.ds(pl.multiple_of(r0 + ii * nk + jq * jr, jr), jr)
                a = act_ref[old, er, :]
                act = 0.5 * a * (1.0 + lax.erf(a * float(1.0 / np.sqrt(2.0))))
                wa_new[er, :] = (accs[ii].astype(F32) * act).astype(BF16)
        return carry

    @pl.when(g % 2 == 0)
    def _():
        lax.fori_loop(0, nq, functools.partial(quarter, wa0_ref, wa1_ref), 0)

    @pl.when(g % 2 == 1)
    def _():
        lax.fori_loop(0, nq, functools.partial(quarter, wa1_ref, wa0_ref), 0)

    @pl.when((e_c == nec - 1) & (g >= 2))
    def _():
        tok = tok0 + lax.broadcasted_iota(jnp.int32, (tb, 1), 0)
        gate = jnp.where(tok < nctx, modc_ref[5:6, :], modl_ref[5:6, :])
        o_ref[...] = x_ref[...] + gate * acc_ref[...].T


def _peer_dense(xs, ht, c0, e0, r1, e1, mod, w, ncb_tokens):
    nb, t, d = xs.shape
    tb = TB_DENSE
    ne = EC_DENSE
    n_exp = w["u"].shape[0]
    assert t % tb == 0 and n_exp % ne == 0 and ne % (DENSE_QUARTERS * N_KEYS) == 0
    ntile, nec = t // tb, n_exp // ne
    nsteps = nb * ntile * nec
    pairs = N_KEYS // (ne // DENSE_QUARTERS // N_KEYS)

    def at(lag):
        def f(g):
            s = jnp.clip(g - lag, 0, nsteps - 1)
            return s // (ntile * nec), (s // nec) % ntile, s % nec
        return f

    sa, sb, sc = at(0), at(1), at(2)
    sspec = pl.BlockSpec((None, PEER_HEADS, N_KEYS, tb), lambda g: (sb(g)[0], 0, 0, sb(g)[1]))
    return pl.pallas_call(
        functools.partial(_peer_dense_kernel, nctx=ncb_tokens, nec=nec, ntile=ntile, nsteps=nsteps),
        grid=(nsteps + 2,),
        in_specs=[pl.BlockSpec((None, tb, d), lambda g: (sc(g)[0], sc(g)[1], 0)),
                  pl.BlockSpec((None, d, tb), lambda g: (sa(g)[0], 0, sa(g)[1])),
                  sspec, sspec, sspec, sspec,
                  pl.BlockSpec((None, 6, d), lambda g: (sc(g)[0], 0, 0)),
                  pl.BlockSpec((None, 6, d), lambda g: (nb, 0, 0)),
                  pl.BlockSpec((ne, d), lambda g: (sa(g)[2], 0)),
                  pl.BlockSpec((d, ne), lambda g: (0, sc(g)[2]))],
        out_specs=pl.BlockSpec((None, tb, d), lambda g: (sc(g)[0], sc(g)[1], 0)),
        out_shape=jax.ShapeDtypeStruct((nb, t, d), F32),
        scratch_shapes=[pltpu.VMEM((d, tb), F32),
                        pltpu.VMEM((PEER_HEADS, pairs, 8, tb), F32),
                        pltpu.VMEM((PEER_HEADS, pairs, 8, tb), F32),
                        pltpu.VMEM((2, ne, tb), F32),
                        pltpu.VMEM((ne, tb), BF16), pltpu.VMEM((ne, tb), BF16)],
        compiler_params=_params(1),
        name="peer_dense",
    )(xs, ht, c0, e0, r1, e1, mod, mod, w["u"], w["vt"])


def _final_norm_kernel(x_ref, g_ref, o_ref):
    x = x_ref[...]
    ms = jnp.mean(x * x, axis=-1, keepdims=True)
    o_ref[...] = x * lax.rsqrt(ms + NORM_EPS) * g_ref[...]


def _final_norm(xs, g, ncb_tokens):
    nb, t, d = xs.shape
    tb = TB_SCAN
    ncb = ncb_tokens // tb
    return pl.pallas_call(
        _final_norm_kernel,
        grid=(nb, t // tb - ncb),
        in_specs=[pl.BlockSpec((None, tb, d), lambda b, j: (b, j + ncb, 0)), _full(g)],
        out_specs=pl.BlockSpec((None, tb, d), lambda b, j: (b, j, 0)),
        out_shape=jax.ShapeDtypeStruct((nb, t - ncb_tokens, d), F32),
        compiler_params=_params(2),
        name="final_norm",
    )(xs, g)


def _row(a):
    return a.reshape(1, -1)


def _pad_dir(w2):
    z = jnp.zeros_like(w2[0])
    return jnp.stack([jnp.concatenate([w2[0], z], axis=0), jnp.concatenate([z, w2[1]], axis=0)]).astype(BF16)


def kernel(x, c, ctx, c_ctx, ada_w, ada_b, norm_mix, norm_ffn, rw_mix, rw_w_rkv, rw_w0, rw_w1, rw_w2, rw_a0, rw_a1, rw_a2, rw_g1, rw_g2, rw_k_k, rw_k_a, rw_r_k, rw_ln_w, rw_ln_b, rw_w_o, gla_w_in, gla_conv, gla_gk_up, gla_gk_b, gla_head_norm, gla_w_o, peer_wq, peer_keys, peer_u, peer_v, final_norm):
    nb, seq, d = x.shape
    nctx = ctx.shape[1]
    depth = ada_w.shape[0]
    assert nctx == TB_SCAN and seq % TB_SCAN == 0 and d % LANES == 0

    xs = jnp.concatenate([ctx, x], axis=1)
    cc = jnp.zeros((16, d), F32).at[:nb].set(c).at[nb].set(c_ctx)
    mods = _modulation(cc, ada_w, ada_b)[:, :nb + 1].reshape(depth, nb + 1, 6, d)

    bd_head = _block_diag_ones(RWKV_HEAD)
    ci = np.arange(GLA_CHUNK)
    tri = jnp.asarray(np.stack([ci[None, :] <= ci[:, None], ci[None, :] >= ci[:, None]]), dtype=BF16)

    for i in range(depth):
        mod = mods[i]
        j = i // 2
        if i % 2 == 0:
            w = dict(
                mix=rw_mix[j], w_rkv=rw_w_rkv[j].astype(BF16),
                w1=jnp.concatenate([rw_w1[j, 0], rw_w1[j, 1]], axis=1).astype(BF16),
                a1=jnp.concatenate([rw_a1[j, 0], rw_a1[j, 1]], axis=1).astype(BF16),
                g1=rw_g1[j].astype(BF16), w2=_pad_dir(rw_w2[j]), a2=_pad_dir(rw_a2[j]),
                g2=rw_g2[j].astype(BF16), w0=rw_w0[j], a0=rw_a0[j],
                k_k=_row(rw_k_k[j]), k_a=_row(rw_k_a[j]), r_k=_row(rw_r_k[j]),
                ln_w=_row(rw_ln_w[j]), ln_b=_row(rw_ln_b[j]), w_o=rw_w_o[j].astype(BF16), bd=bd_head)
            P = _rw_project(xs, mod, _row(norm_mix[i]), w, nctx)
            yf, yb = _rw_scan(P, bd_head, nctx)
            xs = _rw_readout(xs, yf, yb, P[3], P[4], mod, w, nctx)
        else:
            w_in = gla_w_in[j]
            qk_width = gla_gk_up.shape[-1] * 2
            conv_ch = qk_width + d
            dk = gla_gk_up.shape[-1] // GLA_HEADS
            dv = d // GLA_HEADS
            rank = gla_gk_up.shape[2]
            lr0 = conv_ch + d
            w_lr = jnp.zeros((d, LANES), F32).at[:, :2 * rank].set(w_in[:, lr0:lr0 + 2 * rank])
            up = jnp.zeros((LANES, d), F32)
            up = up.at[:rank, :qk_width // 2].set(gla_gk_up[j, 0]).at[rank:2 * rank, qk_width // 2:].set(gla_gk_up[j, 1])
            w = dict(w_qkv=w_in[:, :conv_ch].astype(BF16), w_g=w_in[:, conv_ch:lr0].astype(BF16),
                     w_lr=w_lr.astype(BF16), up=up.astype(BF16),
                     gk_b=jnp.concatenate([gla_gk_b[j, 0], gla_gk_b[j, 1]]).reshape(1, d),
                     head_norm=_row(gla_head_norm[j]), w_o=gla_w_o[j].astype(BF16))
            z, g, la = _gla_project(xs, mod, _row(norm_mix[i]), w, nctx)
            qkv = _gla_conv(z, gla_conv[j].reshape(9, conv_ch), nctx, qk_width, float(dk) ** -0.5)
            of, ob = _gla_scan(qkv, la, tri, nctx, dk, dv)
            xs = _gla_readout(xs, of, ob, g, mod, w, nctx, dv)

        keys = peer_keys[i].reshape(PEER_HEADS * 2, N_KEYS, -1)
        key_hi = keys.astype(BF16)
        pw = dict(wq=peer_wq[i].T.astype(BF16),
                  key_hi=key_hi, key_lo=(keys - key_hi.astype(F32)).astype(BF16),
                  u=peer_u[i].astype(BF16), vt=peer_v[i].T.astype(BF16))
        ht, c0, e0, r1, e1 = _peer_route(xs, mod, _row(norm_ffn[i]), pw, nctx)
        xs = _peer_dense(xs, ht, c0, e0, r1, e1, mod, pw, nctx)

    return _final_norm(xs, _row(final_norm), nctx)
```

```python
import functools

import numpy as np
import jax
import jax.numpy as jnp
from jax import lax
from jax.experimental import pallas as pl
from jax.experimental.pallas import tpu as pltpu

F32 = jnp.float32
BF16 = jnp.bfloat16

NORM_EPS = 1e-6
GRID_W = 64
RWKV_HEAD = 64
RWKV_GN_EPS = 64e-5
DECAY_LORA = 64
GLA_HEADS = 4
GLA_GATE_RANK = 16
GLA_GATE_NORM = 16.0
GLA_CHUNK = 64
PEER_HEADS = 8
N_KEYS = 128
PEER_TOPK = 16

LANES = 128
VMEM_LIMIT = 56 * 1024 * 1024

TB_PROJ = 128
TB_SCAN = 256
TB_ROUTE = 256
TB_DENSE = 256
EC_DENSE = 2048
DENSE_QUARTERS = 4


def _params(n_axes):
    return pltpu.CompilerParams(
        dimension_semantics=("arbitrary",) * n_axes,
        vmem_limit_bytes=VMEM_LIMIT)


def _full(a):
    nd = a.ndim
    return pl.BlockSpec(a.shape, lambda *_: (0,) * nd)


def _sigmoid(x):
    return 0.5 + 0.5 * jnp.tanh(0.5 * x)


def _silu(x):
    return x * _sigmoid(x)


def _log_sigmoid(x):
    return jnp.minimum(x, 0.0) - jnp.log(1.0 + jnp.exp(-jnp.abs(x)))


def _modnorm(x, g, shift, scale):
    ms = jnp.mean(x * x, axis=-1, keepdims=True)
    return x * lax.rsqrt(ms + NORM_EPS) * (g * (1.0 + scale)) + shift


def _split2(x):
    hi = x.astype(BF16)
    lo = (x - hi.astype(F32)).astype(BF16)
    return hi, lo


def _split3(x):
    x1 = x.astype(BF16)
    r1 = x - x1.astype(F32)
    x2 = r1.astype(BF16)
    x3 = (r1 - x2.astype(F32)).astype(BF16)
    return x1, x2, x3


def _dot(a, b):
    return jnp.dot(a, b, preferred_element_type=F32)


def _dot_nt(a, b):
    return lax.dot_general(a, b, (((1,), (1,)), ((), ())), preferred_element_type=F32)


def _dot_tn(a, b):
    return lax.dot_general(a, b, (((0,), (0,)), ((), ())), preferred_element_type=F32)


def _dot3(a_hi, a_lo, b_hi, b_lo):
    return _dot(a_hi, b_hi) + _dot(a_hi, b_lo) + _dot(a_lo, b_hi)


def _seg_sum(x, bd):
    outs = []
    for c in range(x.shape[-1] // LANES):
        hi, lo = _split2(x[:, c * LANES:(c + 1) * LANES])
        outs.append(_dot(hi, bd) + _dot(lo, bd))
    return jnp.concatenate(outs, axis=-1)


def _block_diag_ones(group):
    i = np.arange(LANES)
    return jnp.asarray((i[:, None] // group) == (i[None, :] // group), dtype=BF16)


def _mod_kernel(c_ref, w_ref, b_ref, o_ref):
    a = _silu(c_ref[...])
    o_ref[...] = jnp.dot(a, w_ref[...], preferred_element_type=F32,
                         precision=lax.Precision.HIGHEST) + b_ref[...]


def _modulation(cc, ada_w, ada_b):
    depth, d, n = ada_w.shape
    nt = 768
    return pl.pallas_call(
        _mod_kernel,
        grid=(depth, n // nt),
        in_specs=[pl.BlockSpec(cc.shape, lambda i, k: (0, 0)),
                  pl.BlockSpec((None, d, nt), lambda i, k: (i, 0, k)),
                  pl.BlockSpec((None, 1, nt), lambda i, k: (i, 0, k))],
        out_specs=pl.BlockSpec((None, cc.shape[0], nt), lambda i, k: (i, 0, k)),
        out_shape=jax.ShapeDtypeStruct((depth, cc.shape[0], n), F32),
        compiler_params=_params(2),
        name="modulation",
    )(cc, ada_w, ada_b.reshape(depth, 1, n))


def _mod_spec(nb, d, ncb):
    return pl.BlockSpec((None, 6, d), lambda b, j: (jnp.where(j < ncb, nb, b), 0, 0))


def _rw_proj_kernel(x_ref, xp_ref, xn_ref, mod_ref, ng_ref, mix_ref, wrkv_ref,
                    w1_ref, a1_ref, g1_ref, w2_ref, a2_ref, g2_ref, w0_ref, a0_ref,
                    kk_w_ref, ka_ref, rk_ref, bd_ref,
                    r_o, v_o, kk_o, g_o, bon_o, w0_o, k0_o, b0_o, w1_o, k1_o, b1_o,
                    *, ncb, nblk):
    j = pl.program_id(1)
    tb = x_ref.shape[0]
    shift = mod_ref[0:1, :]
    scale = mod_ref[1:2, :]
    g = ng_ref[...]
    h = _modnorm(x_ref[...], g, shift, scale)
    hp = _modnorm(xp_ref[7:8, :], g, shift, scale)
    hn = _modnorm(xn_ref[0:1, :], g, shift, scale)
    hp = jnp.where((j != 0) & (j != ncb), hp, 0.0)
    hn = jnp.where((j != ncb - 1) & (j != nblk - 1), hn, 0.0)
    row = lax.broadcasted_iota(jnp.int32, (tb, 1), 0)
    prev = jnp.where(row == 0, hp, pltpu.roll(h, 1, 0))
    nxt = jnp.where(row == tb - 1, hn, pltpu.roll(h, tb - 1, 0))
    xx = 0.5 * (prev + nxt) - h
    xr, xw, xk, xv, xa, xg = (h + xx * mix_ref[i:i + 1, :] for i in range(6))

    r = _dot(xr.astype(BF16), wrkv_ref[0])
    k = _dot(xk.astype(BF16), wrkv_ref[1])
    v = _dot(xv.astype(BF16), wrkv_ref[2])
    gate = _dot(_sigmoid(_dot(xg.astype(BF16), g1_ref[...])).astype(BF16), g2_ref[...])
    tw = jnp.tanh(_dot(xw.astype(BF16), w1_ref[...])).astype(BF16)
    ta = _dot(xa.astype(BF16), a1_ref[...]).astype(BF16)

    bd = bd_ref[...]
    kk = k * kk_w_ref[...]
    kk = kk / jnp.maximum(jnp.sqrt(_seg_sum(kk * kk, bd)), 1e-12)

    r_o[...] = r
    v_o[...] = v
    kk_o[...] = kk
    g_o[...] = gate
    ksum = None
    for d, (w_o, k_o, b_o) in enumerate(((w0_o, k0_o, b0_o), (w1_o, k1_o, b1_o))):
        pre = w0_ref[d:d + 1, :] + _dot(tw, w2_ref[d])
        w_o[...] = jnp.exp(-_sigmoid(pre) * float(np.exp(-0.5)))
        a = _sigmoid(a0_ref[d:d + 1, :] + _dot(ta, a2_ref[d]))
        kd = k * (1.0 + (a - 1.0) * ka_ref[...])
        k_o[...] = kd
        b_o[...] = kk * a
        ksum = kd if ksum is None else ksum + kd
    bon_o[...] = _seg_sum(r * ksum * rk_ref[...], bd) * v


def _rw_project(xs, mod, ng, w, ncb_tokens):
    nb, t, d = xs.shape
    tb = TB_PROJ
    nblk, ncb = t // tb, ncb_tokens // tb
    sl = tb // 8
    weights = [ng, w["mix"], w["w_rkv"], w["w1"], w["a1"], w["g1"], w["w2"], w["a2"], w["g2"],
               w["w0"], w["a0"], w["k_k"], w["k_a"], w["r_k"], w["bd"]]
    tok = pl.BlockSpec((None, tb, d), lambda b, j: (b, j, 0))
    return pl.pallas_call(
        functools.partial(_rw_proj_kernel, ncb=ncb, nblk=nblk),
        grid=(nb, nblk),
        in_specs=[tok,
                  pl.BlockSpec((None, 8, d), lambda b, j: (b, jnp.maximum(j * sl - 1, 0), 0)),
                  pl.BlockSpec((None, 8, d), lambda b, j: (b, jnp.minimum((j + 1) * sl, t // 8 - 1), 0)),
                  _mod_spec(nb, d, ncb)] + [_full(a) for a in weights],
        out_specs=[tok] * 11,
        out_shape=[jax.ShapeDtypeStruct((nb, t, d), F32)] * 11,
        compiler_params=_params(2),
        name="rwkv_project",
    )(xs, xs, xs, mod, *weights)


def _rw_scan_kernel(rf, wf, kf, vf, kkf, bf, rb, wb, kb, vb, kkb, bb, bd_ref,
                    yf_o, yb_o, s_ref):
    tb = rf.shape[0]
    nch = rf.shape[1] // LANES

    @pl.when(pl.program_id(1) == 0)
    def _():
        s_ref[...] = jnp.zeros_like(s_ref)

    bd = bd_ref[...]
    hw = RWKV_HEAD
    vi = lax.broadcasted_iota(jnp.int32, (hw, LANES), 0)
    li = lax.broadcasted_iota(jnp.int32, (hw, LANES), 1)
    diag_bf = ((li & (hw - 1)) == vi).astype(F32).astype(BF16)
    lane = lax.broadcasted_iota(jnp.int32, (1, LANES), 1)
    first_head = lane < hw
    head_mask = (first_head.astype(F32), 1.0 - first_head.astype(F32))
    sub = 64

    gch = 4
    groups = [(d, list(range(c0, c0 + gch))) for c0 in range(0, nch, gch) for d in (0, 1)]
    fwd = (rf, wf, kf, vf, kkf, bf)
    bwd = (rb, wb, kb, vb, kkb, bb)

    def rows_of(s, i):
        return s[i * hw:(i + 1) * hw]

    def body(g, carry):
        base = (pl.multiple_of(g * sub, sub), pl.multiple_of(tb - sub - g * sub, sub))
        lanes = [slice(c * LANES, (c + 1) * LANES) for c in range(nch)]
        tiles = [[[ref[pl.ds(base[d], sub), ls] for ref in refs] for ls in lanes]
                 for d, refs in enumerate((fwd, bwd))]

        def rows(d, c, n):
            m = n if d == 0 else sub - 1 - n
            return [t[m:m + 1] for t in tiles[d][c]]

        def tile_bf(row):
            t = jnp.broadcast_to(row, (2 * 8, LANES)).astype(BF16)
            return jnp.concatenate([t] * (hw // 16), axis=0)

        def start(gi, sb, n):
            d, cs = groups[gi]
            vcol = _dot(jnp.concatenate([tile_bf(rows(d, c, n)[3]) * diag_bf for c in cs], axis=0), bd)
            sa = _dot(sb * jnp.concatenate([tile_bf(rows(d, c, n)[4]) for c in cs], axis=0), bd)
            return vcol, sa

        state = [[s_ref[d, c] for c in cs] for d, cs in groups]
        pend = [start(gi, jnp.concatenate(state[gi], axis=0).astype(BF16), 0) for gi in range(len(groups))]
        ys = [[[] for _ in range(nch)] for _ in range(2)]
        for n in range(sub):
            for gi, (d, cs) in enumerate(groups):
                vcol, sa = pend[gi]
                s = []
                for i, c in enumerate(cs):
                    r, w, k, v, kk, b = rows(d, c, n)
                    s.append(state[gi][i] * w - rows_of(sa, i) * b + rows_of(vcol, i) * k)
                state[gi] = s
                sb = jnp.concatenate(s, axis=0).astype(BF16)
                if n + 1 < sub:
                    pend[gi] = start(gi, sb, n + 1)
                lhs = jnp.concatenate([rows(d, c, n)[0] * head_mask[h] for c in cs for h in range(2)], axis=0)
                out = _dot_nt(lhs.astype(BF16), sb)
                for m in range(gch // 2):
                    x = out[:, m * LANES:(m + 1) * LANES]
                    xr = pltpu.roll(x, hw, 1)
                    r0 = 4 * m
                    ys[d][cs[2 * m]].append(jnp.where(first_head, x[r0:r0 + 1], xr[r0 + 1:r0 + 2]))
                    ys[d][cs[2 * m + 1]].append(jnp.where(first_head, xr[r0 + 2:r0 + 3], x[r0 + 3:r0 + 4]))
        for gi, (d, cs) in enumerate(groups):
            for i, c in enumerate(cs):
                s_ref[d, c] = state[gi][i]
        for c, ls in enumerate(lanes):
            yf_o[pl.ds(base[0], sub), ls] = jnp.concatenate(ys[0][c], axis=0)
            yb_o[pl.ds(base[1], sub), ls] = jnp.concatenate(ys[1][c][::-1], axis=0)
        return carry

    lax.fori_loop(0, tb // sub, body, 0)


def _bwd_block(j, ncb, nblk):
    return jnp.where(j < ncb, ncb - 1 - j, nblk - 1 - (j - ncb))


def _rw_scan(P, bd, ncb_tokens):
    r, v, kk, _, _, w0, k0, b0, w1, k1, b1 = P
    nb, t, d = r.shape
    tb = TB_SCAN
    nblk, ncb = t // tb, ncb_tokens // tb
    fs = pl.BlockSpec((None, tb, d), lambda b, j: (b, j, 0))
    bs = pl.BlockSpec((None, tb, d), lambda b, j: (b, _bwd_block(j, ncb, nblk), 0))
    return pl.pallas_call(
        _rw_scan_kernel,
        grid=(nb, nblk),
        in_specs=[fs] * 6 + [bs] * 6 + [_full(bd)],
        out_specs=[fs, bs],
        out_shape=[jax.ShapeDtypeStruct((nb, t, d), F32)] * 2,
        scratch_shapes=[pltpu.VMEM((2, d // LANES, RWKV_HEAD, LANES), F32)],
        compiler_params=_params(2),
        name="rwkv_scan",
    )(r, w0, k0, v, kk, b0, r, w1, k1, v, kk, b1, bd)


def _rw_out_kernel(x_ref, yf_ref, yb_ref, g_ref, bon_ref, mod_ref, lnw_ref, lnb_ref, wo_ref, bd_ref, o_ref):
    bd = bd_ref[...]
    y = yf_ref[...] + yb_ref[...]
    mu = _seg_sum(y, bd) * (1.0 / RWKV_HEAD)
    yc = y - mu
    var = _seg_sum(yc * yc, bd) * (1.0 / RWKV_HEAD)
    yn = yc * lax.rsqrt(var + RWKV_GN_EPS) * lnw_ref[...] + lnb_ref[...]
    z = ((yn + bon_ref[...]) * g_ref[...]).astype(BF16)
    o_ref[...] = x_ref[...] + mod_ref[2:3, :] * _dot(z, wo_ref[...])


def _rw_readout(xs, yf, yb, gate, bon, mod, w, ncb_tokens):
    nb, t, d = xs.shape
    tb = TB_SCAN
    tok = pl.BlockSpec((None, tb, d), lambda b, j: (b, j, 0))
    weights = [w["ln_w"], w["ln_b"], w["w_o"], w["bd"]]
    return pl.pallas_call(
        _rw_out_kernel,
        grid=(nb, t // tb),
        in_specs=[tok] * 5 + [_mod_spec(nb, d, ncb_tokens // tb)] + [_full(a) for a in weights],
        out_specs=tok,
        out_shape=jax.ShapeDtypeStruct((nb, t, d), F32),
        compiler_params=_params(2),
        name="rwkv_readout",
    )(xs, yf, yb, gate, bon, mod, *weights)


def _gla_proj_kernel(x_ref, mod_ref, ng_ref, wqkv_ref, wg_ref, wlr_ref, up_ref, gkb_ref,
                     z_o, g_o, la_o):
    h = _modnorm(x_ref[...], ng_ref[...], mod_ref[0:1, :], mod_ref[1:2, :])
    hb = h.astype(BF16)
    z_o[...] = _dot(hb, wqkv_ref[...])
    g_o[...] = _dot(hb, wg_ref[...])
    lr = _dot(hb, wlr_ref[...]).astype(BF16)
    la_o[...] = _log_sigmoid(_dot(lr, up_ref[...]) + gkb_ref[...]) * (1.0 / GLA_GATE_NORM)


def _gla_project(xs, mod, ng, w, ncb_tokens):
    nb, t, d = xs.shape
    tb = TB_SCAN
    weights = [ng, w["w_qkv"], w["w_g"], w["w_lr"], w["up"], w["gk_b"]]
    tok = pl.BlockSpec((None, tb, d), lambda b, j: (b, j, 0))
    tok2 = pl.BlockSpec((None, tb, 2 * d), lambda b, j: (b, j, 0))
    return pl.pallas_call(
        _gla_proj_kernel,
        grid=(nb, t // tb),
        in_specs=[tok, _mod_spec(nb, d, ncb_tokens // tb)] + [_full(a) for a in weights],
        out_specs=[tok2, tok, tok],
        out_shape=[jax.ShapeDtypeStruct((nb, t, 2 * d), F32),
                   jax.ShapeDtypeStruct((nb, t, d), F32),
                   jax.ShapeDtypeStruct((nb, t, d), F32)],
        compiler_params=_params(2),
        name="gla_project",
    )(xs, mod, *weights)


def _gla_conv_kernel(zc_ref, zp_ref, zn_ref, cw_ref, o_ref, *, nblk, qk_width, q_scale):
    j = pl.program_id(1)
    tb, ch = zc_ref.shape
    is_ctx = j == 0
    up_ok = jnp.where(j > 1, 1.0, 0.0)
    dn_ok = jnp.where((j > 0) & (j < nblk - 1), 1.0, 0.0)
    vert = jnp.where(is_ctx, 0.0, 1.0)
    ext = tb + 2 * GRID_W
    pos = lax.broadcasted_iota(jnp.int32, (ext, 1), 0) - GRID_W
    col = jnp.where(is_ctx, pos, pos & (GRID_W - 1))
    no_left = col == 0
    no_right = col == jnp.where(is_ctx, tb - 1, GRID_W - 1)
    cc = 256
    for c in range(ch // cc):
        cs = slice(c * cc, (c + 1) * cc)
        e = jnp.concatenate([zp_ref[:, cs] * up_ok, zc_ref[:, cs], zn_ref[:, cs] * dn_ok], axis=0)
        em = jnp.where(no_left, 0.0, pltpu.roll(e, 1, 0))
        ep = jnp.where(no_right, 0.0, pltpu.roll(e, ext - 1, 0))
        acc = None
        for dy in range(3):
            lo = dy * GRID_W
            for dx, src in enumerate((em, e, ep)):
                wt = cw_ref[dy * 3 + dx:dy * 3 + dx + 1, cs]
                if dy != 1:
                    wt = wt * vert
                term = src[lo:lo + tb] * wt
                acc = term if acc is None else acc + term
        y = _silu(acc)
        if (c + 1) * cc <= qk_width // 2:
            y = y * q_scale
        o_ref[:, cs] = y


def _gla_conv(z, cw, ncb_tokens, qk_width, q_scale):
    nb, t, ch = z.shape
    tb = TB_SCAN
    assert ncb_tokens == tb and tb % GRID_W == 0
    sl = tb // GRID_W
    nblk = t // tb
    return pl.pallas_call(
        functools.partial(_gla_conv_kernel, nblk=nblk, qk_width=qk_width, q_scale=q_scale),
        grid=(nb, nblk),
        in_specs=[pl.BlockSpec((None, tb, ch), lambda b, j: (b, j, 0)),
                  pl.BlockSpec((None, GRID_W, ch), lambda b, j: (b, jnp.maximum(j * sl - 1, 0), 0)),
                  pl.BlockSpec((None, GRID_W, ch),
                               lambda b, j: (b, jnp.minimum((j + 1) * sl, t // GRID_W - 1), 0)),
                  _full(cw)],
        out_specs=pl.BlockSpec((None, tb, ch), lambda b, j: (b, j, 0)),
        out_shape=jax.ShapeDtypeStruct((nb, t, ch), F32),
        compiler_params=_params(2),
        name="gla_conv",
    )(z, z, z, cw)


def _gla_scan_kernel(qf_ref, laf_ref, qb_ref, lab_ref, tri_ref, of_o, ob_o, st_ref, *, dk, dv):
    tb = qf_ref.shape[0]
    c = GLA_CHUNK
    nh = GLA_HEADS
    qkw = nh * dk

    @pl.when(pl.program_id(1) == 0)
    def _():
        st_ref[...] = jnp.zeros_like(st_ref)

    dirs = ((qf_ref, laf_ref, of_o), (qb_ref, lab_ref, ob_o))
    nchunk = tb // c
    for p in range(nchunk):
        units = [(d, hh) for d in range(2) for hh in range(nh)]

        def load(d, hh):
            q_ref, la_ref, _ = dirs[d]
            n = p if d == 0 else nchunk - 1 - p
            rows = slice(n * c, (n + 1) * c)
            return (rows, q_ref[rows, hh * dk:(hh + 1) * dk], q_ref[rows, qkw + hh * dk:qkw + (hh + 1) * dk],
                    q_ref[rows, 2 * qkw + hh * dv:2 * qkw + (hh + 1) * dv].astype(BF16),
                    la_ref[rows, d * qkw + hh * dk:d * qkw + (hh + 1) * dk])

        data = [load(d, hh) for d, hh in units]
        bcums = []
        for (d, hh), (_, _, _, _, la) in zip(units, data):
            tri = tri_ref[d]
            l1, l2, l3 = _split3(la)
            bcums.append(_dot(tri, l1) + _dot(tri, l2) + _dot(tri, l3))
        intra = []
        for (d, hh), (_, q, k, _, _), bcum in zip(units, data, bcums):
            ref_row = c // 2 if d == 0 else c - 1 - c // 2
            ref = bcum[ref_row:ref_row + 1]
            a = _dot_nt((q * jnp.exp(bcum - ref)).astype(BF16), (k * jnp.exp(ref - bcum)).astype(BF16))
            intra.append(jnp.where(tri_ref[d] > 0, a, 0.0).astype(BF16))
        for (d, hh), (rows, q, k, vb, _), bcum, a in zip(units, data, bcums, intra):
            last_row = c - 1 if d == 0 else 0
            last = bcum[last_row:last_row + 1]
            st = st_ref[d, hh]
            o = _dot(a, vb) + _dot_nt((q * jnp.exp(bcum)).astype(BF16), st.astype(BF16))
            dirs[d][2][rows, hh * dv:(hh + 1) * dv] = o
            kd = (k * jnp.exp(last - bcum)).astype(BF16)
            st_ref[d, hh] = st * jnp.exp(last) + _dot_tn(vb, kd)


def _gla_scan(qkv, la, tri, ncb_tokens, dk, dv):
    nb, t, ch = qkv.shape
    d = la.shape[-1]
    tb = TB_SCAN
    nblk, ncb = t // tb, ncb_tokens // tb
    f2 = pl.BlockSpec((None, tb, ch), lambda b, j: (b, j, 0))
    b2 = pl.BlockSpec((None, tb, ch), lambda b, j: (b, _bwd_block(j, ncb, nblk), 0))
    f1 = pl.BlockSpec((None, tb, d), lambda b, j: (b, j, 0))
    b1 = pl.BlockSpec((None, tb, d), lambda b, j: (b, _bwd_block(j, ncb, nblk), 0))
    return pl.pallas_call(
        functools.partial(_gla_scan_kernel, dk=dk, dv=dv),
        grid=(nb, nblk),
        in_specs=[f2, f1, b2, b1, _full(tri)],
        out_specs=[f1, b1],
        out_shape=[jax.ShapeDtypeStruct((nb, t, d), F32)] * 2,
        scratch_shapes=[pltpu.VMEM((2, GLA_HEADS, dv, dk), F32)],
        compiler_params=_params(2),
        name="gla_scan",
    )(qkv, la, qkv, la, tri)


def _gla_out_kernel(x_ref, of_ref, ob_ref, g_ref, mod_ref, hn_ref, wo_ref, o_ref, *, dv):
    o = of_ref[...] + ob_ref[...]
    parts = []
    for hh in range(GLA_HEADS):
        oh = o[:, hh * dv:(hh + 1) * dv]
        ms = jnp.mean(oh * oh, axis=-1, keepdims=True)
        parts.append(oh * lax.rsqrt(ms + NORM_EPS) * hn_ref[...])
    on = jnp.concatenate(parts, axis=-1)
    z = (on * _silu(g_ref[...])).astype(BF16)
    o_ref[...] = x_ref[...] + mod_ref[2:3, :] * _dot(z, wo_ref[...])


def _gla_readout(xs, of, ob, g, mod, w, ncb_tokens, dv):
    nb, t, d = xs.shape
    tb = TB_SCAN
    tok = pl.BlockSpec((None, tb, d), lambda b, j: (b, j, 0))
    weights = [w["head_norm"], w["w_o"]]
    return pl.pallas_call(
        functools.partial(_gla_out_kernel, dv=dv),
        grid=(nb, t // tb),
        in_specs=[tok] * 4 + [_mod_spec(nb, d, ncb_tokens // tb)] + [_full(a) for a in weights],
        out_specs=tok,
        out_shape=jax.ShapeDtypeStruct((nb, t, d), F32),
        compiler_params=_params(2),
        name="gla_readout",
    )(xs, of, ob, g, mod, *weights)


def _top_values(s, k):
    vals = []
    cur = s
    for _ in range(k):
        m = jnp.max(cur, axis=0, keepdims=True)
        vals.append(m)
        cur = jnp.where(cur == m, -jnp.inf, cur)
    return vals


def _sorting_network(n):
    pairs = []
    p = 1
    while p < n:
        k = p
        while k >= 1:
            for j in range(k % p, n - k, 2 * k):
                for i in range(min(k, n - j - k)):
                    if (i + j) // (2 * p) == (i + j + k) // (2 * p):
                        pairs.append((i + j, i + j + k))
            k //= 2
        p *= 2
    return pairs


def _top_values_tiled(s, k):
    sub = 8
    n = s.shape[0] // sub
    lists = [s[g * sub:(g + 1) * sub] for g in range(n)]
    for a, b in _sorting_network(n):
        hi, lo = jnp.maximum(lists[a], lists[b]), jnp.minimum(lists[a], lists[b])
        lists[a], lists[b] = hi, lo
    lists.append(jnp.full_like(lists[0], -jnp.inf))
    vals = []
    for it in range(k):
        m = jnp.max(lists[0], axis=0, keepdims=True)
        vals.append(m)
        took = lists[0] == m
        for r in range(min(n, k - it - 1)):
            lists[r] = jnp.where(took, lists[r + 1], lists[r])
    return vals


def _peer_route_kernel(x_ref, mod_ref, ng_ref, wq_ref, key_hi_ref, key_lo_ref,
                       ht_o, c0_o, e0_o, r1_o, e1_o, q_hi_ref, q_lo_ref):
    h = _modnorm(x_ref[...], ng_ref[...], mod_ref[3:4, :], mod_ref[4:5, :])
    hb = h.T.astype(BF16)
    ht_o[...] = hb
    qt = _dot(wq_ref[...], hb)
    q_hi, q_lo = _split2(qt)
    q_hi_ref[...] = q_hi
    q_lo_ref[...] = q_lo
    nk = N_KEYS
    tb = x_ref.shape[0]

    def head(hd, carry):
        sv, sc = [], []
        for p in range(2):
            hp = hd * 2 + p
            rows = pl.ds(pl.multiple_of(hp * nk, nk), nk)
            s = _dot3(key_hi_ref[hp], key_lo_ref[hp], q_hi_ref[rows, :], q_lo_ref[rows, :])
            sc.append(s)
            sv.append(_top_values_tiled(s, PEER_TOPK + 1))
        k, half = PEER_TOPK, PEER_TOPK // 2
        sv0 = jnp.concatenate(sv[0][:k], axis=0)
        sv1 = jnp.concatenate(sv[1][:k], axis=0)
        edge = jnp.concatenate([sv[0][k] + sv[1][0], sv[0][0] + sv[1][k],
                                jnp.full((6, tb), -jnp.inf, F32)], axis=0)
        cand = jnp.concatenate([sv[0][0] + sv1]
                               + [sv[0][a] + sv1[:half] for a in range(1, half)]
                               + [sv0[half:] + sv[1][0], edge], axis=0)
        top = _top_values(cand, k + 1)
        z = top[0] * 0.0
        for cval in top[:k]:
            z = z + jnp.exp(cval - top[0])
        tau = 0.5 * (top[k - 1] + top[k])
        theta = tau - sc[0]
        c0 = jnp.zeros_like(theta)
        r1 = jnp.full_like(theta, float(k))
        for a in range(k):
            c0 = jnp.where(sv[1][a] >= theta, float(a + 1), c0)
            r1 = jnp.where(sc[1] == sv[1][a], float(a), r1)
        c0_o[hd] = c0
        r1_o[hd] = r1.astype(BF16)
        e0_o[hd] = jnp.exp(sc[0] - sv[0][0]) * (1.0 / z)
        e1_o[hd] = jnp.exp(sc[1] - sv[1][0]).astype(BF16)
        return carry

    per_trip = 4

    def head_group(i, carry):
        for hh in range(per_trip):
            head(per_trip * i + hh, carry)
        return carry

    lax.fori_loop(0, PEER_HEADS // per_trip, head_group, 0)


def _peer_route(xs, mod, ng, w, ncb_tokens):
    nb, t, d = xs.shape
    tb = TB_ROUTE
    nq = w["wq"].shape[0]
    weights = [ng, w["wq"], w["key_hi"], w["key_lo"]]
    sspec = pl.BlockSpec((None, PEER_HEADS, N_KEYS, tb), lambda b, j: (b, 0, 0, j))
    return pl.pallas_call(
        _peer_route_kernel,
        grid=(nb, t // tb),
        in_specs=[pl.BlockSpec((None, tb, d), lambda b, j: (b, j, 0)),
                  _mod_spec(nb, d, ncb_tokens // tb)] + [_full(a) for a in weights],
        out_specs=[pl.BlockSpec((None, d, tb), lambda b, j: (b, 0, j)), sspec, sspec, sspec, sspec],
        out_shape=[jax.ShapeDtypeStruct((nb, d, t), BF16),
                   jax.ShapeDtypeStruct((nb, PEER_HEADS, N_KEYS, t), F32),
                   jax.ShapeDtypeStruct((nb, PEER_HEADS, N_KEYS, t), F32),
                   jax.ShapeDtypeStruct((nb, PEER_HEADS, N_KEYS, t), BF16),
                   jax.ShapeDtypeStruct((nb, PEER_HEADS, N_KEYS, t), BF16)],
        scratch_shapes=[pltpu.VMEM((nq, tb), BF16), pltpu.VMEM((nq, tb), BF16)],
        compiler_params=_params(2),
        name="peer_route",
    )(xs, mod, *weights)


def _peer_dense_kernel(x_ref, ht_ref, c0_ref, e0_ref, r1_ref, e1_ref, modl_ref, modc_ref, u_ref, vt_ref, o_ref,
                       acc_ref, cp_ref, ep_ref, act_ref, wa0_ref, wa1_ref,
                       *, nctx, nec, ntile, nsteps):
    g = pl.program_id(0)
    ne = u_ref.shape[0]
    d, tb = acc_ref.shape
    nk = N_KEYS
    nq = DENSE_QUARTERS
    qe, qd = ne // nq, d // nq
    qi = qe // nk
    g_b = jnp.clip(g - 1, 0, nsteps - 1)
    g_c = jnp.clip(g - 2, 0, nsteps - 1)
    e_b = g_b % nec
    e_c = g_c % nec
    tok0 = ((g_c // nec) % ntile) * tb

    @pl.when(g == 0)
    def _():
        for ref in (act_ref, wa0_ref, wa1_ref, cp_ref, ep_ref):
            ref[...] = jnp.zeros_like(ref)

    @pl.when(e_b == 0)
    def _():
        for hd in range(PEER_HEADS):
            c0 = c0_ref[hd]
            e0 = e0_ref[hd]
            for p in range(nk // qi):
                cp_ref[hd, p, 0:qi, :] = c0[p * qi:(p + 1) * qi]
                ep_ref[hd, p, 0:qi, :] = e0[p * qi:(p + 1) * qi]

    @pl.when(e_c == 0)
    def _():
        acc_ref[...] = jnp.zeros_like(acc_ref)

    jr = 32

    new, old = g % 2, (g + 1) % 2

    def quarter(wa_new, wa_old, q, carry):
        r0 = pl.multiple_of(q * qe, qe)
        d0 = pl.multiple_of(q * qd, qd)
        pair = e_b * nq + q
        nj = nk // jr
        ka, kc = d // nj, ne // nj
        act_ref[new, pl.ds(r0, qe), :] = jnp.zeros((qe, tb), F32)
        half = jr // 2
        cb, eb = [], []
        for hd in range(PEER_HEADS):
            ct = cp_ref[hd, pair]
            et = ep_ref[hd, pair]
            for ii in range(qi):
                c16 = jnp.broadcast_to(ct[ii:ii + 1], (half, tb)).astype(BF16)
                e16 = jnp.broadcast_to(et[ii:ii + 1], (half, tb)).astype(BF16)
                cb.append(jnp.concatenate([c16, c16], axis=0))
                eb.append(jnp.concatenate([e16, e16], axis=0))
        for jq in range(nj):
            act_ref[new, pl.ds(r0, qe), :] += _dot(u_ref[pl.ds(r0, qe), jq * ka:(jq + 1) * ka],
                                                   ht_ref[jq * ka:(jq + 1) * ka, :])
            acc_ref[pl.ds(d0, qd), :] += _dot(vt_ref[pl.ds(d0, qd), jq * kc:(jq + 1) * kc],
                                              wa_old[jq * kc:(jq + 1) * kc, :])
            rows = slice(jq * jr, (jq + 1) * jr)
            accs = [None] * qi
            for hd in range(PEER_HEADS):
                r1 = r1_ref[hd, rows, :]
                e1 = e1_ref[hd, rows, :]
                for ii in range(qi):
                    term = jnp.where(r1 < cb[hd * qi + ii], e1, jnp.zeros_like(e1)) * eb[hd * qi + ii]
                    accs[ii] = term if accs[ii] is None else accs[ii] + term
            for ii in range(qi):
                er = pl.ds(pl.multiple_of(r0 + ii * nk + jq * jr, jr), jr)
                a = act_ref[old, er, :]
                act = 0.5 * a * (1.0 + lax.erf(a * float(1.0 / np.sqrt(2.0))))
                wa_new[er, :] = (accs[ii].astype(F32) * act).astype(BF16)
        return carry

    @pl.when(g % 2 == 0)
    def _():
        lax.fori_loop(0, nq, functools.partial(quarter, wa0_ref, wa1_ref), 0)

    @pl.when(g % 2 == 1)
    def _():
        lax.fori_loop(0, nq, functools.partial(quarter, wa1_ref, wa0_ref), 0)

    @pl.when((e_c == nec - 1) & (g >= 2))
    def _():
        tok = tok0 + lax.broadcasted_iota(jnp.int32, (tb, 1), 0)
        gate = jnp.where(tok < nctx, modc_ref[5:6, :], modl_ref[5:6, :])
        o_ref[...] = x_ref[...] + gate * acc_ref[...].T


def _peer_dense(xs, ht, c0, e0, r1, e1, mod, w, ncb_tokens):
    nb, t, d = xs.shape
    tb = TB_DENSE
    ne = EC_DENSE
    n_exp = w["u"].shape[0]
    assert t % tb == 0 and n_exp % ne == 0 and ne % (DENSE_QUARTERS * N_KEYS) == 0
    ntile, nec = t // tb, n_exp // ne
    nsteps = nb * ntile * nec
    pairs = N_KEYS // (ne // DENSE_QUARTERS // N_KEYS)

    def at(lag):
        def f(g):
            s = jnp.clip(g - lag, 0, nsteps - 1)
            return s // (ntile * nec), (s // nec) % ntile, s % nec
        return f

    sa, sb, sc = at(0), at(1), at(2)
    sspec = pl.BlockSpec((None, PEER_HEADS, N_KEYS, tb), lambda g: (sb(g)[0], 0, 0, sb(g)[1]))
    return pl.pallas_call(
        functools.partial(_peer_dense_kernel, nctx=ncb_tokens, nec=nec, ntile=ntile, nsteps=nsteps),
        grid=(nsteps + 2,),
        in_specs=[pl.BlockSpec((None, tb, d), lambda g: (sc(g)[0], sc(g)[1], 0)),
                  pl.BlockSpec((None, d, tb), lambda g: (sa(g)[0], 0, sa(g)[1])),
                  sspec, sspec, sspec, sspec,
                  pl.BlockSpec((None, 6, d), lambda g: (sc(g)[0], 0, 0)),
                  pl.BlockSpec((None, 6, d), lambda g: (nb, 0, 0)),
                  pl.BlockSpec((ne, d), lambda g: (sa(g)[2], 0)),
                  pl.BlockSpec((d, ne), lambda g: (0, sc(g)[2]))],
        out_specs=pl.BlockSpec((None, tb, d), lambda g: (sc(g)[0], sc(g)[1], 0)),
        out_shape=jax.ShapeDtypeStruct((nb, t, d), F32),
        scratch_shapes=[pltpu.VMEM((d, tb), F32),
                        pltpu.VMEM((PEER_HEADS, pairs, 8, tb), F32),
                        pltpu.VMEM((PEER_HEADS, pairs, 8, tb), F32),
                        pltpu.VMEM((2, ne, tb), F32),
                        pltpu.VMEM((ne, tb), BF16), pltpu.VMEM((ne, tb), BF16)],
        compiler_params=_params(1),
        name="peer_dense",
    )(xs, ht, c0, e0, r1, e1, mod, mod, w["u"], w["vt"])


def _final_norm_kernel(x_ref, g_ref, o_ref):
    x = x_ref[...]
    ms = jnp.mean(x * x, axis=-1, keepdims=True)
    o_ref[...] = x * lax.rsqrt(ms + NORM_EPS) * g_ref[...]


def _final_norm(xs, g, ncb_tokens):
    nb, t, d = xs.shape
    tb = TB_SCAN
    ncb = ncb_tokens // tb
    return pl.pallas_call(
        _final_norm_kernel,
        grid=(nb, t // tb - ncb),
        in_specs=[pl.BlockSpec((None, tb, d), lambda b, j: (b, j + ncb, 0)), _full(g)],
        out_specs=pl.BlockSpec((None, tb, d), lambda b, j: (b, j, 0)),
        out_shape=jax.ShapeDtypeStruct((nb, t - ncb_tokens, d), F32),
        compiler_params=_params(2),
        name="final_norm",
    )(xs, g)


def _row(a):
    return a.reshape(1, -1)


def _pad_dir(w2):
    z = jnp.zeros_like(w2[0])
    return jnp.stack([jnp.concatenate([w2[0], z], axis=0), jnp.concatenate([z, w2[1]], axis=0)]).astype(BF16)


def kernel(x, c, ctx, c_ctx, ada_w, ada_b, norm_mix, norm_ffn, rw_mix, rw_w_rkv, rw_w0, rw_w1, rw_w2, rw_a0, rw_a1, rw_a2, rw_g1, rw_g2, rw_k_k, rw_k_a, rw_r_k, rw_ln_w, rw_ln_b, rw_w_o, gla_w_in, gla_conv, gla_gk_up, gla_gk_b, gla_head_norm, gla_w_o, peer_wq, peer_keys, peer_u, peer_v, final_norm):
    nb, seq, d = x.shape
    nctx = ctx.shape[1]
    depth = ada_w.shape[0]
    assert nctx == TB_SCAN and seq % TB_SCAN == 0 and d % LANES == 0

    xs = jnp.concatenate([ctx, x], axis=1)
    cc = jnp.zeros((16, d), F32).at[:nb].set(c).at[nb].set(c_ctx)
    mods = _modulation(cc, ada_w, ada_b)[:, :nb + 1].reshape(depth, nb + 1, 6, d)

    bd_head = _block_diag_ones(RWKV_HEAD)
    ci = np.arange(GLA_CHUNK)
    tri = jnp.asarray(np.stack([ci[None, :] <= ci[:, None], ci[None, :] >= ci[:, None]]), dtype=BF16)

    for i in range(depth):
        mod = mods[i]
        j = i // 2
        if i % 2 == 0:
            w = dict(
                mix=rw_mix[j], w_rkv=rw_w_rkv[j].astype(BF16),
                w1=jnp.concatenate([rw_w1[j, 0], rw_w1[j, 1]], axis=1).astype(BF16),
                a1=jnp.concatenate([rw_a1[j, 0], rw_a1[j, 1]], axis=1).astype(BF16),
                g1=rw_g1[j].astype(BF16), w2=_pad_dir(rw_w2[j]), a2=_pad_dir(rw_a2[j]),
                g2=rw_g2[j].astype(BF16), w0=rw_w0[j], a0=rw_a0[j],
                k_k=_row(rw_k_k[j]), k_a=_row(rw_k_a[j]), r_k=_row(rw_r_k[j]),
                ln_w=_row(rw_ln_w[j]), ln_b=_row(rw_ln_b[j]), w_o=rw_w_o[j].astype(BF16), bd=bd_head)
            P = _rw_project(xs, mod, _row(norm_mix[i]), w, nctx)
            yf, yb = _rw_scan(P, bd_head, nctx)
            xs = _rw_readout(xs, yf, yb, P[3], P[4], mod, w, nctx)
        else:
            w_in = gla_w_in[j]
            qk_width = gla_gk_up.shape[-1] * 2
            conv_ch = qk_width + d
            dk = gla_gk_up.shape[-1] // GLA_HEADS
            dv = d // GLA_HEADS
            rank = gla_gk_up.shape[2]
            lr0 = conv_ch + d
            w_lr = jnp.zeros((d, LANES), F32).at[:, :2 * rank].set(w_in[:, lr0:lr0 + 2 * rank])
            up = jnp.zeros((LANES, d), F32)
            up = up.at[:rank, :qk_width // 2].set(gla_gk_up[j, 0]).at[rank:2 * rank, qk_width // 2:].set(gla_gk_up[j, 1])
            w = dict(w_qkv=w_in[:, :conv_ch].astype(BF16), w_g=w_in[:, conv_ch:lr0].astype(BF16),
                     w_lr=w_lr.astype(BF16), up=up.astype(BF16),
                     gk_b=jnp.concatenate([gla_gk_b[j, 0], gla_gk_b[j, 1]]).reshape(1, d),
                     head_norm=_row(gla_head_norm[j]), w_o=gla_w_o[j].astype(BF16))
            z, g, la = _gla_project(xs, mod, _row(norm_mix[i]), w, nctx)
            qkv = _gla_conv(z, gla_conv[j].reshape(9, conv_ch), nctx, qk_width, float(dk) ** -0.5)
            of, ob = _gla_scan(qkv, la, tri, nctx, dk, dv)
            xs = _gla_readout(xs, of, ob, g, mod, w, nctx, dv)

        keys = peer_keys[i].reshape(PEER_HEADS * 2, N_KEYS, -1)
        key_hi = keys.astype(BF16)
        pw = dict(wq=peer_wq[i].T.astype(BF16),
                  key_hi=key_hi, key_lo=(keys - key_hi.astype(F32)).astype(BF16),
                  u=peer_u[i].astype(BF16), vt=peer_v[i].T.astype(BF16))
        ht, c0, e0, r1, e1 = _peer_route(xs, mod, _row(norm_ffn[i]), pw, nctx)
        xs = _peer_dense(xs, ht, c0, e0, r1, e1, mod, pw, nctx)

    return _final_norm(xs, _row(final_norm), nctx)
```

```python
import functools

import numpy as np
import jax
import jax.numpy as jnp
from jax import lax
from jax.experimental import pallas as pl
from jax.experimental.pallas import tpu as pltpu

F32 = jnp.float32
BF16 = jnp.bfloat16

NORM_EPS = 1e-6
GRID_W = 64
RWKV_HEAD = 64
RWKV_GN_EPS = 64e-5
DECAY_LORA = 64
GLA_HEADS = 4
GLA_GATE_RANK = 16
GLA_GATE_NORM = 16.0
GLA_CHUNK = 64
PEER_HEADS = 8
N_KEYS = 128
PEER_TOPK = 16

LANES = 128
VMEM_LIMIT = 56 * 1024 * 1024

TB_PROJ = 128
TB_SCAN = 256
TB_ROUTE = 256
TB_DENSE = 256
EC_DENSE = 2048
DENSE_QUARTERS = 2


def _params(n_axes):
    return pltpu.CompilerParams(
        dimension_semantics=("arbitrary",) * n_axes,
        vmem_limit_bytes=VMEM_LIMIT)


def _full(a):
    nd = a.ndim
    return pl.BlockSpec(a.shape, lambda *_: (0,) * nd)


def _sigmoid(x):
    return 0.5 + 0.5 * jnp.tanh(0.5 * x)


def _silu(x):
    return x * _sigmoid(x)


def _log_sigmoid(x):
    return jnp.minimum(x, 0.0) - jnp.log(1.0 + jnp.exp(-jnp.abs(x)))


def _modnorm(x, g, shift, scale):
    ms = jnp.mean(x * x, axis=-1, keepdims=True)
    return x * lax.rsqrt(ms + NORM_EPS) * (g * (1.0 + scale)) + shift


def _split2(x):
    hi = x.astype(BF16)
    lo = (x - hi.astype(F32)).astype(BF16)
    return hi, lo


def _split3(x):
    x1 = x.astype(BF16)
    r1 = x - x1.astype(F32)
    x2 = r1.astype(BF16)
    x3 = (r1 - x2.astype(F32)).astype(BF16)
    return x1, x2, x3


def _dot(a, b):
    return jnp.dot(a, b, preferred_element_type=F32)


def _dot_nt(a, b):
    return lax.dot_general(a, b, (((1,), (1,)), ((), ())), preferred_element_type=F32)


def _dot_tn(a, b):
    return lax.dot_general(a, b, (((0,), (0,)), ((), ())), preferred_element_type=F32)


def _dot3(a_hi, a_lo, b_hi, b_lo):
    return _dot(a_hi, b_hi) + _dot(a_hi, b_lo) + _dot(a_lo, b_hi)


def _seg_sum(x, bd):
    outs = []
    for c in range(x.shape[-1] // LANES):
        hi, lo = _split2(x[:, c * LANES:(c + 1) * LANES])
        outs.append(_dot(hi, bd) + _dot(lo, bd))
    return jnp.concatenate(outs, axis=-1)


def _block_diag_ones(group):
    i = np.arange(LANES)
    return jnp.asarray((i[:, None] // group) == (i[None, :] // group), dtype=BF16)


def _mod_kernel(c_ref, w_ref, b_ref, o_ref):
    a = _silu(c_ref[...])
    o_ref[...] = jnp.dot(a, w_ref[...], preferred_element_type=F32,
                         precision=lax.Precision.HIGHEST) + b_ref[...]


def _modulation(cc, ada_w, ada_b):
    depth, d, n = ada_w.shape
    nt = 768
    return pl.pallas_call(
        _mod_kernel,
        grid=(depth, n // nt),
        in_specs=[pl.BlockSpec(cc.shape, lambda i, k: (0, 0)),
                  pl.BlockSpec((None, d, nt), lambda i, k: (i, 0, k)),
                  pl.BlockSpec((None, 1, nt), lambda i, k: (i, 0, k))],
        out_specs=pl.BlockSpec((None, cc.shape[0], nt), lambda i, k: (i, 0, k)),
        out_shape=jax.ShapeDtypeStruct((depth, cc.shape[0], n), F32),
        compiler_params=_params(2),
        name="modulation",
    )(cc, ada_w, ada_b.reshape(depth, 1, n))


def _mod_spec(nb, d, ncb):
    return pl.BlockSpec((None, 6, d), lambda b, j: (jnp.where(j < ncb, nb, b), 0, 0))


def _rw_proj_kernel(x_ref, xp_ref, xn_ref, mod_ref, ng_ref, mix_ref, wrkv_ref,
                    w1_ref, a1_ref, g1_ref, w2_ref, a2_ref, g2_ref, w0_ref, a0_ref,
                    kk_w_ref, ka_ref, rk_ref, bd_ref,
                    r_o, v_o, kk_o, g_o, bon_o, w0_o, k0_o, b0_o, w1_o, k1_o, b1_o,
                    *, ncb, nblk):
    j = pl.program_id(1)
    tb = x_ref.shape[0]
    shift = mod_ref[0:1, :]
    scale = mod_ref[1:2, :]
    g = ng_ref[...]
    h = _modnorm(x_ref[...], g, shift, scale)
    hp = _modnorm(xp_ref[7:8, :], g, shift, scale)
    hn = _modnorm(xn_ref[0:1, :], g, shift, scale)
    hp = jnp.where((j != 0) & (j != ncb), hp, 0.0)
    hn = jnp.where((j != ncb - 1) & (j != nblk - 1), hn, 0.0)
    row = lax.broadcasted_iota(jnp.int32, (tb, 1), 0)
    prev = jnp.where(row == 0, hp, pltpu.roll(h, 1, 0))
    nxt = jnp.where(row == tb - 1, hn, pltpu.roll(h, tb - 1, 0))
    xx = 0.5 * (prev + nxt) - h
    xr, xw, xk, xv, xa, xg = (h + xx * mix_ref[i:i + 1, :] for i in range(6))

    r = _dot(xr.astype(BF16), wrkv_ref[0])
    k = _dot(xk.astype(BF16), wrkv_ref[1])
    v = _dot(xv.astype(BF16), wrkv_ref[2])
    gate = _dot(_sigmoid(_dot(xg.astype(BF16), g1_ref[...])).astype(BF16), g2_ref[...])
    tw = jnp.tanh(_dot(xw.astype(BF16), w1_ref[...])).astype(BF16)
    ta = _dot(xa.astype(BF16), a1_ref[...]).astype(BF16)

    bd = bd_ref[...]
    kk = k * kk_w_ref[...]
    kk = kk / jnp.maximum(jnp.sqrt(_seg_sum(kk * kk, bd)), 1e-12)

    r_o[...] = r
    v_o[...] = v
    kk_o[...] = kk
    g_o[...] = gate
    ksum = None
    for d, (w_o, k_o, b_o) in enumerate(((w0_o, k0_o, b0_o), (w1_o, k1_o, b1_o))):
        pre = w0_ref[d:d + 1, :] + _dot(tw, w2_ref[d])
        w_o[...] = jnp.exp(-_sigmoid(pre) * float(np.exp(-0.5)))
        a = _sigmoid(a0_ref[d:d + 1, :] + _dot(ta, a2_ref[d]))
        kd = k * (1.0 + (a - 1.0) * ka_ref[...])
        k_o[...] = kd
        b_o[...] = kk * a
        ksum = kd if ksum is None else ksum + kd
    bon_o[...] = _seg_sum(r * ksum * rk_ref[...], bd) * v


def _rw_project(xs, mod, ng, w, ncb_tokens):
    nb, t, d = xs.shape
    tb = TB_PROJ
    nblk, ncb = t // tb, ncb_tokens // tb
    sl = tb // 8
    weights = [ng, w["mix"], w["w_rkv"], w["w1"], w["a1"], w["g1"], w["w2"], w["a2"], w["g2"],
               w["w0"], w["a0"], w["k_k"], w["k_a"], w["r_k"], w["bd"]]
    tok = pl.BlockSpec((None, tb, d), lambda b, j: (b, j, 0))
    return pl.pallas_call(
        functools.partial(_rw_proj_kernel, ncb=ncb, nblk=nblk),
        grid=(nb, nblk),
        in_specs=[tok,
                  pl.BlockSpec((None, 8, d), lambda b, j: (b, jnp.maximum(j * sl - 1, 0), 0)),
                  pl.BlockSpec((None, 8, d), lambda b, j: (b, jnp.minimum((j + 1) * sl, t // 8 - 1), 0)),
                  _mod_spec(nb, d, ncb)] + [_full(a) for a in weights],
        out_specs=[tok] * 11,
        out_shape=[jax.ShapeDtypeStruct((nb, t, d), F32)] * 11,
        compiler_params=_params(2),
        name="rwkv_project",
    )(xs, xs, xs, mod, *weights)


def _rw_scan_kernel(rf, wf, kf, vf, kkf, bf, rb, wb, kb, vb, kkb, bb, bd_ref,
                    yf_o, yb_o, s_ref):
    tb = rf.shape[0]
    nch = rf.shape[1] // LANES

    @pl.when(pl.program_id(1) == 0)
    def _():
        s_ref[...] = jnp.zeros_like(s_ref)

    bd = bd_ref[...]
    hw = RWKV_HEAD
    vi = lax.broadcasted_iota(jnp.int32, (hw, LANES), 0)
    li = lax.broadcasted_iota(jnp.int32, (hw, LANES), 1)
    diag_bf = ((li & (hw - 1)) == vi).astype(F32).astype(BF16)
    lane = lax.broadcasted_iota(jnp.int32, (1, LANES), 1)
    first_head = lane < hw
    head_mask = (first_head.astype(F32), 1.0 - first_head.astype(F32))
    sub = 64

    gch = 4
    groups = [(d, list(range(c0, c0 + gch))) for c0 in range(0, nch, gch) for d in (0, 1)]
    fwd = (rf, wf, kf, vf, kkf, bf)
    bwd = (rb, wb, kb, vb, kkb, bb)

    def rows_of(s, i):
        return s[i * hw:(i + 1) * hw]

    def body(g, carry):
        base = (pl.multiple_of(g * sub, sub), pl.multiple_of(tb - sub - g * sub, sub))
        lanes = [slice(c * LANES, (c + 1) * LANES) for c in range(nch)]
        tiles = [[[ref[pl.ds(base[d], sub), ls] for ref in refs] for ls in lanes]
                 for d, refs in enumerate((fwd, bwd))]

        def rows(d, c, n):
            m = n if d == 0 else sub - 1 - n
            return [t[m:m + 1] for t in tiles[d][c]]

        def tile_bf(row):
            t = jnp.broadcast_to(row, (2 * 8, LANES)).astype(BF16)
            return jnp.concatenate([t] * (hw // 16), axis=0)

        def start(gi, sb, n):
            d, cs = groups[gi]
            vcol = _dot(jnp.concatenate([tile_bf(rows(d, c, n)[3]) * diag_bf for c in cs], axis=0), bd)
            sa = _dot(sb * jnp.concatenate([tile_bf(rows(d, c, n)[4]) for c in cs], axis=0), bd)
            return vcol, sa

        state = [[s_ref[d, c] for c in cs] for d, cs in groups]
        pend = [start(gi, jnp.concatenate(state[gi], axis=0).astype(BF16), 0) for gi in range(len(groups))]
        ys = [[[] for _ in range(nch)] for _ in range(2)]
        for n in range(sub):
            for gi, (d, cs) in enumerate(groups):
                vcol, sa = pend[gi]
                s = []
                for i, c in enumerate(cs):
                    r, w, k, v, kk, b = rows(d, c, n)
                    s.append(state[gi][i] * w - rows_of(sa, i) * b + rows_of(vcol, i) * k)
                state[gi] = s
                sb = jnp.concatenate(s, axis=0).astype(BF16)
                if n + 1 < sub:
                    pend[gi] = start(gi, sb, n + 1)
                lhs = jnp.concatenate([rows(d, c, n)[0] * head_mask[h] for c in cs for h in range(2)], axis=0)
                out = _dot_nt(lhs.astype(BF16), sb)
                for m in range(gch // 2):
                    x = out[:, m * LANES:(m + 1) * LANES]
                    xr = pltpu.roll(x, hw, 1)
                    r0 = 4 * m
                    ys[d][cs[2 * m]].append(jnp.where(first_head, x[r0:r0 + 1], xr[r0 + 1:r0 + 2]))
                    ys[d][cs[2 * m + 1]].append(jnp.where(first_head, xr[r0 + 2:r0 + 3], x[r0 + 3:r0 + 4]))
        for gi, (d, cs) in enumerate(groups):
            for i, c in enumerate(cs):
                s_ref[d, c] = state[gi][i]
        for c, ls in enumerate(lanes):
            yf_o[pl.ds(base[0], sub), ls] = jnp.concatenate(ys[0][c], axis=0)
            yb_o[pl.ds(base[1], sub), ls] = jnp.concatenate(ys[1][c][::-1], axis=0)
        return carry

    lax.fori_loop(0, tb // sub, body, 0)


def _bwd_block(j, ncb, nblk):
    return jnp.where(j < ncb, ncb - 1 - j, nblk - 1 - (j - ncb))


def _rw_scan(P, bd, ncb_tokens):
    r, v, kk, _, _, w0, k0, b0, w1, k1, b1 = P
    nb, t, d = r.shape
    tb = TB_SCAN
    nblk, ncb = t // tb, ncb_tokens // tb
    fs = pl.BlockSpec((None, tb, d), lambda b, j: (b, j, 0))
    bs = pl.BlockSpec((None, tb, d), lambda b, j: (b, _bwd_block(j, ncb, nblk), 0))
    return pl.pallas_call(
        _rw_scan_kernel,
        grid=(nb, nblk),
        in_specs=[fs] * 6 + [bs] * 6 + [_full(bd)],
        out_specs=[fs, bs],
        out_shape=[jax.ShapeDtypeStruct((nb, t, d), F32)] * 2,
        scratch_shapes=[pltpu.VMEM((2, d // LANES, RWKV_HEAD, LANES), F32)],
        compiler_params=_params(2),
        name="rwkv_scan",
    )(r, w0, k0, v, kk, b0, r, w1, k1, v, kk, b1, bd)


def _rw_out_kernel(x_ref, yf_ref, yb_ref, g_ref, bon_ref, mod_ref, lnw_ref, lnb_ref, wo_ref, bd_ref, o_ref):
    bd = bd_ref[...]
    y = yf_ref[...] + yb_ref[...]
    mu = _seg_sum(y, bd) * (1.0 / RWKV_HEAD)
    yc = y - mu
    var = _seg_sum(yc * yc, bd) * (1.0 / RWKV_HEAD)
    yn = yc * lax.rsqrt(var + RWKV_GN_EPS) * lnw_ref[...] + lnb_ref[...]
    z = ((yn + bon_ref[...]) * g_ref[...]).astype(BF16)
    o_ref[...] = x_ref[...] + mod_ref[2:3, :] * _dot(z, wo_ref[...])


def _rw_readout(xs, yf, yb, gate, bon, mod, w, ncb_tokens):
    nb, t, d = xs.shape
    tb = TB_SCAN
    tok = pl.BlockSpec((None, tb, d), lambda b, j: (b, j, 0))
    weights = [w["ln_w"], w["ln_b"], w["w_o"], w["bd"]]
    return pl.pallas_call(
        _rw_out_kernel,
        grid=(nb, t // tb),
        in_specs=[tok] * 5 + [_mod_spec(nb, d, ncb_tokens // tb)] + [_full(a) for a in weights],
        out_specs=tok,
        out_shape=jax.ShapeDtypeStruct((nb, t, d), F32),
        compiler_params=_params(2),
        name="rwkv_readout",
    )(xs, yf, yb, gate, bon, mod, *weights)


def _gla_proj_kernel(x_ref, mod_ref, ng_ref, wqkv_ref, wg_ref, wlr_ref, up_ref, gkb_ref,
                     z_o, g_o, la_o):
    h = _modnorm(x_ref[...], ng_ref[...], mod_ref[0:1, :], mod_ref[1:2, :])
    hb = h.astype(BF16)
    z_o[...] = _dot(hb, wqkv_ref[...])
    g_o[...] = _dot(hb, wg_ref[...])
    lr = _dot(hb, wlr_ref[...]).astype(BF16)
    la_o[...] = _log_sigmoid(_dot(lr, up_ref[...]) + gkb_ref[...]) * (1.0 / GLA_GATE_NORM)


def _gla_project(xs, mod, ng, w, ncb_tokens):
    nb, t, d = xs.shape
    tb = TB_SCAN
    weights = [ng, w["w_qkv"], w["w_g"], w["w_lr"], w["up"], w["gk_b"]]
    tok = pl.BlockSpec((None, tb, d), lambda b, j: (b, j, 0))
    tok2 = pl.BlockSpec((None, tb, 2 * d), lambda b, j: (b, j, 0))
    return pl.pallas_call(
        _gla_proj_kernel,
        grid=(nb, t // tb),
        in_specs=[tok, _mod_spec(nb, d, ncb_tokens // tb)] + [_full(a) for a in weights],
        out_specs=[tok2, tok, tok],
        out_shape=[jax.ShapeDtypeStruct((nb, t, 2 * d), F32),
                   jax.ShapeDtypeStruct((nb, t, d), F32),
                   jax.ShapeDtypeStruct((nb, t, d), F32)],
        compiler_params=_params(2),
        name="gla_project",
    )(xs, mod, *weights)


def _gla_conv_kernel(zc_ref, zp_ref, zn_ref, cw_ref, o_ref, *, nblk, qk_width, q_scale):
    j = pl.program_id(1)
    tb, ch = zc_ref.shape
    is_ctx = j == 0
    up_ok = jnp.where(j > 1, 1.0, 0.0)
    dn_ok = jnp.where((j > 0) & (j < nblk - 1), 1.0, 0.0)
    vert = jnp.where(is_ctx, 0.0, 1.0)
    ext = tb + 2 * GRID_W
    pos = lax.broadcasted_iota(jnp.int32, (ext, 1), 0) - GRID_W
    col = jnp.where(is_ctx, pos, pos & (GRID_W - 1))
    no_left = col == 0
    no_right = col == jnp.where(is_ctx, tb - 1, GRID_W - 1)
    cc = 256
    for c in range(ch // cc):
        cs = slice(c * cc, (c + 1) * cc)
        e = jnp.concatenate([zp_ref[:, cs] * up_ok, zc_ref[:, cs], zn_ref[:, cs] * dn_ok], axis=0)
        em = jnp.where(no_left, 0.0, pltpu.roll(e, 1, 0))
        ep = jnp.where(no_right, 0.0, pltpu.roll(e, ext - 1, 0))
        acc = None
        for dy in range(3):
            lo = dy * GRID_W
            for dx, src in enumerate((em, e, ep)):
                wt = cw_ref[dy * 3 + dx:dy * 3 + dx + 1, cs]
                if dy != 1:
                    wt = wt * vert
                term = src[lo:lo + tb] * wt
                acc = term if acc is None else acc + term
        y = _silu(acc)
        if (c + 1) * cc <= qk_width // 2:
            y = y * q_scale
        o_ref[:, cs] = y


def _gla_conv(z, cw, ncb_tokens, qk_width, q_scale):
    nb, t, ch = z.shape
    tb = TB_SCAN
    assert ncb_tokens == tb and tb % GRID_W == 0
    sl = tb // GRID_W
    nblk = t // tb
    return pl.pallas_call(
        functools.partial(_gla_conv_kernel, nblk=nblk, qk_width=qk_width, q_scale=q_scale),
        grid=(nb, nblk),
        in_specs=[pl.BlockSpec((None, tb, ch), lambda b, j: (b, j, 0)),
                  pl.BlockSpec((None, GRID_W, ch), lambda b, j: (b, jnp.maximum(j * sl - 1, 0), 0)),
                  pl.BlockSpec((None, GRID_W, ch),
                               lambda b, j: (b, jnp.minimum((j + 1) * sl, t // GRID_W - 1), 0)),
                  _full(cw)],
        out_specs=pl.BlockSpec((None, tb, ch), lambda b, j: (b, j, 0)),
        out_shape=jax.ShapeDtypeStruct((nb, t, ch), F32),
        compiler_params=_params(2),
        name="gla_conv",
    )(z, z, z, cw)


def _gla_scan_kernel(qf_ref, laf_ref, qb_ref, lab_ref, tri_ref, of_o, ob_o, st_ref, *, dk, dv):
    tb = qf_ref.shape[0]
    c = GLA_CHUNK
    nh = GLA_HEADS
    qkw = nh * dk

    @pl.when(pl.program_id(1) == 0)
    def _():
        st_ref[...] = jnp.zeros_like(st_ref)

    dirs = ((qf_ref, laf_ref, of_o), (qb_ref, lab_ref, ob_o))
    nchunk = tb // c
    for p in range(nchunk):
        units = [(d, hh) for d in range(2) for hh in range(nh)]

        def load(d, hh):
            q_ref, la_ref, _ = dirs[d]
            n = p if d == 0 else nchunk - 1 - p
            rows = slice(n * c, (n + 1) * c)
            return (rows, q_ref[rows, hh * dk:(hh + 1) * dk], q_ref[rows, qkw + hh * dk:qkw + (hh + 1) * dk],
                    q_ref[rows, 2 * qkw + hh * dv:2 * qkw + (hh + 1) * dv].astype(BF16),
                    la_ref[rows, d * qkw + hh * dk:d * qkw + (hh + 1) * dk])

        data = [load(d, hh) for d, hh in units]
        bcums = []
        for (d, hh), (_, _, _, _, la) in zip(units, data):
            tri = tri_ref[d]
            l1, l2, l3 = _split3(la)
            bcums.append(_dot(tri, l1) + _dot(tri, l2) + _dot(tri, l3))
        intra = []
        for (d, hh), (_, q, k, _, _), bcum in zip(units, data, bcums):
            ref_row = c // 2 if d == 0 else c - 1 - c // 2
            ref = bcum[ref_row:ref_row + 1]
            a = _dot_nt((q * jnp.exp(bcum - ref)).astype(BF16), (k * jnp.exp(ref - bcum)).astype(BF16))
            intra.append(jnp.where(tri_ref[d] > 0, a, 0.0).astype(BF16))
        for (d, hh), (rows, q, k, vb, _), bcum, a in zip(units, data, bcums, intra):
            last_row = c - 1 if d == 0 else 0
            last = bcum[last_row:last_row + 1]
            st = st_ref[d, hh]
            o = _dot(a, vb) + _dot_nt((q * jnp.exp(bcum)).astype(BF16), st.astype(BF16))
            dirs[d][2][rows, hh * dv:(hh + 1) * dv] = o
            kd = (k * jnp.exp(last - bcum)).astype(BF16)
            st_ref[d, hh] = st * jnp.exp(last) + _dot_tn(vb, kd)


def _gla_scan(qkv, la, tri, ncb_tokens, dk, dv):
    nb, t, ch = qkv.shape
    d = la.shape[-1]
    tb = TB_SCAN
    nblk, ncb = t // tb, ncb_tokens // tb
    f2 = pl.BlockSpec((None, tb, ch), lambda b, j: (b, j, 0))
    b2 = pl.BlockSpec((None, tb, ch), lambda b, j: (b, _bwd_block(j, ncb, nblk), 0))
    f1 = pl.BlockSpec((None, tb, d), lambda b, j: (b, j, 0))
    b1 = pl.BlockSpec((None, tb, d), lambda b, j: (b, _bwd_block(j, ncb, nblk), 0))
    return pl.pallas_call(
        functools.partial(_gla_scan_kernel, dk=dk, dv=dv),
        grid=(nb, nblk),
        in_specs=[f2, f1, b2, b1, _full(tri)],
        out_specs=[f1, b1],
        out_shape=[jax.ShapeDtypeStruct((nb, t, d), F32)] * 2,
        scratch_shapes=[pltpu.VMEM((2, GLA_HEADS, dv, dk), F32)],
        compiler_params=_params(2),
        name="gla_scan",
    )(qkv, la, qkv, la, tri)


def _gla_out_kernel(x_ref, of_ref, ob_ref, g_ref, mod_ref, hn_ref, wo_ref, o_ref, *, dv):
    o = of_ref[...] + ob_ref[...]
    parts = []
    for hh in range(GLA_HEADS):
        oh = o[:, hh * dv:(hh + 1) * dv]
        ms = jnp.mean(oh * oh, axis=-1, keepdims=True)
        parts.append(oh * lax.rsqrt(ms + NORM_EPS) * hn_ref[...])
    on = jnp.concatenate(parts, axis=-1)
    z = (on * _silu(g_ref[...])).astype(BF16)
    o_ref[...] = x_ref[...] + mod_ref[2:3, :] * _dot(z, wo_ref[...])


def _gla_readout(xs, of, ob, g, mod, w, ncb_tokens, dv):
    nb, t, d = xs.shape
    tb = TB_SCAN
    tok = pl.BlockSpec((None, tb, d), lambda b, j: (b, j, 0))
    weights = [w["head_norm"], w["w_o"]]
    return pl.pallas_call(
        functools.partial(_gla_out_kernel, dv=dv),
        grid=(nb, t // tb),
        in_specs=[tok] * 4 + [_mod_spec(nb, d, ncb_tokens // tb)] + [_full(a) for a in weights],
        out_specs=tok,
        out_shape=jax.ShapeDtypeStruct((nb, t, d), F32),
        compiler_params=_params(2),
        name="gla_readout",
    )(xs, of, ob, g, mod, *weights)


def _top_values(s, k):
    vals = []
    cur = s
    for _ in range(k):
        m = jnp.max(cur, axis=0, keepdims=True)
        vals.append(m)
        cur = jnp.where(cur == m, -jnp.inf, cur)
    return vals


def _sorting_network(n):
    pairs = []
    p = 1
    while p < n:
        k = p
        while k >= 1:
            for j in range(k % p, n - k, 2 * k):
                for i in range(min(k, n - j - k)):
                    if (i + j) // (2 * p) == (i + j + k) // (2 * p):
                        pairs.append((i + j, i + j + k))
            k //= 2
        p *= 2
    return pairs


def _top_values_tiled(s, k):
    sub = 8
    n = s.shape[0] // sub
    lists = [s[g * sub:(g + 1) * sub] for g in range(n)]
    for a, b in _sorting_network(n):
        hi, lo = jnp.maximum(lists[a], lists[b]), jnp.minimum(lists[a], lists[b])
        lists[a], lists[b] = hi, lo
    lists.append(jnp.full_like(lists[0], -jnp.inf))
    vals = []
    for it in range(k):
        m = jnp.max(lists[0], axis=0, keepdims=True)
        vals.append(m)
        took = lists[0] == m
        for r in range(min(n, k - it - 1)):
            lists[r] = jnp.where(took, lists[r + 1], lists[r])
    return vals


def _peer_route_kernel(x_ref, mod_ref, ng_ref, wq_ref, key_hi_ref, key_lo_ref,
                       ht_o, c0_o, e0_o, r1_o, e1_o, q_hi_ref, q_lo_ref):
    h = _modnorm(x_ref[...], ng_ref[...], mod_ref[3:4, :], mod_ref[4:5, :])
    hb = h.T.astype(BF16)
    ht_o[...] = hb
    qt = _dot(wq_ref[...], hb)
    q_hi, q_lo = _split2(qt)
    q_hi_ref[...] = q_hi
    q_lo_ref[...] = q_lo
    nk = N_KEYS
    tb = x_ref.shape[0]

    def head(hd, carry):
        sv, sc = [], []
        for p in range(2):
            hp = hd * 2 + p
            rows = pl.ds(pl.multiple_of(hp * nk, nk), nk)
            s = _dot3(key_hi_ref[hp], key_lo_ref[hp], q_hi_ref[rows, :], q_lo_ref[rows, :])
            sc.append(s)
            sv.append(_top_values_tiled(s, PEER_TOPK + 1))
        k, half = PEER_TOPK, PEER_TOPK // 2
        sv0 = jnp.concatenate(sv[0][:k], axis=0)
        sv1 = jnp.concatenate(sv[1][:k], axis=0)
        edge = jnp.concatenate([sv[0][k] + sv[1][0], sv[0][0] + sv[1][k],
                                jnp.full((6, tb), -jnp.inf, F32)], axis=0)
        cand = jnp.concatenate([sv[0][0] + sv1]
                               + [sv[0][a] + sv1[:half] for a in range(1, half)]
                               + [sv0[half:] + sv[1][0], edge], axis=0)
        top = _top_values(cand, k + 1)
        z = top[0] * 0.0
        for cval in top[:k]:
            z = z + jnp.exp(cval - top[0])
        tau = 0.5 * (top[k - 1] + top[k])
        theta = tau - sc[0]
        c0 = jnp.zeros_like(theta)
        r1 = jnp.full_like(theta, float(k))
        for a in range(k):
            c0 = jnp.where(sv[1][a] >= theta, float(a + 1), c0)
            r1 = jnp.where(sc[1] == sv[1][a], float(a), r1)
        c0_o[hd] = c0
        r1_o[hd] = r1.astype(BF16)
        e0_o[hd] = jnp.exp(sc[0] - sv[0][0]) * (1.0 / z)
        e1_o[hd] = jnp.exp(sc[1] - sv[1][0]).astype(BF16)
        return carry

    per_trip = 4

    def head_group(i, carry):
        for hh in range(per_trip):
            head(per_trip * i + hh, carry)
        return carry

    lax.fori_loop(0, PEER_HEADS // per_trip, head_group, 0)


def _peer_route(xs, mod, ng, w, ncb_tokens):
    nb, t, d = xs.shape
    tb = TB_ROUTE
    nq = w["wq"].shape[0]
    weights = [ng, w["wq"], w["key_hi"], w["key_lo"]]
    sspec = pl.BlockSpec((None, PEER_HEADS, N_KEYS, tb), lambda b, j: (b, 0, 0, j))
    return pl.pallas_call(
        _peer_route_kernel,
        grid=(nb, t // tb),
        in_specs=[pl.BlockSpec((None, tb, d), lambda b, j: (b, j, 0)),
                  _mod_spec(nb, d, ncb_tokens // tb)] + [_full(a) for a in weights],
        out_specs=[pl.BlockSpec((None, d, tb), lambda b, j: (b, 0, j)), sspec, sspec, sspec, sspec],
        out_shape=[jax.ShapeDtypeStruct((nb, d, t), BF16),
                   jax.ShapeDtypeStruct((nb, PEER_HEADS, N_KEYS, t), F32),
                   jax.ShapeDtypeStruct((nb, PEER_HEADS, N_KEYS, t), F32),
                   jax.ShapeDtypeStruct((nb, PEER_HEADS, N_KEYS, t), BF16),
                   jax.ShapeDtypeStruct((nb, PEER_HEADS, N_KEYS, t), BF16)],
        scratch_shapes=[pltpu.VMEM((nq, tb), BF16), pltpu.VMEM((nq, tb), BF16)],
        compiler_params=_params(2),
        name="peer_route",
    )(xs, mod, *weights)


def _peer_dense_kernel(x_ref, ht_ref, c0_ref, e0_ref, r1_ref, e1_ref, modl_ref, modc_ref, u_ref, vt_ref, o_ref,
                       acc_ref, cp_ref, ep_ref, act_ref, wa0_ref, wa1_ref,
                       *, nctx, nec, ntile, nsteps):
    g = pl.program_id(0)
    ne = u_ref.shape[0]
    d, tb = acc_ref.shape
    nk = N_KEYS
    nq = DENSE_QUARTERS
    qe, qd = ne // nq, d // nq
    qi = qe // nk
    g_b = jnp.clip(g - 1, 0, nsteps - 1)
    g_c = jnp.clip(g - 2, 0, nsteps - 1)
    e_b = g_b % nec
    e_c = g_c % nec
    tok0 = ((g_c // nec) % ntile) * tb

    @pl.when(g == 0)
    def _():
        for ref in (act_ref, wa0_ref, wa1_ref, cp_ref, ep_ref):
            ref[...] = jnp.zeros_like(ref)

    @pl.when(e_b == 0)
    def _():
        for hd in range(PEER_HEADS):
            c0 = c0_ref[hd]
            e0 = e0_ref[hd]
            for p in range(nk // qi):
                cp_ref[hd, p, 0:qi, :] = c0[p * qi:(p + 1) * qi]
                ep_ref[hd, p, 0:qi, :] = e0[p * qi:(p + 1) * qi]

    @pl.when(e_c == 0)
    def _():
        acc_ref[...] = jnp.zeros_like(acc_ref)

    jr = 32

    new, old = g % 2, (g + 1) % 2

    def quarter(wa_new, wa_old, q, carry):
        r0 = pl.multiple_of(q * qe, qe)
        d0 = pl.multiple_of(q * qd, qd)
        pair = e_b * nq + q
        nj = nk // jr
        ka, kc = d // nj, ne // nj
        act_ref[new, pl.ds(r0, qe), :] = jnp.zeros((qe, tb), F32)
        half = jr // 2
        cb, eb = [], []
        for hd in range(PEER_HEADS):
            ct = cp_ref[hd, pair]
            et = ep_ref[hd, pair]
            for ii in range(qi):
                c16 = jnp.broadcast_to(ct[ii:ii + 1], (half, tb)).astype(BF16)
                e16 = jnp.broadcast_to(et[ii:ii + 1], (half, tb)).astype(BF16)
                cb.append(jnp.concatenate([c16, c16], axis=0))
                eb.append(jnp.concatenate([e16, e16], axis=0))
        for jq in range(nj):
            act_ref[new, pl.ds(r0, qe), :] += _dot(u_ref[pl.ds(r0, qe), jq * ka:(jq + 1) * ka],
                                                   ht_ref[jq * ka:(jq + 1) * ka, :])
            acc_ref[pl.ds(d0, qd), :] += _dot(vt_ref[pl.ds(d0, qd), jq * kc:(jq + 1) * kc],
                                              wa_old[jq * kc:(jq + 1) * kc, :])
            rows = slice(jq * jr, (jq + 1) * jr)
            accs = [None] * qi
            for hd in range(PEER_HEADS):
                r1 = r1_ref[hd, rows, :]
                e1 = e1_ref[hd, rows, :]
                for ii in range(qi):
                    term = jnp.where(r1 < cb[hd * qi + ii], e1, jnp.zeros_like(e1)) * eb[hd * qi + ii]
                    accs[ii] = term if accs[ii] is None else accs[ii] + term
            for ii in range(qi):
                er = pl.ds(pl.multiple_of(r0 + ii * nk + jq * jr, jr), jr)
                a = act_ref[old, er, :]
                act = 0.5 * a * (1.0 + lax.erf(a * float(1.0 / np.sqrt(2.0))))
                wa_new[er, :] = (accs[ii].astype(F32) * act).astype(BF16)
        return carry

    @pl.when(g % 2 == 0)
    def _():
        lax.fori_loop(0, nq, functools.partial(quarter, wa0_ref, wa1_ref), 0)

    @pl.when(g % 2 == 1)
    def _():
        lax.fori_loop(0, nq, functools.partial(quarter, wa1_ref, wa0_ref), 0)

    @pl.when((e_c == nec - 1) & (g >= 2))
    def _():
        tok = tok0 + lax.broadcasted_iota(jnp.int32, (tb, 1), 0)
        gate = jnp.where(tok < nctx, modc_ref[5:6, :], modl_ref[5:6, :])
        o_ref[...] = x_ref[...] + gate * acc_ref[...].T


def _peer_dense(xs, ht, c0, e0, r1, e1, mod, w, ncb_tokens):
    nb, t, d = xs.shape
    tb = TB_DENSE
    ne = EC_DENSE
    n_exp = w["u"].shape[0]
    assert t % tb == 0 and n_exp % ne == 0 and ne % (DENSE_QUARTERS * N_KEYS) == 0
    ntile, nec = t // tb, n_exp // ne
    nsteps = nb * ntile * nec
    pairs = N_KEYS // (ne // DENSE_QUARTERS // N_KEYS)

    def at(lag):
        def f(g):
            s = jnp.clip(g - lag, 0, nsteps - 1)
            return s // (ntile * nec), (s // nec) % ntile, s % nec
        return f

    sa, sb, sc = at(0), at(1), at(2)
    sspec = pl.BlockSpec((None, PEER_HEADS, N_KEYS, tb), lambda g: (sb(g)[0], 0, 0, sb(g)[1]))
    return pl.pallas_call(
        functools.partial(_peer_dense_kernel, nctx=ncb_tokens, nec=nec, ntile=ntile, nsteps=nsteps),
        grid=(nsteps + 2,),
        in_specs=[pl.BlockSpec((None, tb, d), lambda g: (sc(g)[0], sc(g)[1], 0)),
                  pl.BlockSpec((None, d, tb), lambda g: (sa(g)[0], 0, sa(g)[1])),
                  sspec, sspec, sspec, sspec,
                  pl.BlockSpec((None, 6, d), lambda g: (sc(g)[0], 0, 0)),
                  pl.BlockSpec((None, 6, d), lambda g: (nb, 0, 0)),
                  pl.BlockSpec((ne, d), lambda g: (sa(g)[2], 0)),
                  pl.BlockSpec((d, ne), lambda g: (0, sc(g)[2]))],
        out_specs=pl.BlockSpec((None, tb, d), lambda g: (sc(g)[0], sc(g)[1], 0)),
        out_shape=jax.ShapeDtypeStruct((nb, t, d), F32),
        scratch_shapes=[pltpu.VMEM((d, tb), F32),
                        pltpu.VMEM((PEER_HEADS, pairs, 8, tb), F32),
                        pltpu.VMEM((PEER_HEADS, pairs, 8, tb), F32),
                        pltpu.VMEM((2, ne, tb), F32),
                        pltpu.VMEM((ne, tb), BF16), pltpu.VMEM((ne, tb), BF16)],
        compiler_params=_params(1),
        name="peer_dense",
    )(xs, ht, c0, e0, r1, e1, mod, mod, w["u"], w["vt"])


def _final_norm_kernel(x_ref, g_ref, o_ref):
    x = x_ref[...]
    ms = jnp.mean(x * x, axis=-1, keepdims=True)
    o_ref[...] = x * lax.rsqrt(ms + NORM_EPS) * g_ref[...]


def _final_norm(xs, g, ncb_tokens):
    nb, t, d = xs.shape
    tb = TB_SCAN
    ncb = ncb_tokens // tb
    return pl.pallas_call(
        _final_norm_kernel,
        grid=(nb, t // tb - ncb),
        in_specs=[pl.BlockSpec((None, tb, d), lambda b, j: (b, j + ncb, 0)), _full(g)],
        out_specs=pl.BlockSpec((None, tb, d), lambda b, j: (b, j, 0)),
        out_shape=jax.ShapeDtypeStruct((nb, t - ncb_tokens, d), F32),
        compiler_params=_params(2),
        name="final_norm",
    )(xs, g)


def _row(a):
    return a.reshape(1, -1)


def _pad_dir(w2):
    z = jnp.zeros_like(w2[0])
    return jnp.stack([jnp.concatenate([w2[0], z], axis=0), jnp.concatenate([z, w2[1]], axis=0)]).astype(BF16)


def kernel(x, c, ctx, c_ctx, ada_w, ada_b, norm_mix, norm_ffn, rw_mix, rw_w_rkv, rw_w0, rw_w1, rw_w2, rw_a0, rw_a1, rw_a2, rw_g1, rw_g2, rw_k_k, rw_k_a, rw_r_k, rw_ln_w, rw_ln_b, rw_w_o, gla_w_in, gla_conv, gla_gk_up, gla_gk_b, gla_head_norm, gla_w_o, peer_wq, peer_keys, peer_u, peer_v, final_norm):
    nb, seq, d = x.shape
    nctx = ctx.shape[1]
    depth = ada_w.shape[0]
    assert nctx == TB_SCAN and seq % TB_SCAN == 0 and d % LANES == 0

    xs = jnp.concatenate([ctx, x], axis=1)
    cc = jnp.zeros((16, d), F32).at[:nb].set(c).at[nb].set(c_ctx)
    mods = _modulation(cc, ada_w, ada_b)[:, :nb + 1].reshape(depth, nb + 1, 6, d)

    bd_head = _block_diag_ones(RWKV_HEAD)
    ci = np.arange(GLA_CHUNK)
    tri = jnp.asarray(np.stack([ci[None, :] <= ci[:, None], ci[None, :] >= ci[:, None]]), dtype=BF16)

    for i in range(depth):
        mod = mods[i]
        j = i // 2
        if i % 2 == 0:
            w = dict(
                mix=rw_mix[j], w_rkv=rw_w_rkv[j].astype(BF16),
                w1=jnp.concatenate([rw_w1[j, 0], rw_w1[j, 1]], axis=1).astype(BF16),
                a1=jnp.concatenate([rw_a1[j, 0], rw_a1[j, 1]], axis=1).astype(BF16),
                g1=rw_g1[j].astype(BF16), w2=_pad_dir(rw_w2[j]), a2=_pad_dir(rw_a2[j]),
                g2=rw_g2[j].astype(BF16), w0=rw_w0[j], a0=rw_a0[j],
                k_k=_row(rw_k_k[j]), k_a=_row(rw_k_a[j]), r_k=_row(rw_r_k[j]),
                ln_w=_row(rw_ln_w[j]), ln_b=_row(rw_ln_b[j]), w_o=rw_w_o[j].astype(BF16), bd=bd_head)
            P = _rw_project(xs, mod, _row(norm_mix[i]), w, nctx)
            yf, yb = _rw_scan(P, bd_head, nctx)
            xs = _rw_readout(xs, yf, yb, P[3], P[4], mod, w, nctx)
        else:
            w_in = gla_w_in[j]
            qk_width = gla_gk_up.shape[-1] * 2
            conv_ch = qk_width + d
            dk = gla_gk_up.shape[-1] // GLA_HEADS
            dv = d // GLA_HEADS
            rank = gla_gk_up.shape[2]
            lr0 = conv_ch + d
            w_lr = jnp.zeros((d, LANES), F32).at[:, :2 * rank].set(w_in[:, lr0:lr0 + 2 * rank])
            up = jnp.zeros((LANES, d), F32)
            up = up.at[:rank, :qk_width // 2].set(gla_gk_up[j, 0]).at[rank:2 * rank, qk_width // 2:].set(gla_gk_up[j, 1])
            w = dict(w_qkv=w_in[:, :conv_ch].astype(BF16), w_g=w_in[:, conv_ch:lr0].astype(BF16),
                     w_lr=w_lr.astype(BF16), up=up.astype(BF16),
                     gk_b=jnp.concatenate([gla_gk_b[j, 0], gla_gk_b[j, 1]]).reshape(1, d),
                     head_norm=_row(gla_head_norm[j]), w_o=gla_w_o[j].astype(BF16))
            z, g, la = _gla_project(xs, mod, _row(norm_mix[i]), w, nctx)
            qkv = _gla_conv(z, gla_conv[j].reshape(9, conv_ch), nctx, qk_width, float(dk) ** -0.5)
            of, ob = _gla_scan(qkv, la, tri, nctx, dk, dv)
            xs = _gla_readout(xs, of, ob, g, mod, w, nctx, dv)

        keys = peer_keys[i].reshape(PEER_HEADS * 2, N_KEYS, -1)
        key_hi = keys.astype(BF16)
        pw = dict(wq=peer_wq[i].T.astype(BF16),
                  key_hi=key_hi, key_lo=(keys - key_hi.astype(F32)).astype(BF16),
                  u=peer_u[i].astype(BF16), vt=peer_v[i].T.astype(BF16))
        ht, c0, e0, r1, e1 = _peer_route(xs, mod, _row(norm_ffn[i]), pw, nctx)
        xs = _peer_dense(xs, ht, c0, e0, r1, e1, mod, pw, nctx)

    return _final_norm(xs, _row(final_norm), nctx)
```
